```python
import math
import numpy as np
import jax
import jax.numpy as jnp
from jax import lax

D_MODEL = 1024
BATCH = 2
SEQ = 8192
DEPTH = 2

NORM_EPS = 1e-6
D_FF = 2816
FFN_RES_SCALE = 0.5

REL_BUCKETS = 32
REL_MAX_DIST = 128

GDN_HEADS = 4
GDN_DK = 128
GDN_DV = 128
GDN_CONV = 4
GDN_CHUNK = 64

NSA_HEADS = 8
NSA_GROUPS = 2
NSA_DK = 96
NSA_DV = 64
CMP_LEN = 32
CMP_STRIDE = 16
CMP_HIDDEN = 256
SEL_LEN = 64
SEL_TOPK = 16
WINDOW = 512
Q_BLOCK = 128

GLA_HEADS = 4
GLA_DK = 64
GLA_DV = 128
GLA_RANK = 16
GLA_TAU = 16
GLA_CHUNK = 16

N_BRANCH = 3
MASK_VALUE = -1e30

IN_COLUMNS = (
    ('gdn_q', GDN_HEADS * GDN_DK),
    ('gdn_k', GDN_HEADS * GDN_DK),
    ('gdn_v', GDN_HEADS * GDN_DV),
    ('gdn_z', GDN_HEADS * GDN_DV),
    ('gdn_b', GDN_HEADS),
    ('gdn_a', GDN_HEADS),
    ('nsa_q', NSA_HEADS * NSA_DK),
    ('nsa_kv_cmp', NSA_GROUPS * (NSA_DK + NSA_DV)),
    ('nsa_kv_sel', NSA_GROUPS * (NSA_DK + NSA_DV)),
    ('nsa_kv_win', NSA_GROUPS * (NSA_DK + NSA_DV)),
    ('nsa_gate', 3 * NSA_HEADS),
    ('gla_q', GLA_HEADS * GLA_DK),
    ('gla_k', GLA_HEADS * GLA_DK),
    ('gla_v', GLA_HEADS * GLA_DV),
    ('gla_r', GLA_HEADS * GLA_DV),
    ('gla_a', GLA_RANK),
    ('merge_gate', N_BRANCH * D_MODEL),
)
D_IN = sum(width for _, width in IN_COLUMNS)

kernel_name = "hybrid_gdn_nsa_gla_macaron"


def rms_norm(x, w, eps=NORM_EPS):
    xf = x.astype(jnp.float32)
    y = xf * lax.rsqrt(jnp.mean(xf * xf, axis=-1, keepdims=True) + eps)
    return (y * w.astype(jnp.float32)).astype(x.dtype)


def l2_normalize(t, eps=1e-6):
    t = t.astype(jnp.float32)
    return t * lax.rsqrt(jnp.sum(t * t, axis=-1, keepdims=True) + eps)


def swiglu(h, w_gate_up, w_down):
    g, u = jnp.split(h @ w_gate_up, 2, axis=-1)
    return (jax.nn.silu(g) * u) @ w_down


def macaron_ffn(x, norm_pre, w_gate_up, w_down, norm_post):
    y = swiglu(rms_norm(x, norm_pre), w_gate_up, w_down)
    return x + FFN_RES_SCALE * rms_norm(y, norm_post)


def split_columns(p):
    out = {}
    off = 0
    for name, width in IN_COLUMNS:
        out[name] = p[..., off:off + width]
        off += width
    return out


def rel_bucket(dist):
    n = jnp.maximum(dist, 0)
    max_exact = REL_BUCKETS // 2
    nf = jnp.maximum(n, 1).astype(jnp.float32)
    large = max_exact + (jnp.log(nf / max_exact) / math.log(REL_MAX_DIST / max_exact)
                         * (REL_BUCKETS - max_exact)).astype(jnp.int32)
    large = jnp.minimum(large, REL_BUCKETS - 1)
    return jnp.where(n < max_exact, n, large)


def causal_depthwise_conv(x, w):
    k_width, ch = w.shape
    return lax.conv_general_dilated(
        x, w[:, None, :].astype(x.dtype), window_strides=(1,),
        padding=[(k_width - 1, 0)], dimension_numbers=('NWC', 'WIO', 'NWC'),
        feature_group_count=ch)


def gated_deltanet(q, k, v, z, b, a, conv_w, a_log, dt_bias, norm_w):
    B, S, _ = q.shape
    H, dk, dv, C = GDN_HEADS, GDN_DK, GDN_DV, GDN_CHUNK
    N = S // C
    f32 = jnp.float32
    qkv = jax.nn.silu(causal_depthwise_conv(jnp.concatenate([q, k, v], axis=-1), conv_w))
    q, k, v = jnp.split(qkv, [H * dk, 2 * H * dk], axis=-1)
    q = l2_normalize(q.reshape(B, S, H, dk)) * dk ** -0.5
    k = l2_normalize(k.reshape(B, S, H, dk))
    v = v.reshape(B, S, H, dv)
    beta = jax.nn.sigmoid(b.astype(f32))
    g = -jnp.exp(a_log.astype(f32)) * jax.nn.softplus(a.astype(f32) + dt_bias.astype(f32))

    def chunk(t):
        return t.reshape(B, N, C, H, -1).transpose(0, 3, 1, 2, 4).astype(f32)

    qc, kc, vc = chunk(q), chunk(k), chunk(v)
    gc = jnp.cumsum(chunk(g[..., None])[..., 0], axis=-1)
    bc = chunk(beta[..., None])[..., 0]
    lower = jnp.tril(jnp.ones((C, C), bool))
    decay = jnp.exp(jnp.where(lower, gc[..., :, None] - gc[..., None, :], -jnp.inf))
    eye = jnp.eye(C, dtype=f32)
    kb = kc * bc[..., None]
    lmat = jnp.einsum('bhnid,bhnjd->bhnij', kb, kc) * decay * (1.0 - eye)
    rhs = jnp.concatenate([vc * bc[..., None], kb * jnp.exp(gc)[..., None]], axis=-1)
    sol = lax.linalg.triangular_solve(lmat + eye, rhs, left_side=True, lower=True)
    u, w = sol[..., :dv], sol[..., dv:]
    attn = jnp.einsum('bhnid,bhnjd->bhnij', qc, kc) * decay
    q_dec = qc * jnp.exp(gc)[..., None]
    k_dec = kc * jnp.exp(gc[..., -1:] - gc)[..., None]
    c_dec = jnp.exp(gc[..., -1])

    def step(state, xs):
        attn_n, u_n, w_n, q_n, k_n, d_n = xs
        v_new = u_n - jnp.einsum('bhcd,bhde->bhce', w_n, state)
        o = jnp.einsum('bhcd,bhde->bhce', q_n, state) + jnp.einsum('bhij,bhje->bhie', attn_n, v_new)
        state = state * d_n[..., None, None] + jnp.einsum('bhcd,bhce->bhde', k_n, v_new)
        return state, o

    xs = tuple(jnp.moveaxis(t, 2, 0) for t in (attn, u, w, q_dec, k_dec, c_dec))
    _, o = lax.scan(step, jnp.zeros((B, H, dk, dv), f32), xs)
    o = jnp.moveaxis(o, 0, 2).transpose(0, 2, 3, 1, 4).reshape(B, S, H, dv)
    o = rms_norm(o, norm_w) * jax.nn.silu(z.reshape(B, S, H, dv).astype(f32))
    return o.reshape(B, S, H * dv)


def compress_blocks(t, pe, w1, w2):
    B, S, G, d = t.shape
    n_seg = S // CMP_STRIDE
    r = CMP_LEN // CMP_STRIDE
    n_cmp = n_seg - r + 1
    seg = t.reshape(B, n_seg, CMP_STRIDE, G, d)
    blocks = jnp.concatenate([seg[:, j:j + n_cmp] for j in range(r)], axis=2)
    blocks = blocks + pe[None, None, :, None, :]
    flat = blocks.transpose(0, 1, 3, 2, 4).reshape(B, n_cmp, G, CMP_LEN * d)
    return jax.nn.gelu(flat @ w1) @ w2


def native_sparse_attention(q, kv_cmp, kv_sel, kv_win, gate_logits, rel_bias,
                            pe_k, w1_k, w2_k, pe_v, w1_v, w2_v):
    B, S, _ = q.shape
    G, HPG = NSA_GROUPS, NSA_HEADS // NSA_GROUPS
    dk, dv = NSA_DK, NSA_DV
    f32 = jnp.float32
    n_q = S // Q_BLOCK
    n_cmp = (S - CMP_LEN) // CMP_STRIDE + 1
    n_sel = S // SEL_LEN
    k_top = min(SEL_TOPK, n_sel)
    scale = dk ** -0.5

    def split_kv(kv):
        kv = kv.reshape(B, S, G, dk + dv)
        return kv[..., :dk], kv[..., dk:]

    k_c, v_c = split_kv(kv_cmp)
    k_s, v_s = split_kv(kv_sel)
    k_w, v_w = split_kv(kv_win)
    k_cmp = compress_blocks(k_c, pe_k, w1_k, w2_k)
    v_cmp = compress_blocks(v_c, pe_v, w1_v, w2_v)
    k_blk = k_s.reshape(B, n_sel, SEL_LEN, G, dk).transpose(0, 3, 1, 2, 4)
    v_blk = v_s.reshape(B, n_sel, SEL_LEN, G, dv).transpose(0, 3, 1, 2, 4)
    pad = ((0, 0), (WINDOW, 0), (0, 0), (0, 0))
    k_wp, v_wp = jnp.pad(k_w, pad), jnp.pad(v_w, pad)

    cmp_end = jnp.arange(n_cmp) * CMP_STRIDE + CMP_LEN - 1
    c_start = np.arange(n_cmp) * CMP_STRIDE
    s_start = np.arange(n_sel) * SEL_LEN
    overlap = jnp.asarray(((c_start[:, None] < s_start[None, :] + SEL_LEN)
                           & (c_start[:, None] + CMP_LEN > s_start[None, :])).astype(np.float32))
    tbl = rel_bias.astype(f32)
    tbl_g = tbl.reshape(REL_BUCKETS, G, HPG).transpose(1, 0, 2)
    bi = jnp.arange(B)[:, None, None, None]
    gi = jnp.arange(G)[None, None, :, None]
    sel_ids = jnp.arange(n_sel)
    kw_off = jnp.arange(WINDOW + Q_BLOCK) - WINDOW

    def head_bias(dist):
        bias = tbl[rel_bucket(dist)]
        return bias.reshape(dist.shape[0], dist.shape[1], G, HPG).transpose(2, 3, 0, 1)

    def block(args):
        i, qb, gb = args
        q0 = i * Q_BLOCK
        t = q0 + jnp.arange(Q_BLOCK)
        dist_c = t[:, None] - cmp_end[None, :]
        valid_c = dist_c >= 0
        s_c = jnp.einsum('bghqd,bngd->bghqn', qb, k_cmp).astype(f32) * scale + head_bias(dist_c)
        p_c = jax.nn.softmax(jnp.where(valid_c, s_c, MASK_VALUE), axis=-1)
        p_c = p_c * jnp.any(valid_c, axis=-1)[:, None]
        o_c = jnp.einsum('bghqn,bngd->bqghd', p_c.astype(v_cmp.dtype), v_cmp)
        imp = jnp.einsum('bghqn,ns->bqgs', p_c, overlap)
        blk_t = (t // SEL_LEN)[:, None]
        forced = (sel_ids == 0) | (sel_ids == blk_t) | (sel_ids == blk_t - 1)
        future = sel_ids > blk_t
        imp = jnp.where(forced[:, None, :], jnp.inf, jnp.where(future[:, None, :], -jnp.inf, imp))
        _, sel = lax.top_k(imp, k_top)
        k_g = k_blk[bi, gi, sel].reshape(B, Q_BLOCK, G, k_top * SEL_LEN, dk)
        v_g = v_blk[bi, gi, sel].reshape(B, Q_BLOCK, G, k_top * SEL_LEN, dv)
        pos = (sel[..., None] * SEL_LEN + jnp.arange(SEL_LEN)).reshape(B, Q_BLOCK, G, k_top * SEL_LEN)
        dist_s = t[None, :, None, None] - pos
        bias_s = tbl_g[gi, rel_bucket(dist_s)].transpose(0, 2, 4, 1, 3)
        s_s = jnp.einsum('bghqd,bqgsd->bghqs', qb, k_g).astype(f32) * scale + bias_s
        valid_s = (dist_s >= 0).transpose(0, 2, 1, 3)[:, :, None]
        p_s = jax.nn.softmax(jnp.where(valid_s, s_s, MASK_VALUE), axis=-1)
        o_s = jnp.einsum('bghqs,bqgsd->bqghd', p_s.astype(v_g.dtype), v_g)
        k_wb = lax.dynamic_slice_in_dim(k_wp, q0, WINDOW + Q_BLOCK, axis=1)
        v_wb = lax.dynamic_slice_in_dim(v_wp, q0, WINDOW + Q_BLOCK, axis=1)
        kpos = q0 + kw_off
        dist_w = t[:, None] - kpos[None, :]
        valid_w = (dist_w >= 0) & (dist_w < WINDOW) & (kpos[None, :] >= 0)
        s_w = jnp.einsum('bghqd,bkgd->bghqk', qb, k_wb).astype(f32) * scale + head_bias(dist_w)
        p_w = jax.nn.softmax(jnp.where(valid_w, s_w, MASK_VALUE), axis=-1)
        o_w = jnp.einsum('bghqk,bkgd->bqghd', p_w.astype(v_wb.dtype), v_wb)
        gb = gb.reshape(B, Q_BLOCK, G, HPG, 3)
        o = gb[..., 0:1] * o_c + gb[..., 1:2] * o_s + gb[..., 2:3] * o_w
        return o.reshape(B, Q_BLOCK, NSA_HEADS * dv)

    qb = q.reshape(B, n_q, Q_BLOCK, G, HPG, dk).transpose(1, 0, 3, 4, 2, 5)
    gates = jax.nn.sigmoid(gate_logits.astype(f32)).reshape(B, S, NSA_HEADS, 3)
    gb = gates.reshape(B, n_q, Q_BLOCK, NSA_HEADS, 3).transpose(1, 0, 2, 3, 4)
    out = lax.map(block, (jnp.arange(n_q), qb, gb))
    return out.transpose(1, 0, 2, 3).reshape(B, S, NSA_HEADS * dv)


def gated_linear_attention(q, k, v, r, a_low, gate_w, gate_b, norm_w):
    B, S, _ = q.shape
    H, dk, dv, C = GLA_HEADS, GLA_DK, GLA_DV, GLA_CHUNK
    N = S // C
    f32 = jnp.float32
    log_a = jax.nn.log_sigmoid((a_low @ gate_w + gate_b).astype(f32)) / GLA_TAU

    def chunk(t, d):
        return t.reshape(B, N, C, H, d).transpose(0, 3, 1, 2, 4).astype(f32)

    qc = chunk(q, dk) * dk ** -0.5
    kc = chunk(k, dk)
    vc = chunk(v, dv)
    bc = jnp.cumsum(chunk(log_a, dk), axis=-2)
    lower = jnp.tril(jnp.ones((C, C), bool))[..., None]
    dec = jnp.exp(jnp.where(lower, bc[..., :, None, :] - bc[..., None, :, :], -jnp.inf))
    attn = jnp.einsum('bhnid,bhnjd,bhnijd->bhnij', qc, kc, dec)
    o_intra = jnp.einsum('bhnij,bhnje->bhnie', attn, vc)
    q_dec = qc * jnp.exp(bc)
    k_dec = kc * jnp.exp(bc[..., -1:, :] - bc)
    c_dec = jnp.exp(bc[..., -1, :])

    def step(state, xs):
        q_n, k_n, v_n, d_n = xs
        o = jnp.einsum('bhcd,bhde->bhce', q_n, state)
        state = state * d_n[..., None] + jnp.einsum('bhcd,bhce->bhde', k_n, v_n)
        return state, o

    xs = tuple(jnp.moveaxis(t, 2, 0) for t in (q_dec, k_dec, vc, c_dec))
    _, o_inter = lax.scan(step, jnp.zeros((B, H, dk, dv), f32), xs)
    o = jnp.moveaxis(o_inter, 0, 2) + o_intra
    o = o.transpose(0, 2, 3, 1, 4).reshape(B, S, H, dv)
    o = rms_norm(o, norm_w) * jax.nn.silu(r.reshape(B, S, H, dv).astype(f32))
    return o.reshape(B, S, H * dv)


def token_mixing(h, rel_bias, w_in, gdn_conv_w, gdn_a_log, gdn_dt_bias, gdn_norm_w,
                 nsa_pe_k, nsa_cmp_k_w1, nsa_cmp_k_w2, nsa_pe_v, nsa_cmp_v_w1, nsa_cmp_v_w2,
                 gla_gate_w, gla_gate_b, gla_norm_w,
                 w_branch_gdn, w_branch_nsa, w_branch_gla, w_out):
    B, S, D = h.shape
    p = split_columns(h @ w_in)
    o_a = gated_deltanet(p['gdn_q'], p['gdn_k'], p['gdn_v'], p['gdn_z'], p['gdn_b'], p['gdn_a'],
                         gdn_conv_w, gdn_a_log, gdn_dt_bias, gdn_norm_w).astype(h.dtype)
    o_b = native_sparse_attention(p['nsa_q'], p['nsa_kv_cmp'], p['nsa_kv_sel'], p['nsa_kv_win'],
                                  p['nsa_gate'], rel_bias, nsa_pe_k, nsa_cmp_k_w1, nsa_cmp_k_w2,
                                  nsa_pe_v, nsa_cmp_v_w1, nsa_cmp_v_w2).astype(h.dtype)
    o_c = gated_linear_attention(p['gla_q'], p['gla_k'], p['gla_v'], p['gla_r'], p['gla_a'],
                                 gla_gate_w, gla_gate_b, gla_norm_w).astype(h.dtype)
    gates = jax.nn.sigmoid(p['merge_gate'].astype(jnp.float32)).astype(h.dtype).reshape(B, S, N_BRANCH, D)
    merged = (gates[:, :, 0] * (o_a @ w_branch_gdn)
              + gates[:, :, 1] * (o_b @ w_branch_nsa)
              + gates[:, :, 2] * (o_c @ w_branch_gla))
    return merged @ w_out


def setup_inputs(seed: int = 0) -> dict:
    key = jax.random.key(seed)
    keys = iter(jax.random.split(key, 40))
    f32 = jnp.float32
    L = DEPTH

    def dense(shape, fan_in):
        return jax.random.normal(next(keys), shape, f32) * fan_in ** -0.5

    def gain(shape):
        return 1.0 + 0.05 * jax.random.normal(next(keys), shape, f32)

    x = jax.random.normal(next(keys), (BATCH, SEQ, D_MODEL), f32)
    rel_bias = 0.2 * jax.random.normal(next(keys), (REL_BUCKETS, NSA_HEADS), f32)
    gdn_a_log = jnp.log(jax.random.uniform(next(keys), (L, GDN_HEADS), f32, minval=1.0, maxval=16.0))
    dt = jnp.exp(jax.random.uniform(next(keys), (L, GDN_HEADS), f32,
                                    minval=math.log(1e-3), maxval=math.log(1e-1)))
    gdn_dt_bias = dt + jnp.log(-jnp.expm1(-dt))
    return {
        "x": x,
        "rel_bias": rel_bias,
        "ffn1_norm_pre": gain((L, D_MODEL)),
        "ffn1_w_gate_up": dense((L, D_MODEL, 2 * D_FF), D_MODEL),
        "ffn1_w_down": dense((L, D_FF, D_MODEL), D_FF),
        "ffn1_norm_post": gain((L, D_MODEL)),
        "mix_norm_pre": gain((L, D_MODEL)),
        "w_in": dense((L, D_MODEL, D_IN), D_MODEL),
        "gdn_conv_w": dense((L, GDN_CONV, GDN_HEADS * (2 * GDN_DK + GDN_DV)), GDN_CONV),
        "gdn_a_log": gdn_a_log,
        "gdn_dt_bias": gdn_dt_bias,
        "gdn_norm_w": gain((L, GDN_DV)),
        "nsa_pe_k": dense((L, CMP_LEN, NSA_DK), NSA_DK),
        "nsa_cmp_k_w1": dense((L, CMP_LEN * NSA_DK, CMP_HIDDEN), CMP_LEN * NSA_DK),
        "nsa_cmp_k_w2": dense((L, CMP_HIDDEN, NSA_DK), CMP_HIDDEN),
        "nsa_pe_v": dense((L, CMP_LEN, NSA_DV), NSA_DV),
        "nsa_cmp_v_w1": dense((L, CMP_LEN * NSA_DV, CMP_HIDDEN), CMP_LEN * NSA_DV),
        "nsa_cmp_v_w2": dense((L, CMP_HIDDEN, NSA_DV), CMP_HIDDEN),
        "gla_gate_w": dense((L, GLA_RANK, GLA_HEADS * GLA_DK), GLA_RANK),
        "gla_gate_b": 0.1 * jax.random.normal(next(keys), (L, GLA_HEADS * GLA_DK), f32),
        "gla_norm_w": gain((L, GLA_DV)),
        "w_branch_gdn": dense((L, GDN_HEADS * GDN_DV, D_MODEL), GDN_HEADS * GDN_DV),
        "w_branch_nsa": dense((L, NSA_HEADS * NSA_DV, D_MODEL), NSA_HEADS * NSA_DV),
        "w_branch_gla": dense((L, GLA_HEADS * GLA_DV, D_MODEL), GLA_HEADS * GLA_DV),
        "w_out": dense((L, D_MODEL, D_MODEL), D_MODEL),
        "mix_norm_post": gain((L, D_MODEL)),
        "ffn2_norm_pre": gain((L, D_MODEL)),
        "ffn2_w_gate_up": dense((L, D_MODEL, 2 * D_FF), D_MODEL),
        "ffn2_w_down": dense((L, D_FF, D_MODEL), D_FF),
        "ffn2_norm_post": gain((L, D_MODEL)),
    }


def reference(x, rel_bias,
              ffn1_norm_pre, ffn1_w_gate_up, ffn1_w_down, ffn1_norm_post,
              mix_norm_pre, w_in, gdn_conv_w, gdn_a_log, gdn_dt_bias, gdn_norm_w,
              nsa_pe_k, nsa_cmp_k_w1, nsa_cmp_k_w2, nsa_pe_v, nsa_cmp_v_w1, nsa_cmp_v_w2,
              gla_gate_w, gla_gate_b, gla_norm_w,
              w_branch_gdn, w_branch_nsa, w_branch_gla, w_out, mix_norm_post,
              ffn2_norm_pre, ffn2_w_gate_up, ffn2_w_down, ffn2_norm_post):
    for l in range(DEPTH):
        x = macaron_ffn(x, ffn1_norm_pre[l], ffn1_w_gate_up[l], ffn1_w_down[l], ffn1_norm_post[l])
        y = token_mixing(rms_norm(x, mix_norm_pre[l]), rel_bias, w_in[l],
                         gdn_conv_w[l], gdn_a_log[l], gdn_dt_bias[l], gdn_norm_w[l],
                         nsa_pe_k[l], nsa_cmp_k_w1[l], nsa_cmp_k_w2[l],
                         nsa_pe_v[l], nsa_cmp_v_w1[l], nsa_cmp_v_w2[l],
                         gla_gate_w[l], gla_gate_b[l], gla_norm_w[l],
                         w_branch_gdn[l], w_branch_nsa[l], w_branch_gla[l], w_out[l])
        x = x + rms_norm(y, mix_norm_post[l])
        x = macaron_ffn(x, ffn2_norm_pre[l], ffn2_w_gate_up[l], ffn2_w_down[l], ffn2_norm_post[l])
    return x
```

```python
import functools
import math

import jax
import jax.numpy as jnp
from jax import lax
from jax.experimental import pallas as pl
from jax.experimental.pallas import tpu as pltpu

F32 = jnp.float32
BF16 = jnp.bfloat16
HIGHEST = lax.Precision.HIGHEST

LANE = 128
VMEM_LIMIT_BYTES = 56 * 1024 * 1024

NORM_EPS = 1e-6
FFN_RES_SCALE = 0.5
REL_BUCKETS = 32
REL_MAX_DIST = 128
GDN_HEADS, GDN_DK, GDN_DV, GDN_CONV, GDN_CHUNK = 4, 128, 128, 4, 64
NSA_HEADS, NSA_GROUPS, NSA_DK, NSA_DV = 8, 2, 96, 64
NSA_HPG = NSA_HEADS // NSA_GROUPS
CMP_LEN, CMP_STRIDE, CMP_HIDDEN = 32, 16, 256
SEL_LEN, SEL_TOPK, WINDOW, Q_BLOCK = 64, 16, 512, 128
GLA_HEADS, GLA_DK, GLA_DV, GLA_RANK, GLA_TAU, GLA_CHUNK = 4, 64, 128, 16, 16, 16
N_BRANCH = 3
MASK_VALUE = -1e30
L2_EPS = 1e-6

CMP_PER_Q = Q_BLOCK // CMP_STRIDE
CMP_BAND_LO = -10
CMP_BAND = 17
CMP_BAND_PAD = 32


def _nt(a, b, **kw):
    return lax.dot_general(a, b, (((1,), (1,)), ((), ())), preferred_element_type=F32, **kw)


def _tn(a, b, **kw):
    return lax.dot_general(a, b, (((0,), (0,)), ((), ())), preferred_element_type=F32, **kw)


def _mm(a, b, **kw):
    return jnp.dot(a, b, preferred_element_type=F32, **kw)


def _rms(x, w):
    return x * lax.rsqrt(jnp.mean(x * x, axis=-1, keepdims=True) + NORM_EPS) * w


def _silu(x):
    return x * jax.nn.sigmoid(x)


def _softplus(x):
    return jnp.maximum(x, 0.0) + jnp.log1p(jnp.exp(-jnp.abs(x)))


def _params(*semantics):
    return pltpu.CompilerParams(dimension_semantics=semantics, vmem_limit_bytes=VMEM_LIMIT_BYTES)


def _resident(shape):
    return pl.BlockSpec(shape, lambda *_: (0,) * len(shape), pipeline_mode=pl.Buffered(1))


FFN_TOKENS = 512
FFN_CHUNK = 256


def _ffn_kernel(x_ref, npre_ref, wgu_ref, wd_ref, npost_ref, o_ref):
    d_ff = wd_ref.shape[0]
    x = x_ref[...]
    h = _rms(x, npre_ref[...]).astype(BF16)
    acc = jnp.zeros(x.shape, F32)
    for c in range(d_ff // FFN_CHUNK):
        lo = c * FFN_CHUNK
        g = _mm(h, wgu_ref[:, lo:lo + FFN_CHUNK])
        u = _mm(h, wgu_ref[:, d_ff + lo:d_ff + lo + FFN_CHUNK])
        a = (_silu(g) * u).astype(BF16)
        acc = acc + _mm(a, wd_ref[lo:lo + FFN_CHUNK, :])
    o_ref[...] = x + FFN_RES_SCALE * _rms(acc, npost_ref[...])


def _ffn(x, norm_pre, w_gate_up, w_down, norm_post):
    m, d = x.shape
    d_ff = w_down.shape[0]
    row = pl.BlockSpec((FFN_TOKENS, d), lambda i: (i, 0))
    return pl.pallas_call(
        _ffn_kernel,
        grid=(m // FFN_TOKENS,),
        in_specs=[row, _resident((1, d)), _resident((d, 2 * d_ff)), _resident((d_ff, d)), _resident((1, d))],
        out_specs=row,
        out_shape=jax.ShapeDtypeStruct((m, d), F32),
        compiler_params=_params("parallel"),
        name="ffn",
    )(x, norm_pre.reshape(1, d), w_gate_up.astype(BF16), w_down.astype(BF16), norm_post.reshape(1, d))


PROJ_TOKENS = 256
PROJ_CHUNK = 512


def _proj_kernel(x_ref, nw_ref, w_ref, *o_refs, segs):
    h = _rms(x_ref[...], nw_ref[...]).astype(BF16)
    for (off, width, epilogue), o_ref in zip(segs, o_refs):
        for lo in range(0, width, PROJ_CHUNK):
            hi = min(lo + PROJ_CHUNK, width)
            y = _mm(h, w_ref[:, off + lo:off + hi])
            if epilogue == "sigmoid":
                y = jax.nn.sigmoid(y)
            elif epilogue is not None:
                y = y * epilogue
            o_ref[:, lo:hi] = y.astype(o_ref.dtype)


def _proj(x, norm_w, pieces):
    m, d = x.shape
    segs, cols, off = [], [], 0
    for w, _, epilogue in pieces:
        width = w.shape[1]
        pad = (-width) % LANE
        segs.append((off, width, epilogue))
        cols.append(w)
        if pad:
            cols.append(jnp.zeros((d, pad), w.dtype))
        off += width + pad
    w_all = jnp.concatenate(cols, axis=1).astype(BF16)
    row = pl.BlockSpec((PROJ_TOKENS, d), lambda i: (i, 0))
    return pl.pallas_call(
        functools.partial(_proj_kernel, segs=tuple(segs)),
        grid=(m // PROJ_TOKENS,),
        in_specs=[row, _resident((1, d)), _resident((d, off))],
        out_specs=[pl.BlockSpec((PROJ_TOKENS, w.shape[1]), lambda i: (i, 0)) for w, _, _ in pieces],
        out_shape=[jax.ShapeDtypeStruct((m, w.shape[1]), dt) for w, dt, _ in pieces],
        compiler_params=_params("parallel"),
        name="in_proj",
    )(x, norm_w.reshape(1, d), w_all)


GDN_TOKENS = 256
CONV_PAD = 8


def _gdn_kernel(q_ref, k_ref, v_ref, z_ref, ba_ref, cwq_ref, cwk_ref, cwv_ref, alog_ref, dtb_ref, nw_ref,
                o_ref, xq_ref, xk_ref, xv_ref, state_ref):
    head = pl.program_id(1)
    t_blk = q_ref.shape[1]
    c_len = GDN_CHUNK

    @pl.when(pl.program_id(2) == 0)
    def _():
        state_ref[...] = jnp.zeros_like(state_ref)
        for x_ref in (xq_ref, xk_ref, xv_ref):
            x_ref[0:CONV_PAD, :] = jnp.zeros((CONV_PAD, x_ref.shape[1]), F32)

    def conv_silu(raw_ref, x_ref, cw_ref):
        x_ref[CONV_PAD:CONV_PAD + t_blk, :] = raw_ref[0]
        y = jnp.zeros((t_blk, x_ref.shape[1]), F32)
        for tap in range(GDN_CONV):
            y = y + cw_ref[tap:tap + 1, :] * x_ref[pl.ds(CONV_PAD - (GDN_CONV - 1) + tap, t_blk), :]
        x_ref[0:CONV_PAD, :] = x_ref[t_blk:t_blk + CONV_PAD, :]
        return _silu(y)

    q = conv_silu(q_ref, xq_ref, cwq_ref)
    k = conv_silu(k_ref, xk_ref, cwk_ref)
    v = conv_silu(v_ref, xv_ref, cwv_ref)
    q = q * lax.rsqrt(jnp.sum(q * q, axis=-1, keepdims=True) + L2_EPS) * (GDN_DK ** -0.5)
    k = k * lax.rsqrt(jnp.sum(k * k, axis=-1, keepdims=True) + L2_EPS)

    ba = ba_ref[0]
    g_all = -jnp.exp(alog_ref[...]) * _softplus(ba + dtb_ref[...])
    row = lax.broadcasted_iota(jnp.int32, (LANE, LANE), 0)
    pick_b = (row == head).astype(F32)
    pick_a = (row == head + GDN_HEADS).astype(F32)
    beta = jax.nn.sigmoid(_mm(ba, pick_b, precision=HIGHEST))
    g = _mm(g_all, pick_a, precision=HIGHEST)

    ci = lax.broadcasted_iota(jnp.int32, (c_len, c_len), 0)
    cj = lax.broadcasted_iota(jnp.int32, (c_len, c_len), 1)
    lower = ci >= cj
    strict = ci > cj
    tri = lower.astype(F32)
    eye = (ci == cj).astype(F32)
    lane0 = (lax.broadcasted_iota(jnp.int32, (c_len, LANE), 1) == 0).astype(F32)

    outs = []
    state = state_ref[...]
    for c in range(t_blk // c_len):
        rows = slice(c * c_len, (c + 1) * c_len)
        qc, kc, vc, bc = q[rows], k[rows], v[rows], beta[rows]
        gc = _mm(tri, g[rows], precision=HIGHEST)
        gc_row = _nt(lane0, gc, precision=HIGHEST)
        gc_col = gc[:, :c_len]
        decay = jnp.exp(jnp.where(lower, gc_col - gc_row, MASK_VALUE))
        kb = kc * bc
        lmat = jnp.where(strict, _nt(kb, kc, precision=HIGHEST) * decay, 0.0)
        pw = -lmat
        inv = eye + pw
        for _ in range(5):
            pw = _mm(pw, pw, precision=HIGHEST)
            inv = inv + _mm(inv, pw, precision=HIGHEST)
        e_gc = jnp.exp(gc)
        rhs = jnp.concatenate([vc * bc, kb * e_gc], axis=-1)
        sol = _mm(inv, rhs, precision=HIGHEST)
        u, w = sol[:, :GDN_DV], sol[:, GDN_DV:]
        attn = _nt(qc.astype(BF16), kc.astype(BF16)) * decay
        gc_last = gc[c_len - 1:c_len, :]
        q_dec = (qc * e_gc).astype(BF16)
        k_dec = (kc * jnp.exp(gc_last - gc)).astype(BF16)
        s16 = state.astype(BF16)
        v_new = u - _mm(w.astype(BF16), s16)
        v16 = v_new.astype(BF16)
        outs.append(_mm(q_dec, s16) + _mm(attn.astype(BF16), v16))
        state = state * jnp.exp(gc_last) + _tn(k_dec, v16)
    state_ref[...] = state
    o = jnp.concatenate(outs, axis=0)
    o_ref[0] = _rms(o, nw_ref[...]) * _silu(z_ref[0])


def _gdn(qkv, z, ba, conv_w, a_log, dt_bias, norm_w, batch):
    m = qkv.shape[0]
    s = m // batch
    h, dk, dv = GDN_HEADS, GDN_DK, GDN_DV
    qkv = qkv.reshape(batch, s, 3 * h * dk)
    tok = lambda col: pl.BlockSpec((1, GDN_TOKENS, LANE), lambda b, hh, i, col=col: (b, i, col + hh))
    cw = lambda col: pl.BlockSpec((GDN_CONV, LANE), lambda b, hh, i, col=col: (0, col + hh))
    lane_pad = jnp.zeros((LANE - 2 * h,), F32)
    alog_row = jnp.concatenate([jnp.zeros((h,), F32), a_log, lane_pad]).reshape(1, LANE)
    dtb_row = jnp.concatenate([jnp.zeros((h,), F32), dt_bias, lane_pad]).reshape(1, LANE)
    out = pl.pallas_call(
        _gdn_kernel,
        grid=(batch, h, s // GDN_TOKENS),
        in_specs=[tok(0), tok(h), tok(2 * h),
                  pl.BlockSpec((1, GDN_TOKENS, LANE), lambda b, hh, i: (b, i, hh)),
                  pl.BlockSpec((1, GDN_TOKENS, LANE), lambda b, hh, i: (b, i, 0)),
                  cw(0), cw(h), cw(2 * h),
                  _resident((1, LANE)), _resident((1, LANE)), _resident((1, dv))],
        out_specs=pl.BlockSpec((1, GDN_TOKENS, LANE), lambda b, hh, i: (b, i, hh)),
        out_shape=jax.ShapeDtypeStruct((batch, s, h * dv), F32),
        scratch_shapes=[pltpu.VMEM((GDN_TOKENS + CONV_PAD, LANE), F32)] * 3 + [pltpu.VMEM((dk, dv), F32)],
        compiler_params=_params("parallel", "parallel", "arbitrary"),
        name="gdn",
    )(qkv, qkv, qkv, z.reshape(batch, s, h * dv), ba.reshape(batch, s, LANE),
      conv_w, conv_w, conv_w, alog_row, dtb_row, norm_w.reshape(1, dv))
    return out.reshape(m, h * dv)


GLA_TOKENS = 256


def _log_sigmoid(x):
    return jnp.minimum(x, 0.0) - jnp.log1p(jnp.exp(-jnp.abs(x)))


def _gla_kernel(qk_ref, v_ref, r_ref, a_ref, gw_ref, gb_ref, nw_ref, o_ref, ks_ref, bs_ref, vs_ref, state_ref):
    t_blk = qk_ref.shape[1]
    c_len = GLA_CHUNK
    hdk = GLA_HEADS * GLA_DK
    hdv = GLA_HEADS * GLA_DV

    @pl.when(pl.program_id(1) == 0)
    def _():
        state_ref[...] = jnp.zeros_like(state_ref)
        ks_ref[0:c_len, :] = jnp.zeros((c_len, hdk), F32)
        bs_ref[0:c_len, :] = jnp.zeros((c_len, hdk), F32)
        vs_ref[0:c_len, :] = jnp.zeros((c_len, hdv), F32)

    qk = qk_ref[0]
    q = qk[:, :hdk] * (GLA_DK ** -0.5)
    k = qk[:, hdk:]
    v = v_ref[0]
    log_a = _log_sigmoid(_mm(a_ref[0].astype(BF16), gw_ref[...]) + gb_ref[...]) * (1.0 / GLA_TAU)

    ti = lax.broadcasted_iota(jnp.int32, (t_blk, t_blk), 0)
    tj = lax.broadcasted_iota(jnp.int32, (t_blk, t_blk), 1)
    same = (ti // c_len) == (tj // c_len)
    cum = _mm((same & (tj <= ti)).astype(F32), log_a, precision=HIGHEST)
    tot = _mm(same.astype(F32), log_a, precision=HIGHEST)

    ks_ref[c_len:, :] = k
    bs_ref[c_len:, :] = cum
    vs_ref[c_len:, :] = v
    pos = lax.broadcasted_iota(jnp.int32, (t_blk, 1), 0) % c_len
    hd = lax.broadcasted_iota(jnp.int32, (hdk, hdv), 0) // GLA_DK
    he = lax.broadcasted_iota(jnp.int32, (hdk, hdv), 1) // GLA_DV
    spread = (hd == he).astype(BF16)
    o = jnp.zeros((t_blk, hdv), F32)
    for off in range(c_len):
        k_o = ks_ref[pl.ds(c_len - off, t_blk), :]
        b_o = bs_ref[pl.ds(c_len - off, t_blk), :]
        v_o = vs_ref[pl.ds(c_len - off, t_blk), :]
        w = jnp.exp(jnp.where(pos >= off, cum - b_o, MASK_VALUE))
        o = o + _mm((q * k_o * w).astype(BF16), spread) * v_o

    q_dec = (q * jnp.exp(cum)).astype(BF16)
    k_dec = (k * jnp.exp(tot - cum)).astype(BF16)
    c_dec = jnp.exp(tot)
    v16 = v.astype(BF16)
    states = [state_ref[h] for h in range(GLA_HEADS)]
    inter = []
    for n in range(t_blk // c_len):
        rows = slice(n * c_len, (n + 1) * c_len)
        parts = []
        for h in range(GLA_HEADS):
            dks = slice(h * GLA_DK, (h + 1) * GLA_DK)
            dvs = slice(h * GLA_DV, (h + 1) * GLA_DV)
            parts.append(_nt(q_dec[rows, dks], states[h].astype(BF16)))
            states[h] = states[h] * c_dec[n * c_len:n * c_len + 1, dks] + _tn(v16[rows, dvs], k_dec[rows, dks])
        inter.append(jnp.concatenate(parts, axis=-1))
    for h in range(GLA_HEADS):
        state_ref[h] = states[h]
    o = o + jnp.concatenate(inter, axis=0)

    r = r_ref[0]
    nw = nw_ref[...]
    for h in range(GLA_HEADS):
        dvs = slice(h * GLA_DV, (h + 1) * GLA_DV)
        o_ref[0, :, dvs] = _rms(o[:, dvs], nw) * _silu(r[:, dvs])


def _gla(qk, v, r, a_low, gate_w, gate_b, norm_w, batch):
    m = qk.shape[0]
    s = m // batch
    hdk, hdv = GLA_HEADS * GLA_DK, GLA_HEADS * GLA_DV
    gw = jnp.zeros((LANE, hdk), F32).at[:GLA_RANK].set(gate_w).astype(BF16)
    blk = lambda w: pl.BlockSpec((1, GLA_TOKENS, w), lambda b, i: (b, i, 0))
    out = pl.pallas_call(
        _gla_kernel,
        grid=(batch, s // GLA_TOKENS),
        in_specs=[blk(2 * hdk), blk(hdv), blk(hdv), blk(LANE),
                  _resident((LANE, hdk)), _resident((1, hdk)), _resident((1, GLA_DV))],
        out_specs=blk(hdv),
        out_shape=jax.ShapeDtypeStruct((batch, s, hdv), F32),
        scratch_shapes=[pltpu.VMEM((GLA_TOKENS + GLA_CHUNK, hdk), F32),
                        pltpu.VMEM((GLA_TOKENS + GLA_CHUNK, hdk), F32),
                        pltpu.VMEM((GLA_TOKENS + GLA_CHUNK, hdv), F32),
                        pltpu.VMEM((GLA_HEADS, GLA_DV, GLA_DK), F32)],
        compiler_params=_params("parallel", "arbitrary"),
        name="gla",
    )(qk.reshape(batch, s, 2 * hdk), v.reshape(batch, s, hdv), r.reshape(batch, s, hdv),
      a_low.reshape(batch, s, LANE), gw, gate_b.reshape(1, hdk), norm_w.reshape(1, GLA_DV))
    return out.reshape(m, hdv)


def _gelu_tanh(x):
    return 0.5 * x * (1.0 + jnp.tanh(math.sqrt(2.0 / math.pi) * (x + 0.044715 * x * x * x)))


def _cmp_kernel(seg_ref, wa_ref, wb_ref, pek_ref, w1k_ref, pev_ref, w1v_ref, w2_ref, k_ref, v_ref):
    seg = seg_ref[0]
    n_seg = seg.shape[0]
    first = _mm(seg, wa_ref[...])
    second = _mm(seg, wb_ref[...])
    bias_k = _mm(pek_ref[...], w1k_ref[...])[0:1]
    bias_v = _mm(pev_ref[...], w1v_ref[...])[0:1]
    bias = jnp.concatenate([bias_k, bias_v] * NSA_GROUPS, axis=-1)
    hid = _gelu_tanh(first + pltpu.roll(second, n_seg - 1, 0) + bias).astype(BF16)
    out = _mm(hid, w2_ref[...])
    kw = NSA_GROUPS * LANE
    k_ref[0] = out[:, :kw].astype(BF16)
    v_ref[0] = out[:, kw:].astype(BF16)


def _cmp_weights(pe_k, w1k, w2k, pe_v, w1v, w2v):
    g, dk, dv, hid = NSA_GROUPS, NSA_DK, NSA_DV, CMP_HIDDEN
    w1k = w1k.reshape(CMP_LEN, dk, hid)
    w1v = w1v.reshape(CMP_LEN, dv, hid)

    def half(lo):
        blk = jnp.zeros((CMP_STRIDE, g, dk + dv, g, 2, hid), F32)
        for gg in range(g):
            blk = blk.at[:, gg, :dk, gg, 0].set(w1k[lo:lo + CMP_STRIDE])
            blk = blk.at[:, gg, dk:, gg, 1].set(w1v[lo:lo + CMP_STRIDE])
        return blk.reshape(CMP_STRIDE * g * (dk + dv), g * 2 * hid).astype(BF16)

    w2 = jnp.zeros((g, 2, hid, g * LANE + g * 2 * LANE), F32)
    for gg in range(g):
        w2 = w2.at[gg, 0, :, gg * LANE:gg * LANE + dk].set(w2k)
        base = g * LANE + gg * 2 * LANE
        w2 = w2.at[gg, 1, :, base:base + dv].set(w2v)
        w2 = w2.at[gg, 1, :, base + LANE + dv:base + 2 * LANE].set(w2v)
    w2 = w2.reshape(g * 2 * hid, -1).astype(BF16)
    pad8 = lambda pe: jnp.zeros((8, pe.size), F32).at[0].set(pe.reshape(-1)).astype(BF16)
    return half(0), half(CMP_STRIDE), pad8(pe_k), w1k.reshape(-1, hid).astype(BF16), pad8(pe_v), \
        w1v.reshape(-1, hid).astype(BF16), w2


def _compress(kv_cmp, weights, batch):
    m, width = kv_cmp.shape
    s = m // batch
    n_seg = s // CMP_STRIDE
    seg = kv_cmp.reshape(batch, n_seg, CMP_STRIDE * width)
    wa, wb, pek, w1k, pev, w1v, w2 = weights
    full = lambda a: _resident(a.shape)
    kw, vw = NSA_GROUPS * LANE, NSA_GROUPS * 2 * LANE
    return pl.pallas_call(
        _cmp_kernel,
        grid=(batch,),
        in_specs=[pl.BlockSpec((1, n_seg, CMP_STRIDE * width), lambda b: (b, 0, 0)),
                  full(wa), full(wb), full(pek), full(w1k), full(pev), full(w1v), full(w2)],
        out_specs=[pl.BlockSpec((1, n_seg, kw), lambda b: (b, 0, 0)),
                   pl.BlockSpec((1, n_seg, vw), lambda b: (b, 0, 0))],
        out_shape=[jax.ShapeDtypeStruct((batch, n_seg, kw), BF16), jax.ShapeDtypeStruct((batch, n_seg, vw), BF16)],
        compiler_params=_params("parallel"),
        name="nsa_compress",
    )(seg, wa, wb, pek, w1k, pev, w1v, w2)


def _softmax_step(carry, s, valid, v_lo, v_hi, odd):
    m_old, l_old = carry
    s = jnp.where(valid, s, MASK_VALUE)
    m_new = jnp.maximum(m_old, jnp.max(s, axis=-1, keepdims=True))
    alpha = jnp.exp(m_old - m_new)
    p = jnp.exp(s - m_new)
    l_new = alpha * l_old + jnp.sum(p, axis=-1, keepdims=True)
    pv = _mm(p.astype(BF16), v_hi if odd else v_lo)
    return (m_new, l_new), alpha, pv


def _nsa_kernel(q_ref, gate_ref, kc_ref, vc_ref, ks_ref, vs_ref, kw_ref, vw_ref, gcb_ref, bd_ref, bp_ref, far_ref,
                o_ref, *, n_sel, k_top):
    qb = pl.program_id(1)
    tq = Q_BLOCK
    n_cmp_pad = kc_ref.shape[1]
    q0 = qb * tq
    row_t = q0 + lax.broadcasted_iota(jnp.int32, (tq, 1), 0)
    qi = lax.broadcasted_iota(jnp.int32, (tq, tq), 0)
    kj = lax.broadcasted_iota(jnp.int32, (tq, tq), 1)
    low_half = lax.broadcasted_iota(jnp.int32, (tq, LANE), 1) < NSA_DV

    cmp_n = lax.broadcasted_iota(jnp.int32, (tq, n_cmp_pad), 1)
    valid_c = row_t >= cmp_n * CMP_STRIDE + (CMP_LEN - 1)
    any_c = (row_t >= CMP_LEN - 1).astype(F32)
    band_m = lax.broadcasted_iota(jnp.int32, (CMP_BAND_PAD, n_cmp_pad), 0)
    band_n = lax.broadcasted_iota(jnp.int32, (CMP_BAND_PAD, n_cmp_pad), 1)
    band_sel = ((band_n - CMP_PER_Q * qb == band_m + CMP_BAND_LO) | (band_m == CMP_BAND_PAD - 1)).astype(F32)
    ov_s = lax.broadcasted_iota(jnp.int32, (n_sel, n_cmp_pad), 0) * SEL_LEN
    ov_c = lax.broadcasted_iota(jnp.int32, (n_sel, n_cmp_pad), 1) * CMP_STRIDE
    overlap_t = ((ov_c < ov_s + SEL_LEN) & (ov_c + CMP_LEN > ov_s)).astype(F32)

    sel_s = lax.broadcasted_iota(jnp.int32, (n_sel, tq), 0)
    blk_t = (q0 + lax.broadcasted_iota(jnp.int32, (n_sel, tq), 1)) // SEL_LEN
    forced = (sel_s == 0) | (sel_s == blk_t) | (sel_s == blk_t - 1)
    future = sel_s > blk_t

    q_all = q_ref[0]
    gates = gate_ref[0]
    hw = NSA_HEADS * NSA_DV

    for g in range(NSA_GROUPS):
        heads = range(g * NSA_HPG, (g + 1) * NSA_HPG)
        qs = [q_all[:, h * LANE:(h + 1) * LANE] for h in heads]
        kcol = slice(g * LANE, (g + 1) * LANE)
        vlo = slice(g * 2 * LANE, g * 2 * LANE + LANE)
        vhi = slice(g * 2 * LANE + LANE, (g + 1) * 2 * LANE)

        k_c = kc_ref[0, :, kcol]
        vc_lo, vc_hi = vc_ref[0, :, vlo], vc_ref[0, :, vhi]
        p_sum = jnp.zeros((tq, n_cmp_pad), F32)
        o_cmp = []
        for pair in range(NSA_HPG // 2):
            acc = jnp.zeros((tq, LANE), F32)
            for odd in range(2):
                h = heads[2 * pair + odd]
                s = _nt(qs[2 * pair + odd], k_c) + _mm(gcb_ref[h], band_sel, precision=HIGHEST)
                s = jnp.where(valid_c, s, MASK_VALUE)
                p = jnp.exp(s - jnp.max(s, axis=-1, keepdims=True))
                p = p / jnp.sum(p, axis=-1, keepdims=True) * any_c
                p_sum = p_sum + p
                acc = acc + _mm(p.astype(BF16), vc_hi if odd else vc_lo)
            o_cmp.append(acc)

        imp = _nt(overlap_t, p_sum, precision=HIGHEST)
        imp = jnp.where(forced, jnp.inf, jnp.where(future, -jnp.inf, imp))
        chosen = jnp.zeros((n_sel, tq), F32)
        for _ in range(k_top):
            best = jnp.max(imp, axis=0, keepdims=True)
            first = jnp.min(jnp.where(imp == best, sel_s, n_sel), axis=0, keepdims=True)
            pick = sel_s == first
            chosen = jnp.where(pick, 1.0, chosen)
            imp = jnp.where(pick, -jnp.inf, imp)
        chosen = chosen.T.astype(BF16)

        far = [far_ref[h] for h in heads]

        def tile_bias(h, is_diag, is_prev):
            return jnp.where(is_diag, bd_ref[h], jnp.where(is_prev, bp_ref[h], far_ref[h]))

        def init():
            return (tuple((jnp.full((tq, 1), MASK_VALUE, F32), jnp.zeros((tq, 1), F32)) for _ in heads),
                    tuple(jnp.zeros((tq, LANE), F32) for _ in range(NSA_HPG // 2)))

        def attend(carry, k_t, v_lo, v_hi, valid, is_diag, is_prev):
            stats, accs = carry
            new_stats, new_accs = [], []
            for pair in range(NSA_HPG // 2):
                upd = []
                for odd in range(2):
                    idx = 2 * pair + odd
                    s = _nt(qs[idx], k_t) + tile_bias(heads[idx], is_diag, is_prev)
                    st, alpha, pv = _softmax_step(stats[idx], s, valid, v_lo, v_hi, odd)
                    new_stats.append(st)
                    upd.append((alpha, pv))
                alpha = jnp.where(low_half, upd[0][0], upd[1][0])
                new_accs.append(accs[pair] * alpha + upd[0][1] + upd[1][1])
            return tuple(new_stats), tuple(new_accs)

        def finish(carry):
            stats, accs = carry
            return [accs[pair] / jnp.where(low_half, stats[2 * pair][1], stats[2 * pair + 1][1])
                    for pair in range(NSA_HPG // 2)]

        def sel_body(c, carry):
            base = pl.multiple_of(c * tq, tq)
            k_t = ks_ref[0, pl.ds(base, tq), kcol]
            v_lo = vs_ref[0, pl.ds(base, tq), vlo]
            v_hi = vs_ref[0, pl.ds(base, tq), vhi]
            blk_of_key = lax.broadcasted_iota(jnp.int32, (n_sel, tq), 1) // SEL_LEN + c * (tq // SEL_LEN)
            expand = (lax.broadcasted_iota(jnp.int32, (n_sel, tq), 0) == blk_of_key).astype(BF16)
            picked = _mm(chosen, expand) > 0.5
            valid = picked & ((c < qb) | (qi >= kj))
            return attend(carry, k_t, v_lo, v_hi, valid, c == qb, c == qb - 1)

        o_sel = finish(lax.fori_loop(0, qb + 1, sel_body, init()))

        carry = init()
        n_back = WINDOW // tq
        for j in range(n_back + 1):
            c = qb - n_back + j
            base = pl.multiple_of(jnp.maximum(c, 0) * tq, tq)
            k_t = kw_ref[0, pl.ds(base, tq), kcol]
            v_lo = vw_ref[0, pl.ds(base, tq), vlo]
            v_hi = vw_ref[0, pl.ds(base, tq), vhi]
            if j == 0:
                shape_ok = kj > qi
            elif j == n_back:
                shape_ok = qi >= kj
            else:
                shape_ok = jnp.ones((tq, tq), jnp.bool_)
            valid = shape_ok & (c >= 0)
            carry = attend(carry, k_t, v_lo, v_hi, valid, j == n_back, j == n_back - 1)
        o_win = finish(carry)

        for pair in range(NSA_HPG // 2):
            col = (g * NSA_HPG + 2 * pair) * NSA_DV
            cs = slice(col, col + LANE)
            o_ref[0, :, cs] = (gates[:, cs] * o_cmp[pair]
                               + gates[:, hw + col:hw + col + LANE] * o_sel[pair]
                               + gates[:, 2 * hw + col:2 * hw + col + LANE] * o_win[pair])


def _rel_bucket(dist):
    n = jnp.maximum(dist, 0)
    max_exact = REL_BUCKETS // 2
    nf = jnp.maximum(n, 1).astype(F32)
    large = max_exact + (jnp.log(nf / max_exact) / math.log(REL_MAX_DIST / max_exact)
                         * (REL_BUCKETS - max_exact)).astype(jnp.int32)
    large = jnp.minimum(large, REL_BUCKETS - 1)
    return jnp.where(n < max_exact, n, large)


def _bias_tables(rel_bias):
    tbl = rel_bias.astype(F32).T
    i = jnp.arange(Q_BLOCK)
    dist = i[:, None] - i[None, :]
    diag = tbl[:, _rel_bucket(dist)]
    prev = tbl[:, _rel_bucket(dist + Q_BLOCK)]
    far = jnp.broadcast_to(tbl[:, REL_BUCKETS - 1][:, None, None], diag.shape)
    m = jnp.arange(CMP_BAND_PAD)
    d_c = i[:, None] - CMP_STRIDE * (m[None, :] + CMP_BAND_LO) - (CMP_LEN - 1)
    band = tbl[:, _rel_bucket(d_c)] - tbl[:, REL_BUCKETS - 1][:, None, None]
    band = jnp.where((m < CMP_BAND)[None, None, :], band, 0.0)
    band = band.at[:, :, CMP_BAND_PAD - 1].set(jnp.broadcast_to(tbl[:, REL_BUCKETS - 1][:, None], (NSA_HEADS, Q_BLOCK)))
    return band, diag, prev, far


def _nsa(q, gates, k_cmp, v_cmp, k_sel, v_sel, k_win, v_win, tables, batch):
    m = q.shape[0]
    s = m // batch
    n_sel = s // SEL_LEN
    k_top = min(SEL_TOPK, n_sel)
    hw = NSA_HEADS * NSA_DV
    kw, vw = NSA_GROUPS * LANE, NSA_GROUPS * 2 * LANE
    band, diag, prev, far = tables
    n_seg = k_cmp.shape[1]
    per_batch = lambda n, w: pl.BlockSpec((1, n, w), lambda b, i: (b, 0, 0), pipeline_mode=pl.Buffered(1))
    tok = lambda w: pl.BlockSpec((1, Q_BLOCK, w), lambda b, i: (b, i, 0))
    out = pl.pallas_call(
        functools.partial(_nsa_kernel, n_sel=n_sel, k_top=k_top),
        grid=(batch, s // Q_BLOCK),
        in_specs=[tok(NSA_HEADS * LANE), tok(N_BRANCH * hw),
                  per_batch(n_seg, kw), per_batch(n_seg, vw),
                  per_batch(s, kw), per_batch(s, vw), per_batch(s, kw), per_batch(s, vw),
                  _resident(band.shape), _resident(diag.shape), _resident(prev.shape), _resident(far.shape)],
        out_specs=tok(hw),
        out_shape=jax.ShapeDtypeStruct((batch, s, hw), F32),
        compiler_params=_params("parallel", "arbitrary"),
        name="nsa_attend",
    )(q.reshape(batch, s, -1), gates.reshape(batch, s, -1), k_cmp, v_cmp,
      k_sel.reshape(batch, s, kw), v_sel.reshape(batch, s, vw), k_win.reshape(batch, s, kw), v_win.reshape(batch, s, vw),
      band, diag, prev, far)
    return out.reshape(m, hw)


MERGE_TOKENS = 512


def _merge_kernel(x_ref, oa_ref, ob_ref, oc_ref, gate_ref, wa_ref, wb_ref, wc_ref, wo_ref, nw_ref, o_ref):
    d = x_ref.shape[1]
    merged = (gate_ref[:, 0:d] * _mm(oa_ref[...].astype(BF16), wa_ref[...])
              + gate_ref[:, d:2 * d] * _mm(ob_ref[...].astype(BF16), wb_ref[...])
              + gate_ref[:, 2 * d:3 * d] * _mm(oc_ref[...].astype(BF16), wc_ref[...]))
    y = _mm(merged.astype(BF16), wo_ref[...])
    o_ref[...] = x_ref[...] + _rms(y, nw_ref[...])


def _merge(x, o_a, o_b, o_c, gates, w_a, w_b, w_c, w_out, norm_post):
    m, d = x.shape
    row = lambda w: pl.BlockSpec((MERGE_TOKENS, w), lambda i: (i, 0))
    full = lambda a: _resident(a.shape)
    ws = [w.astype(BF16) for w in (w_a, w_b, w_c, w_out)]
    return pl.pallas_call(
        _merge_kernel,
        grid=(m // MERGE_TOKENS,),
        in_specs=[row(d), row(o_a.shape[1]), row(o_b.shape[1]), row(o_c.shape[1]), row(N_BRANCH * d),
                  full(ws[0]), full(ws[1]), full(ws[2]), full(ws[3]), _resident((1, d))],
        out_specs=row(d),
        out_shape=jax.ShapeDtypeStruct((m, d), F32),
        compiler_params=_params("parallel"),
        name="merge_out",
    )(x, o_a, o_b, o_c, gates, *ws, norm_post.reshape(1, d))


_IN_COLUMNS = (
    ("gdn_q", GDN_HEADS * GDN_DK), ("gdn_k", GDN_HEADS * GDN_DK), ("gdn_v", GDN_HEADS * GDN_DV),
    ("gdn_z", GDN_HEADS * GDN_DV), ("gdn_b", GDN_HEADS), ("gdn_a", GDN_HEADS),
    ("nsa_q", NSA_HEADS * NSA_DK), ("nsa_kv_cmp", NSA_GROUPS * (NSA_DK + NSA_DV)),
    ("nsa_kv_sel", NSA_GROUPS * (NSA_DK + NSA_DV)), ("nsa_kv_win", NSA_GROUPS * (NSA_DK + NSA_DV)),
    ("nsa_gate", 3 * NSA_HEADS),
    ("gla_q", GLA_HEADS * GLA_DK), ("gla_k", GLA_HEADS * GLA_DK), ("gla_v", GLA_HEADS * GLA_DV),
    ("gla_r", GLA_HEADS * GLA_DV), ("gla_a", GLA_RANK), ("merge_gate", None),
)


def _split_w_in(w_in):
    out, off = {}, 0
    for name, width in _IN_COLUMNS:
        width = w_in.shape[1] - off if width is None else width
        out[name] = w_in[:, off:off + width]
        off += width
    return out


def _pad_cols(w, width):
    return jnp.pad(w, ((0, 0), (0, width - w.shape[1])))


def _kv_layout(w):
    d = w.shape[0]
    w = w.reshape(d, NSA_GROUPS, NSA_DK + NSA_DV)
    k = jnp.pad(w[:, :, :NSA_DK], ((0, 0), (0, 0), (0, LANE - NSA_DK))).reshape(d, NSA_GROUPS * LANE)
    v = w[:, :, NSA_DK:]
    v_lo = jnp.pad(v, ((0, 0), (0, 0), (0, LANE - NSA_DV)))
    v_hi = jnp.pad(v, ((0, 0), (0, 0), (LANE - NSA_DV, 0)))
    return k, jnp.stack([v_lo, v_hi], axis=2).reshape(d, NSA_GROUPS * 2 * LANE)


def _mix_pieces(w_in):
    d = w_in.shape[0]
    c = _split_w_in(w_in)
    gdn = [
        (jnp.concatenate([c["gdn_q"], c["gdn_k"], c["gdn_v"]], axis=1), F32, None),
        (c["gdn_z"], F32, None),
        (_pad_cols(jnp.concatenate([c["gdn_b"], c["gdn_a"]], axis=1), LANE), F32, None),
    ]
    gla = [
        (jnp.concatenate([c["gla_q"], c["gla_k"]], axis=1), F32, None),
        (c["gla_v"], F32, None),
        (c["gla_r"], F32, None),
        (_pad_cols(c["gla_a"], LANE), F32, None),
    ]
    q = jnp.pad(c["nsa_q"].reshape(d, NSA_HEADS, NSA_DK), ((0, 0), (0, 0), (0, LANE - NSA_DK)))
    k_sel, v_sel = _kv_layout(c["nsa_kv_sel"])
    k_win, v_win = _kv_layout(c["nsa_kv_win"])
    gate = c["nsa_gate"].reshape(d, NSA_HEADS, N_BRANCH).transpose(0, 2, 1)
    gate = jnp.broadcast_to(gate[..., None], (d, N_BRANCH, NSA_HEADS, NSA_DV)).reshape(d, -1)
    nsa = [
        (q.reshape(d, NSA_HEADS * LANE), BF16, NSA_DK ** -0.5),
        (c["nsa_kv_cmp"], BF16, None),
        (k_sel, BF16, None), (v_sel, BF16, None), (k_win, BF16, None), (v_win, BF16, None),
        (gate, F32, "sigmoid"),
        (c["merge_gate"], F32, "sigmoid"),
    ]
    return gdn + gla, nsa


def _layer(x, batch, tables, p):
    x = _ffn(x, p["ffn1_norm_pre"], p["ffn1_w_gate_up"], p["ffn1_w_down"], p["ffn1_norm_post"])
    rec_pieces, nsa_pieces = _mix_pieces(p["w_in"])
    qkv, z, ba, gla_qk, gla_v, gla_r, gla_a = _proj(x, p["mix_norm_pre"], rec_pieces)
    nsa_q, kv_cmp, k_sel, v_sel, k_win, v_win, nsa_gate, merge_gate = _proj(x, p["mix_norm_pre"], nsa_pieces)
    o_a = _gdn(qkv, z, ba, p["gdn_conv_w"], p["gdn_a_log"], p["gdn_dt_bias"], p["gdn_norm_w"], batch)
    o_c = _gla(gla_qk, gla_v, gla_r, gla_a, p["gla_gate_w"], p["gla_gate_b"], p["gla_norm_w"], batch)
    cmp_w = _cmp_weights(p["nsa_pe_k"], p["nsa_cmp_k_w1"], p["nsa_cmp_k_w2"],
                         p["nsa_pe_v"], p["nsa_cmp_v_w1"], p["nsa_cmp_v_w2"])
    k_cmp, v_cmp = _compress(kv_cmp, cmp_w, batch)
    o_b = _nsa(nsa_q, nsa_gate, k_cmp, v_cmp, k_sel, v_sel, k_win, v_win, tables, batch)
    x = _merge(x, o_a, o_b, o_c, merge_gate, p["w_branch_gdn"], p["w_branch_nsa"], p["w_branch_gla"],
               p["w_out"], p["mix_norm_post"])
    return _ffn(x, p["ffn2_norm_pre"], p["ffn2_w_gate_up"], p["ffn2_w_down"], p["ffn2_norm_post"])


_LAYER_PARAMS = (
    "ffn1_norm_pre", "ffn1_w_gate_up", "ffn1_w_down", "ffn1_norm_post", "mix_norm_pre", "w_in",
    "gdn_conv_w", "gdn_a_log", "gdn_dt_bias", "gdn_norm_w",
    "nsa_pe_k", "nsa_cmp_k_w1", "nsa_cmp_k_w2", "nsa_pe_v", "nsa_cmp_v_w1", "nsa_cmp_v_w2",
    "gla_gate_w", "gla_gate_b", "gla_norm_w",
    "w_branch_gdn", "w_branch_nsa", "w_branch_gla", "w_out", "mix_norm_post",
    "ffn2_norm_pre", "ffn2_w_gate_up", "ffn2_w_down", "ffn2_norm_post",
)


def kernel(x, rel_bias, ffn1_norm_pre, ffn1_w_gate_up, ffn1_w_down, ffn1_norm_post, mix_norm_pre, w_in, gdn_conv_w, gdn_a_log, gdn_dt_bias, gdn_norm_w, nsa_pe_k, nsa_cmp_k_w1, nsa_cmp_k_w2, nsa_pe_v, nsa_cmp_v_w1, nsa_cmp_v_w2, gla_gate_w, gla_gate_b, gla_norm_w, w_branch_gdn, w_branch_nsa, w_branch_gla, w_out, mix_norm_post, ffn2_norm_pre, ffn2_w_gate_up, ffn2_w_down, ffn2_norm_post):
    stacked = dict(zip(_LAYER_PARAMS, (
        ffn1_norm_pre, ffn1_w_gate_up, ffn1_w_down, ffn1_norm_post, mix_norm_pre, w_in,
        gdn_conv_w, gdn_a_log, gdn_dt_bias, gdn_norm_w,
        nsa_pe_k, nsa_cmp_k_w1, nsa_cmp_k_w2, nsa_pe_v, nsa_cmp_v_w1, nsa_cmp_v_w2,
        gla_gate_w, gla_gate_b, gla_norm_w,
        w_branch_gdn, w_branch_nsa, w_branch_gla, w_out, mix_norm_post,
        ffn2_norm_pre, ffn2_w_gate_up, ffn2_w_down, ffn2_norm_post)))
    batch, seq, d = x.shape
    tables = _bias_tables(rel_bias)
    h = x.reshape(batch * seq, d)
    for layer in range(ffn1_norm_pre.shape[0]):
        h = _layer(h, batch, tables, {name: value[layer] for name, value in stacked.items()})
    return h.reshape(batch, seq, d)
```

```python
import functools
import math

import jax
import jax.numpy as jnp
from jax import lax
from jax.experimental import pallas as pl
from jax.experimental.pallas import tpu as pltpu

F32 = jnp.float32
BF16 = jnp.bfloat16
HIGHEST = lax.Precision.HIGHEST

LANE = 128
VMEM_LIMIT_BYTES = 56 * 1024 * 1024

NORM_EPS = 1e-6
FFN_RES_SCALE = 0.5
REL_BUCKETS = 32
REL_MAX_DIST = 128
GDN_HEADS, GDN_DK, GDN_DV, GDN_CONV, GDN_CHUNK = 4, 128, 128, 4, 64
NSA_HEADS, NSA_GROUPS, NSA_DK, NSA_DV = 8, 2, 96, 64
NSA_HPG = NSA_HEADS // NSA_GROUPS
CMP_LEN, CMP_STRIDE, CMP_HIDDEN = 32, 16, 256
SEL_LEN, SEL_TOPK, WINDOW, Q_BLOCK = 64, 16, 512, 128
GLA_HEADS, GLA_DK, GLA_DV, GLA_RANK, GLA_TAU, GLA_CHUNK = 4, 64, 128, 16, 16, 16
N_BRANCH = 3
MASK_VALUE = -1e30
L2_EPS = 1e-6

CMP_PER_Q = Q_BLOCK // CMP_STRIDE
CMP_BAND_LO = -10
CMP_BAND = 17
CMP_BAND_PAD = 32


def _nt(a, b, **kw):
    return lax.dot_general(a, b, (((1,), (1,)), ((), ())), preferred_element_type=F32, **kw)


def _tn(a, b, **kw):
    return lax.dot_general(a, b, (((0,), (0,)), ((), ())), preferred_element_type=F32, **kw)


def _mm(a, b, **kw):
    return jnp.dot(a, b, preferred_element_type=F32, **kw)


def _rms(x, w):
    return x * lax.rsqrt(jnp.mean(x * x, axis=-1, keepdims=True) + NORM_EPS) * w


def _silu(x):
    return x * jax.nn.sigmoid(x)


def _softplus(x):
    return jnp.maximum(x, 0.0) + jnp.log1p(jnp.exp(-jnp.abs(x)))


def _params(*semantics):
    return pltpu.CompilerParams(dimension_semantics=semantics, vmem_limit_bytes=VMEM_LIMIT_BYTES)


def _resident(shape):
    return pl.BlockSpec(shape, lambda *_: (0,) * len(shape), pipeline_mode=pl.Buffered(1))


FFN_TOKENS = 512
FFN_CHUNK = 256


def _ffn_kernel(x_ref, npre_ref, wgu_ref, wd_ref, npost_ref, o_ref):
    d_ff = wd_ref.shape[0]
    x = x_ref[...]
    h = _rms(x, npre_ref[...]).astype(BF16)
    acc = jnp.zeros(x.shape, F32)
    for c in range(d_ff // FFN_CHUNK):
        lo = c * FFN_CHUNK
        g = _mm(h, wgu_ref[:, lo:lo + FFN_CHUNK])
        u = _mm(h, wgu_ref[:, d_ff + lo:d_ff + lo + FFN_CHUNK])
        a = (_silu(g) * u).astype(BF16)
        acc = acc + _mm(a, wd_ref[lo:lo + FFN_CHUNK, :])
    o_ref[...] = x + FFN_RES_SCALE * _rms(acc, npost_ref[...])


def _ffn(x, norm_pre, w_gate_up, w_down, norm_post):
    m, d = x.shape
    d_ff = w_down.shape[0]
    row = pl.BlockSpec((FFN_TOKENS, d), lambda i: (i, 0))
    return pl.pallas_call(
        _ffn_kernel,
        grid=(m // FFN_TOKENS,),
        in_specs=[row, _resident((1, d)), _resident((d, 2 * d_ff)), _resident((d_ff, d)), _resident((1, d))],
        out_specs=row,
        out_shape=jax.ShapeDtypeStruct((m, d), F32),
        compiler_params=_params("parallel"),
        name="ffn",
    )(x, norm_pre.reshape(1, d), w_gate_up.astype(BF16), w_down.astype(BF16), norm_post.reshape(1, d))


PROJ_TOKENS = 256
PROJ_CHUNK = 512


def _proj_kernel(x_ref, nw_ref, w_ref, *o_refs, segs):
    h = _rms(x_ref[...], nw_ref[...]).astype(BF16)
    for (off, width, epilogue), o_ref in zip(segs, o_refs):
        for lo in range(0, width, PROJ_CHUNK):
            hi = min(lo + PROJ_CHUNK, width)
            y = _mm(h, w_ref[:, off + lo:off + hi])
            if epilogue == "sigmoid":
                y = jax.nn.sigmoid(y)
            elif epilogue == "ones_hi":
                lane = lax.broadcasted_iota(jnp.int32, y.shape, 1) % LANE
                y = jnp.where(lane >= LANE // 2, 1.0, y)
            elif epilogue is not None:
                y = y * epilogue
            o_ref[:, lo:hi] = y.astype(o_ref.dtype)


def _proj(x, norm_w, pieces):
    m, d = x.shape
    segs, cols, off = [], [], 0
    for w, _, epilogue in pieces:
        width = w.shape[1]
        pad = (-width) % LANE
        segs.append((off, width, epilogue))
        cols.append(w)
        if pad:
            cols.append(jnp.zeros((d, pad), w.dtype))
        off += width + pad
    w_all = jnp.concatenate(cols, axis=1).astype(BF16)
    row = pl.BlockSpec((PROJ_TOKENS, d), lambda i: (i, 0))
    return pl.pallas_call(
        functools.partial(_proj_kernel, segs=tuple(segs)),
        grid=(m // PROJ_TOKENS,),
        in_specs=[row, _resident((1, d)), _resident((d, off))],
        out_specs=[pl.BlockSpec((PROJ_TOKENS, w.shape[1]), lambda i: (i, 0)) for w, _, _ in pieces],
        out_shape=[jax.ShapeDtypeStruct((m, w.shape[1]), dt) for w, dt, _ in pieces],
        compiler_params=_params("parallel"),
        name="in_proj",
    )(x, norm_w.reshape(1, d), w_all)


GDN_TOKENS = 256
CONV_PAD = 8


def _gdn_kernel(q_ref, k_ref, v_ref, z_ref, ba_ref, cwq_ref, cwk_ref, cwv_ref, alog_ref, dtb_ref, nw_ref,
                o_ref, xq_ref, xk_ref, xv_ref, state_ref):
    head = pl.program_id(1)
    t_blk = q_ref.shape[1]
    c_len = GDN_CHUNK

    @pl.when(pl.program_id(2) == 0)
    def _():
        state_ref[...] = jnp.zeros_like(state_ref)
        for x_ref in (xq_ref, xk_ref, xv_ref):
            x_ref[0:CONV_PAD, :] = jnp.zeros((CONV_PAD, x_ref.shape[1]), F32)

    def conv_silu(raw_ref, x_ref, cw_ref):
        x_ref[CONV_PAD:CONV_PAD + t_blk, :] = raw_ref[0]
        y = jnp.zeros((t_blk, x_ref.shape[1]), F32)
        for tap in range(GDN_CONV):
            y = y + cw_ref[tap:tap + 1, :] * x_ref[pl.ds(CONV_PAD - (GDN_CONV - 1) + tap, t_blk), :]
        x_ref[0:CONV_PAD, :] = x_ref[t_blk:t_blk + CONV_PAD, :]
        return _silu(y)

    q = conv_silu(q_ref, xq_ref, cwq_ref)
    k = conv_silu(k_ref, xk_ref, cwk_ref)
    v = conv_silu(v_ref, xv_ref, cwv_ref)
    q = q * lax.rsqrt(jnp.sum(q * q, axis=-1, keepdims=True) + L2_EPS) * (GDN_DK ** -0.5)
    k = k * lax.rsqrt(jnp.sum(k * k, axis=-1, keepdims=True) + L2_EPS)

    ba = ba_ref[0]
    g_all = -jnp.exp(alog_ref[...]) * _softplus(ba + dtb_ref[...])
    row = lax.broadcasted_iota(jnp.int32, (LANE, LANE), 0)
    pick_b = (row == head).astype(F32)
    pick_a = (row == head + GDN_HEADS).astype(F32)
    beta = jax.nn.sigmoid(_mm(ba, pick_b, precision=HIGHEST))
    g = _mm(g_all, pick_a, precision=HIGHEST)

    ci = lax.broadcasted_iota(jnp.int32, (c_len, c_len), 0)
    cj = lax.broadcasted_iota(jnp.int32, (c_len, c_len), 1)
    lower = ci >= cj
    strict = ci > cj
    tri = lower.astype(F32)
    eye = (ci == cj).astype(F32)
    lane0 = (lax.broadcasted_iota(jnp.int32, (c_len, LANE), 1) == 0).astype(F32)

    outs = []
    state = state_ref[...]
    for c in range(t_blk // c_len):
        rows = slice(c * c_len, (c + 1) * c_len)
        qc, kc, vc, bc = q[rows], k[rows], v[rows], beta[rows]
        gc = _mm(tri, g[rows], precision=HIGHEST)
        gc_row = _nt(lane0, gc, precision=HIGHEST)
        gc_col = gc[:, :c_len]
        decay = jnp.exp(jnp.where(lower, gc_col - gc_row, MASK_VALUE))
        kb = kc * bc
        lmat = jnp.where(strict, _nt(kb, kc, precision=HIGHEST) * decay, 0.0)
        pw = -lmat
        inv = eye + pw
        for _ in range(5):
            pw = _mm(pw, pw, precision=HIGHEST)
            inv = inv + _mm(inv, pw, precision=HIGHEST)
        e_gc = jnp.exp(gc)
        rhs = jnp.concatenate([vc * bc, kb * e_gc], axis=-1)
        sol = _mm(inv, rhs, precision=HIGHEST)
        u, w = sol[:, :GDN_DV], sol[:, GDN_DV:]
        attn = _nt(qc.astype(BF16), kc.astype(BF16)) * decay
        gc_last = gc[c_len - 1:c_len, :]
        q_dec = (qc * e_gc).astype(BF16)
        k_dec = (kc * jnp.exp(gc_last - gc)).astype(BF16)
        s16 = state.astype(BF16)
        v_new = u - _mm(w.astype(BF16), s16)
        v16 = v_new.astype(BF16)
        outs.append(_mm(q_dec, s16) + _mm(attn.astype(BF16), v16))
        state = state * jnp.exp(gc_last) + _tn(k_dec, v16)
    state_ref[...] = state
    o = jnp.concatenate(outs, axis=0)
    o_ref[0] = _rms(o, nw_ref[...]) * _silu(z_ref[0])


def _gdn(qkv, z, ba, conv_w, a_log, dt_bias, norm_w, batch):
    m = qkv.shape[0]
    s = m // batch
    h, dk, dv = GDN_HEADS, GDN_DK, GDN_DV
    qkv = qkv.reshape(batch, s, 3 * h * dk)
    tok = lambda col: pl.BlockSpec((1, GDN_TOKENS, LANE), lambda b, hh, i, col=col: (b, i, col + hh))
    cw = lambda col: pl.BlockSpec((GDN_CONV, LANE), lambda b, hh, i, col=col: (0, col + hh))
    lane_pad = jnp.zeros((LANE - 2 * h,), F32)
    alog_row = jnp.concatenate([jnp.zeros((h,), F32), a_log, lane_pad]).reshape(1, LANE)
    dtb_row = jnp.concatenate([jnp.zeros((h,), F32), dt_bias, lane_pad]).reshape(1, LANE)
    out = pl.pallas_call(
        _gdn_kernel,
        grid=(batch, h, s // GDN_TOKENS),
        in_specs=[tok(0), tok(h), tok(2 * h),
                  pl.BlockSpec((1, GDN_TOKENS, LANE), lambda b, hh, i: (b, i, hh)),
                  pl.BlockSpec((1, GDN_TOKENS, LANE), lambda b, hh, i: (b, i, 0)),
                  cw(0), cw(h), cw(2 * h),
                  _resident((1, LANE)), _resident((1, LANE)), _resident((1, dv))],
        out_specs=pl.BlockSpec((1, GDN_TOKENS, LANE), lambda b, hh, i: (b, i, hh)),
        out_shape=jax.ShapeDtypeStruct((batch, s, h * dv), F32),
        scratch_shapes=[pltpu.VMEM((GDN_TOKENS + CONV_PAD, LANE), F32)] * 3 + [pltpu.VMEM((dk, dv), F32)],
        compiler_params=_params("parallel", "parallel", "arbitrary"),
        name="gdn",
    )(qkv, qkv, qkv, z.reshape(batch, s, h * dv), ba.reshape(batch, s, LANE),
      conv_w, conv_w, conv_w, alog_row, dtb_row, norm_w.reshape(1, dv))
    return out.reshape(m, h * dv)


GLA_TOKENS = 256


def _log_sigmoid(x):
    return jnp.minimum(x, 0.0) - jnp.log1p(jnp.exp(-jnp.abs(x)))


def _gla_kernel(qk_ref, v_ref, r_ref, a_ref, gw_ref, gb_ref, nw_ref, o_ref, ks_ref, bs_ref, vs_ref, state_ref):
    t_blk = qk_ref.shape[1]
    c_len = GLA_CHUNK
    hdk = GLA_HEADS * GLA_DK
    hdv = GLA_HEADS * GLA_DV

    @pl.when(pl.program_id(1) == 0)
    def _():
        state_ref[...] = jnp.zeros_like(state_ref)
        ks_ref[0:c_len, :] = jnp.zeros((c_len, hdk), F32)
        bs_ref[0:c_len, :] = jnp.zeros((c_len, hdk), F32)
        vs_ref[0:c_len, :] = jnp.zeros((c_len, hdv), F32)

    qk = qk_ref[0]
    q = qk[:, :hdk] * (GLA_DK ** -0.5)
    k = qk[:, hdk:]
    v = v_ref[0]
    log_a = _log_sigmoid(_mm(a_ref[0].astype(BF16), gw_ref[...]) + gb_ref[...]) * (1.0 / GLA_TAU)

    ti = lax.broadcasted_iota(jnp.int32, (t_blk, t_blk), 0)
    tj = lax.broadcasted_iota(jnp.int32, (t_blk, t_blk), 1)
    same = (ti // c_len) == (tj // c_len)
    cum = _mm((same & (tj <= ti)).astype(F32), log_a, precision=HIGHEST)
    tot = _mm(same.astype(F32), log_a, precision=HIGHEST)

    ks_ref[c_len:, :] = k
    bs_ref[c_len:, :] = cum
    vs_ref[c_len:, :] = v
    pos = lax.broadcasted_iota(jnp.int32, (t_blk, 1), 0) % c_len
    hd = lax.broadcasted_iota(jnp.int32, (hdk, hdv), 0) // GLA_DK
    he = lax.broadcasted_iota(jnp.int32, (hdk, hdv), 1) // GLA_DV
    spread = (hd == he).astype(BF16)
    o = jnp.zeros((t_blk, hdv), F32)
    for off in range(c_len):
        k_o = ks_ref[pl.ds(c_len - off, t_blk), :]
        b_o = bs_ref[pl.ds(c_len - off, t_blk), :]
        v_o = vs_ref[pl.ds(c_len - off, t_blk), :]
        w = jnp.exp(jnp.where(pos >= off, cum - b_o, MASK_VALUE))
        o = o + _mm((q * k_o * w).astype(BF16), spread) * v_o

    q_dec = (q * jnp.exp(cum)).astype(BF16)
    k_dec = (k * jnp.exp(tot - cum)).astype(BF16)
    c_dec = jnp.exp(tot)
    v16 = v.astype(BF16)
    states = [state_ref[h] for h in range(GLA_HEADS)]
    inter = []
    for n in range(t_blk // c_len):
        rows = slice(n * c_len, (n + 1) * c_len)
        parts = []
        for h in range(GLA_HEADS):
            dks = slice(h * GLA_DK, (h + 1) * GLA_DK)
            dvs = slice(h * GLA_DV, (h + 1) * GLA_DV)
            parts.append(_nt(q_dec[rows, dks], states[h].astype(BF16)))
            states[h] = states[h] * c_dec[n * c_len:n * c_len + 1, dks] + _tn(v16[rows, dvs], k_dec[rows, dks])
        inter.append(jnp.concatenate(parts, axis=-1))
    for h in range(GLA_HEADS):
        state_ref[h] = states[h]
    o = o + jnp.concatenate(inter, axis=0)

    r = r_ref[0]
    nw = nw_ref[...]
    for h in range(GLA_HEADS):
        dvs = slice(h * GLA_DV, (h + 1) * GLA_DV)
        o_ref[0, :, dvs] = _rms(o[:, dvs], nw) * _silu(r[:, dvs])


def _gla(qk, v, r, a_low, gate_w, gate_b, norm_w, batch):
    m = qk.shape[0]
    s = m // batch
    hdk, hdv = GLA_HEADS * GLA_DK, GLA_HEADS * GLA_DV
    gw = jnp.zeros((LANE, hdk), F32).at[:GLA_RANK].set(gate_w).astype(BF16)
    blk = lambda w: pl.BlockSpec((1, GLA_TOKENS, w), lambda b, i: (b, i, 0))
    out = pl.pallas_call(
        _gla_kernel,
        grid=(batch, s // GLA_TOKENS),
        in_specs=[blk(2 * hdk), blk(hdv), blk(hdv), blk(LANE),
                  _resident((LANE, hdk)), _resident((1, hdk)), _resident((1, GLA_DV))],
        out_specs=blk(hdv),
        out_shape=jax.ShapeDtypeStruct((batch, s, hdv), F32),
        scratch_shapes=[pltpu.VMEM((GLA_TOKENS + GLA_CHUNK, hdk), F32),
                        pltpu.VMEM((GLA_TOKENS + GLA_CHUNK, hdk), F32),
                        pltpu.VMEM((GLA_TOKENS + GLA_CHUNK, hdv), F32),
                        pltpu.VMEM((GLA_HEADS, GLA_DV, GLA_DK), F32)],
        compiler_params=_params("parallel", "arbitrary"),
        name="gla",
    )(qk.reshape(batch, s, 2 * hdk), v.reshape(batch, s, hdv), r.reshape(batch, s, hdv),
      a_low.reshape(batch, s, LANE), gw, gate_b.reshape(1, hdk), norm_w.reshape(1, GLA_DV))
    return out.reshape(m, hdv)


def _gelu_tanh(x):
    return 0.5 * x * (1.0 + jnp.tanh(math.sqrt(2.0 / math.pi) * (x + 0.044715 * x * x * x)))


def _cmp_kernel(seg_ref, wa_ref, wb_ref, pek_ref, w1k_ref, pev_ref, w1v_ref, w2_ref, k_ref, v_ref):
    seg = seg_ref[0]
    n_seg = seg.shape[0]
    first = _mm(seg, wa_ref[...])
    second = _mm(seg, wb_ref[...])
    bias_k = _mm(pek_ref[...], w1k_ref[...])[0:1]
    bias_v = _mm(pev_ref[...], w1v_ref[...])[0:1]
    bias = jnp.concatenate([bias_k, bias_v] * NSA_GROUPS, axis=-1)
    hid = _gelu_tanh(first + pltpu.roll(second, n_seg - 1, 0) + bias).astype(BF16)
    out = _mm(hid, w2_ref[...])
    kw = NSA_GROUPS * LANE
    k_ref[0] = out[:, :kw].astype(BF16)
    v_ref[0] = out[:, kw:].astype(BF16)


def _cmp_weights(pe_k, w1k, w2k, pe_v, w1v, w2v):
    g, dk, dv, hid = NSA_GROUPS, NSA_DK, NSA_DV, CMP_HIDDEN
    w1k = w1k.reshape(CMP_LEN, dk, hid)
    w1v = w1v.reshape(CMP_LEN, dv, hid)

    def half(lo):
        blk = jnp.zeros((CMP_STRIDE, g, dk + dv, g, 2, hid), F32)
        for gg in range(g):
            blk = blk.at[:, gg, :dk, gg, 0].set(w1k[lo:lo + CMP_STRIDE])
            blk = blk.at[:, gg, dk:, gg, 1].set(w1v[lo:lo + CMP_STRIDE])
        return blk.reshape(CMP_STRIDE * g * (dk + dv), g * 2 * hid).astype(BF16)

    w2 = jnp.zeros((g, 2, hid, 2 * g * LANE), F32)
    for gg in range(g):
        w2 = w2.at[gg, 0, :, gg * LANE:gg * LANE + dk].set(w2k)
        w2 = w2.at[gg, 1, :, (g + gg) * LANE:(g + gg) * LANE + dv].set(w2v)
    w2 = w2.reshape(g * 2 * hid, -1).astype(BF16)
    pad8 = lambda pe: jnp.zeros((8, pe.size), F32).at[0].set(pe.reshape(-1)).astype(BF16)
    return half(0), half(CMP_STRIDE), pad8(pe_k), w1k.reshape(-1, hid).astype(BF16), pad8(pe_v), \
        w1v.reshape(-1, hid).astype(BF16), w2


def _compress(kv_cmp, weights, batch):
    m, width = kv_cmp.shape
    s = m // batch
    n_seg = s // CMP_STRIDE
    seg = kv_cmp.reshape(batch, n_seg, CMP_STRIDE * width)
    wa, wb, pek, w1k, pev, w1v, w2 = weights
    full = lambda a: _resident(a.shape)
    kw = vw = NSA_GROUPS * LANE
    return pl.pallas_call(
        _cmp_kernel,
        grid=(batch,),
        in_specs=[pl.BlockSpec((1, n_seg, CMP_STRIDE * width), lambda b: (b, 0, 0)),
                  full(wa), full(wb), full(pek), full(w1k), full(pev), full(w1v), full(w2)],
        out_specs=[pl.BlockSpec((1, n_seg, kw), lambda b: (b, 0, 0)),
                   pl.BlockSpec((1, n_seg, vw), lambda b: (b, 0, 0))],
        out_shape=[jax.ShapeDtypeStruct((batch, n_seg, kw), BF16), jax.ShapeDtypeStruct((batch, n_seg, vw), BF16)],
        compiler_params=_params("parallel"),
        name="nsa_compress",
    )(seg, wa, wb, pek, w1k, pev, w1v, w2)


NSA_KEY_CHUNK = 512


def _online_step(carry, s, v):
    m_old, acc = carry
    m_new = jnp.maximum(m_old, jnp.max(s, axis=-1, keepdims=True))
    p = jnp.exp(s - m_new).astype(BF16)
    return m_new, jnp.exp(m_old - m_new) * acc + _mm(p, v)


def _normalised(acc):
    low = lax.broadcasted_iota(jnp.int32, acc.shape, 1) < NSA_DV
    return jnp.where(low, acc / pltpu.roll(acc, NSA_DV, 1), 0.0)


def _nsa_kernel(far_ref, q_ref, gate_ref, kc_ref, vc_ref, ks_ref, vs_ref, eb_ref, kw_ref, vw_ref, gcb_ref, bpd_ref,
                o_ref, *, n_sel, k_top):
    qb = pl.program_id(1)
    tq = Q_BLOCK
    rows = NSA_HPG * tq
    n_cmp_pad = kc_ref.shape[1]
    q0 = qb * tq
    qi = lax.broadcasted_iota(jnp.int32, (rows, 1), 0) % tq
    row_t = q0 + qi

    cmp_n = lax.broadcasted_iota(jnp.int32, (rows, n_cmp_pad), 1)
    valid_c = row_t >= cmp_n * CMP_STRIDE + (CMP_LEN - 1)
    any_c = (row_t >= CMP_LEN - 1).astype(F32)
    band_m = lax.broadcasted_iota(jnp.int32, (CMP_BAND_PAD, n_cmp_pad), 0)
    band_n = lax.broadcasted_iota(jnp.int32, (CMP_BAND_PAD, n_cmp_pad), 1)
    band_sel = ((band_n - CMP_PER_Q * qb == band_m + CMP_BAND_LO) | (band_m == CMP_BAND_PAD - 1)).astype(F32)
    ov_s = lax.broadcasted_iota(jnp.int32, (n_sel, n_cmp_pad), 0) * SEL_LEN
    ov_c = lax.broadcasted_iota(jnp.int32, (n_sel, n_cmp_pad), 1) * CMP_STRIDE
    overlap_t = ((ov_c < ov_s + SEL_LEN) & (ov_c + CMP_LEN > ov_s)).astype(F32)

    sel_s = lax.broadcasted_iota(jnp.int32, (n_sel, tq), 0)
    blk_t = (q0 + lax.broadcasted_iota(jnp.int32, (n_sel, tq), 1)) // SEL_LEN
    forced = (sel_s == 0) | (sel_s == blk_t) | (sel_s == blk_t - 1)
    future = sel_s > blk_t

    q_all = q_ref[0]
    gates = gate_ref[0]
    hw = NSA_HEADS * LANE
    prev_base = pl.multiple_of(jnp.maximum(qb - 1, 0) * tq, tq)
    diag_base = pl.multiple_of(q0, tq)

    def add_far(s, heads):
        return jnp.concatenate([s[i * tq:(i + 1) * tq] + far_ref[h] for i, h in enumerate(heads)], axis=0)

    for g in range(NSA_GROUPS):
        heads = range(g * NSA_HPG, (g + 1) * NSA_HPG)
        q4 = jnp.concatenate([q_all[:, h * LANE:(h + 1) * LANE] for h in heads], axis=0)
        col = slice(g * LANE, (g + 1) * LANE)

        s = _nt(q4, kc_ref[0, :, col]) + _mm(gcb_ref[g], band_sel, precision=HIGHEST)
        s = jnp.where(valid_c, s, MASK_VALUE)
        p = jnp.exp(s - jnp.max(s, axis=-1, keepdims=True))
        p = p / jnp.sum(p, axis=-1, keepdims=True) * any_c
        o_cmp = _mm(p.astype(BF16), vc_ref[0, :, col])
        p_sum = p[0:tq]
        for i in range(1, NSA_HPG):
            p_sum = p_sum + p[i * tq:(i + 1) * tq]

        imp = _nt(overlap_t, p_sum, precision=HIGHEST)
        imp = jnp.where(forced, jnp.inf, jnp.where(future, -jnp.inf, imp))
        chosen = jnp.zeros((n_sel, tq), F32)
        for _ in range(k_top):
            best = jnp.max(imp, axis=0, keepdims=True)
            first = jnp.min(jnp.where(imp == best, sel_s, n_sel), axis=0, keepdims=True)
            pick = sel_s == first
            chosen = jnp.where(pick, 1.0, chosen)
            imp = jnp.where(pick, -jnp.inf, imp)
        skipped = 1.0 - chosen
        skipped_far = jnp.where(sel_s >= (tq // SEL_LEN) * (qb - 1), 1.0, skipped)
        q_far = jnp.concatenate([q4, jnp.concatenate([skipped_far.T.astype(BF16)] * NSA_HPG, axis=0)], axis=1)
        q_near = jnp.concatenate([q4, jnp.concatenate([skipped.T.astype(BF16)] * NSA_HPG, axis=0)], axis=1)

        def far_body(c, carry):
            base = pl.multiple_of(c * NSA_KEY_CHUNK, NSA_KEY_CHUNK)
            k_aug = jnp.concatenate([ks_ref[0, pl.ds(base, NSA_KEY_CHUNK), col],
                                     eb_ref[pl.ds(base, NSA_KEY_CHUNK), :]], axis=1)
            s = add_far(_nt(q_far, k_aug), heads)
            return _online_step(carry, s, vs_ref[0, pl.ds(base, NSA_KEY_CHUNK), col])

        n_far = (jnp.maximum(qb - 1, 0) * tq + NSA_KEY_CHUNK - 1) // NSA_KEY_CHUNK
        carry = lax.fori_loop(0, n_far, far_body,
                              (jnp.full((rows, 1), MASK_VALUE, F32), jnp.zeros((rows, LANE), F32)))

        k_aug = jnp.concatenate(
            [jnp.concatenate([ks_ref[0, pl.ds(prev_base, tq), col], ks_ref[0, pl.ds(diag_base, tq), col]], axis=0),
             jnp.concatenate([eb_ref[pl.ds(prev_base, tq), :], eb_ref[pl.ds(diag_base, tq), :]], axis=0)], axis=1)
        v_near = jnp.concatenate([vs_ref[0, pl.ds(prev_base, tq), col], vs_ref[0, pl.ds(diag_base, tq), col]], axis=0)
        key = lax.broadcasted_iota(jnp.int32, (rows, 2 * tq), 1)
        first_key = jnp.where(qb >= 1, 0, tq)
        valid = (key >= first_key) & (key <= qi + tq)
        s = jnp.where(valid, _nt(q_near, k_aug) + bpd_ref[g], MASK_VALUE)
        o_sel = _normalised(_online_step(carry, s, v_near)[1])

        n_win = WINDOW + tq
        s = _nt(q4, kw_ref[0, pl.ds(diag_base, n_win), col])
        s = jnp.concatenate([add_far(s[:, :n_win - 2 * tq], heads), s[:, n_win - 2 * tq:] + bpd_ref[g]], axis=1)
        key = lax.broadcasted_iota(jnp.int32, (rows, n_win), 1)
        valid = (key > qi) & (key <= qi + WINDOW) & (key >= WINDOW - q0)
        s = jnp.where(valid, s, MASK_VALUE)
        p = jnp.exp(s - jnp.max(s, axis=-1, keepdims=True)).astype(BF16)
        o_win = _normalised(_mm(p, vw_ref[0, pl.ds(diag_base, n_win), col]))

        for i, h in enumerate(heads):
            r = slice(i * tq, (i + 1) * tq)
            c0 = h * LANE
            o_ref[0, :, c0:c0 + LANE] = (gates[:, c0:c0 + LANE] * o_cmp[r]
                                         + gates[:, hw + c0:hw + c0 + LANE] * o_sel[r]
                                         + gates[:, 2 * hw + c0:2 * hw + c0 + LANE] * o_win[r])


def _rel_bucket(dist):
    n = jnp.maximum(dist, 0)
    max_exact = REL_BUCKETS // 2
    nf = jnp.maximum(n, 1).astype(F32)
    large = max_exact + (jnp.log(nf / max_exact) / math.log(REL_MAX_DIST / max_exact)
                         * (REL_BUCKETS - max_exact)).astype(jnp.int32)
    large = jnp.minimum(large, REL_BUCKETS - 1)
    return jnp.where(n < max_exact, n, large)


def _bias_tables(rel_bias):
    tbl = rel_bias.astype(F32).T
    i = jnp.arange(Q_BLOCK)
    dist = i[:, None] - i[None, :]
    diag = tbl[:, _rel_bucket(dist)]
    prev = tbl[:, _rel_bucket(dist + Q_BLOCK)]
    far = tbl[:, REL_BUCKETS - 1]
    m = jnp.arange(CMP_BAND_PAD)
    d_c = i[:, None] - CMP_STRIDE * (m[None, :] + CMP_BAND_LO) - (CMP_LEN - 1)
    band = tbl[:, _rel_bucket(d_c)] - far[:, None, None]
    band = jnp.where((m < CMP_BAND)[None, None, :], band, 0.0)
    band = band.at[:, :, CMP_BAND_PAD - 1].set(jnp.broadcast_to(far[:, None], (NSA_HEADS, Q_BLOCK)))
    band = band.reshape(NSA_GROUPS, NSA_HPG * Q_BLOCK, CMP_BAND_PAD)
    near = jnp.concatenate([prev, diag], axis=-1).reshape(NSA_GROUPS, NSA_HPG * Q_BLOCK, 2 * Q_BLOCK)
    return far, band, near


def _nsa(q, gates, k_cmp, v_cmp, k_sel, v_sel, k_win, v_win, tables, batch):
    m = q.shape[0]
    s = m // batch
    assert s % NSA_KEY_CHUNK == 0
    n_sel = s // SEL_LEN
    k_top = min(SEL_TOPK, n_sel)
    hw = NSA_HEADS * LANE
    kvw = NSA_GROUPS * LANE
    far, band, near = tables
    n_seg = k_cmp.shape[1]
    blk_of_key = jnp.arange(s)[:, None] // SEL_LEN == jnp.arange(n_sel)[None, :]
    key_blk = jnp.where(blk_of_key, MASK_VALUE, 0.0).astype(BF16)
    front = ((0, 0), (WINDOW, 0), (0, 0))
    k_win = jnp.pad(k_win.reshape(batch, s, kvw), front)
    v_win = jnp.pad(v_win.reshape(batch, s, kvw), front)
    per_batch = lambda n: pl.BlockSpec((1, n, kvw), lambda b, i: (b, 0, 0), pipeline_mode=pl.Buffered(1))
    tok = lambda w: pl.BlockSpec((1, Q_BLOCK, w), lambda b, i: (b, i, 0))
    out = pl.pallas_call(
        functools.partial(_nsa_kernel, n_sel=n_sel, k_top=k_top),
        grid=(batch, s // Q_BLOCK),
        in_specs=[pl.BlockSpec(memory_space=pltpu.SMEM),
                  tok(hw), tok(N_BRANCH * hw),
                  per_batch(n_seg), per_batch(n_seg),
                  per_batch(s), per_batch(s), _resident(key_blk.shape), per_batch(s + WINDOW), per_batch(s + WINDOW),
                  _resident(band.shape), _resident(near.shape)],
        out_specs=tok(hw),
        out_shape=jax.ShapeDtypeStruct((batch, s, hw), F32),
        compiler_params=_params("parallel", "arbitrary"),
        name="nsa_attend",
    )(far, q.reshape(batch, s, hw), gates.reshape(batch, s, N_BRANCH * hw), k_cmp, v_cmp,
      k_sel.reshape(batch, s, kvw), v_sel.reshape(batch, s, kvw), key_blk, k_win, v_win, band, near)
    return out.reshape(m, hw)


MERGE_TOKENS = 512


def _merge_kernel(x_ref, oa_ref, ob_ref, oc_ref, gate_ref, wa_ref, wb_ref, wc_ref, wo_ref, nw_ref, o_ref):
    d = x_ref.shape[1]
    merged = (gate_ref[:, 0:d] * _mm(oa_ref[...].astype(BF16), wa_ref[...])
              + gate_ref[:, d:2 * d] * _mm(ob_ref[...].astype(BF16), wb_ref[...])
              + gate_ref[:, 2 * d:3 * d] * _mm(oc_ref[...].astype(BF16), wc_ref[...]))
    y = _mm(merged.astype(BF16), wo_ref[...])
    o_ref[...] = x_ref[...] + _rms(y, nw_ref[...])


def _merge(x, o_a, o_b, o_c, gates, w_a, w_b, w_c, w_out, norm_post):
    m, d = x.shape
    row = lambda w: pl.BlockSpec((MERGE_TOKENS, w), lambda i: (i, 0))
    full = lambda a: _resident(a.shape)
    ws = [w.astype(BF16) for w in (w_a, w_b, w_c, w_out)]
    return pl.pallas_call(
        _merge_kernel,
        grid=(m // MERGE_TOKENS,),
        in_specs=[row(d), row(o_a.shape[1]), row(o_b.shape[1]), row(o_c.shape[1]), row(N_BRANCH * d),
                  full(ws[0]), full(ws[1]), full(ws[2]), full(ws[3]), _resident((1, d))],
        out_specs=row(d),
        out_shape=jax.ShapeDtypeStruct((m, d), F32),
        compiler_params=_params("parallel"),
        name="merge_out",
    )(x, o_a, o_b, o_c, gates, *ws, norm_post.reshape(1, d))


_IN_COLUMNS = (
    ("gdn_q", GDN_HEADS * GDN_DK), ("gdn_k", GDN_HEADS * GDN_DK), ("gdn_v", GDN_HEADS * GDN_DV),
    ("gdn_z", GDN_HEADS * GDN_DV), ("gdn_b", GDN_HEADS), ("gdn_a", GDN_HEADS),
    ("nsa_q", NSA_HEADS * NSA_DK), ("nsa_kv_cmp", NSA_GROUPS * (NSA_DK + NSA_DV)),
    ("nsa_kv_sel", NSA_GROUPS * (NSA_DK + NSA_DV)), ("nsa_kv_win", NSA_GROUPS * (NSA_DK + NSA_DV)),
    ("nsa_gate", 3 * NSA_HEADS),
    ("gla_q", GLA_HEADS * GLA_DK), ("gla_k", GLA_HEADS * GLA_DK), ("gla_v", GLA_HEADS * GLA_DV),
    ("gla_r", GLA_HEADS * GLA_DV), ("gla_a", GLA_RANK), ("merge_gate", None),
)


def _split_w_in(w_in):
    out, off = {}, 0
    for name, width in _IN_COLUMNS:
        width = w_in.shape[1] - off if width is None else width
        out[name] = w_in[:, off:off + width]
        off += width
    return out


def _pad_cols(w, width):
    return jnp.pad(w, ((0, 0), (0, width - w.shape[1])))


def _kv_layout(w):
    d = w.shape[0]
    w = w.reshape(d, NSA_GROUPS, NSA_DK + NSA_DV)
    k = jnp.pad(w[:, :, :NSA_DK], ((0, 0), (0, 0), (0, LANE - NSA_DK))).reshape(d, NSA_GROUPS * LANE)
    v = jnp.pad(w[:, :, NSA_DK:], ((0, 0), (0, 0), (0, LANE - NSA_DV))).reshape(d, NSA_GROUPS * LANE)
    return k, v


def _mix_pieces(w_in):
    d = w_in.shape[0]
    c = _split_w_in(w_in)
    gdn = [
        (jnp.concatenate([c["gdn_q"], c["gdn_k"], c["gdn_v"]], axis=1), F32, None),
        (c["gdn_z"], F32, None),
        (_pad_cols(jnp.concatenate([c["gdn_b"], c["gdn_a"]], axis=1), LANE), F32, None),
    ]
    gla = [
        (jnp.concatenate([c["gla_q"], c["gla_k"]], axis=1), F32, None),
        (c["gla_v"], F32, None),
        (c["gla_r"], F32, None),
        (_pad_cols(c["gla_a"], LANE), F32, None),
    ]
    q = jnp.pad(c["nsa_q"].reshape(d, NSA_HEADS, NSA_DK), ((0, 0), (0, 0), (0, LANE - NSA_DK)))
    k_sel, v_sel = _kv_layout(c["nsa_kv_sel"])
    k_win, v_win = _kv_layout(c["nsa_kv_win"])
    gate = c["nsa_gate"].reshape(d, NSA_HEADS, N_BRANCH).transpose(0, 2, 1)
    gate = jnp.broadcast_to(gate[..., None], (d, N_BRANCH, NSA_HEADS, LANE)).reshape(d, -1)
    nsa = [
        (q.reshape(d, NSA_HEADS * LANE), BF16, NSA_DK ** -0.5),
        (c["nsa_kv_cmp"], BF16, None),
        (k_sel, BF16, None), (v_sel, BF16, "ones_hi"), (k_win, BF16, None), (v_win, BF16, "ones_hi"),
        (gate, F32, "sigmoid"),
        (c["merge_gate"], F32, "sigmoid"),
    ]
    return gdn + gla, nsa


def _layer(x, batch, tables, p):
    x = _ffn(x, p["ffn1_norm_pre"], p["ffn1_w_gate_up"], p["ffn1_w_down"], p["ffn1_norm_post"])
    rec_pieces, nsa_pieces = _mix_pieces(p["w_in"])
    qkv, z, ba, gla_qk, gla_v, gla_r, gla_a = _proj(x, p["mix_norm_pre"], rec_pieces)
    nsa_q, kv_cmp, k_sel, v_sel, k_win, v_win, nsa_gate, merge_gate = _proj(x, p["mix_norm_pre"], nsa_pieces)
    o_a = _gdn(qkv, z, ba, p["gdn_conv_w"], p["gdn_a_log"], p["gdn_dt_bias"], p["gdn_norm_w"], batch)
    o_c = _gla(gla_qk, gla_v, gla_r, gla_a, p["gla_gate_w"], p["gla_gate_b"], p["gla_norm_w"], batch)
    cmp_w = _cmp_weights(p["nsa_pe_k"], p["nsa_cmp_k_w1"], p["nsa_cmp_k_w2"],
                         p["nsa_pe_v"], p["nsa_cmp_v_w1"], p["nsa_cmp_v_w2"])
    k_cmp, v_cmp = _compress(kv_cmp, cmp_w, batch)
    o_b = _nsa(nsa_q, nsa_gate, k_cmp, v_cmp, k_sel, v_sel, k_win, v_win, tables, batch)
    w_nsa = p["w_branch_nsa"].reshape(NSA_HEADS, NSA_DV, -1)
    w_nsa = jnp.pad(w_nsa, ((0, 0), (0, LANE - NSA_DV), (0, 0))).reshape(NSA_HEADS * LANE, -1)
    x = _merge(x, o_a, o_b, o_c, merge_gate, p["w_branch_gdn"], w_nsa, p["w_branch_gla"],
               p["w_out"], p["mix_norm_post"])
    return _ffn(x, p["ffn2_norm_pre"], p["ffn2_w_gate_up"], p["ffn2_w_down"], p["ffn2_norm_post"])


_LAYER_PARAMS = (
    "ffn1_norm_pre", "ffn1_w_gate_up", "ffn1_w_down", "ffn1_norm_post", "mix_norm_pre", "w_in",
    "gdn_conv_w", "gdn_a_log", "gdn_dt_bias", "gdn_norm_w",
    "nsa_pe_k", "nsa_cmp_k_w1", "nsa_cmp_k_w2", "nsa_pe_v", "nsa_cmp_v_w1", "nsa_cmp_v_w2",
    "gla_gate_w", "gla_gate_b", "gla_norm_w",
    "w_branch_gdn", "w_branch_nsa", "w_branch_gla", "w_out", "mix_norm_post",
    "ffn2_norm_pre", "ffn2_w_gate_up", "ffn2_w_down", "ffn2_norm_post",
)


def kernel(x, rel_bias, ffn1_norm_pre, ffn1_w_gate_up, ffn1_w_down, ffn1_norm_post, mix_norm_pre, w_in, gdn_conv_w, gdn_a_log, gdn_dt_bias, gdn_norm_w, nsa_pe_k, nsa_cmp_k_w1, nsa_cmp_k_w2, nsa_pe_v, nsa_cmp_v_w1, nsa_cmp_v_w2, gla_gate_w, gla_gate_b, gla_norm_w, w_branch_gdn, w_branch_nsa, w_branch_gla, w_out, mix_norm_post, ffn2_norm_pre, ffn2_w_gate_up, ffn2_w_down, ffn2_norm_post):
    stacked = dict(zip(_LAYER_PARAMS, (
        ffn1_norm_pre, ffn1_w_gate_up, ffn1_w_down, ffn1_norm_post, mix_norm_pre, w_in,
        gdn_conv_w, gdn_a_log, gdn_dt_bias, gdn_norm_w,
        nsa_pe_k, nsa_cmp_k_w1, nsa_cmp_k_w2, nsa_pe_v, nsa_cmp_v_w1, nsa_cmp_v_w2,
        gla_gate_w, gla_gate_b, gla_norm_w,
        w_branch_gdn, w_branch_nsa, w_branch_gla, w_out, mix_norm_post,
        ffn2_norm_pre, ffn2_w_gate_up, ffn2_w_down, ffn2_norm_post)))
    batch, seq, d = x.shape
    tables = _bias_tables(rel_bias)
    h = x.reshape(batch * seq, d)
    for layer in range(ffn1_norm_pre.shape[0]):
        h = _layer(h, batch, tables, {name: value[layer] for name, value in stacked.items()})
    return h.reshape(batch, seq, d)
```

```python
import functools
import math

import jax
import jax.numpy as jnp
from jax import lax
from jax.experimental import pallas as pl
from jax.experimental.pallas import tpu as pltpu

F32 = jnp.float32
BF16 = jnp.bfloat16
HIGHEST = lax.Precision.HIGHEST

LANE = 128
VMEM_LIMIT_BYTES = 56 * 1024 * 1024

NORM_EPS = 1e-6
FFN_RES_SCALE = 0.5
REL_BUCKETS = 32
REL_MAX_DIST = 128
GDN_HEADS, GDN_DK, GDN_DV, GDN_CONV, GDN_CHUNK = 4, 128, 128, 4, 64
NSA_HEADS, NSA_GROUPS, NSA_DK, NSA_DV = 8, 2, 96, 64
NSA_HPG = NSA_HEADS // NSA_GROUPS
CMP_LEN, CMP_STRIDE, CMP_HIDDEN = 32, 16, 256
SEL_LEN, SEL_TOPK, WINDOW, Q_BLOCK = 64, 16, 512, 128
GLA_HEADS, GLA_DK, GLA_DV, GLA_RANK, GLA_TAU, GLA_CHUNK = 4, 64, 128, 16, 16, 16
N_BRANCH = 3
MASK_VALUE = -1e30
L2_EPS = 1e-6

CMP_PER_Q = Q_BLOCK // CMP_STRIDE
CMP_BAND_LO = -10
CMP_BAND = 17
CMP_BAND_PAD = 32


def _nt(a, b, **kw):
    return lax.dot_general(a, b, (((1,), (1,)), ((), ())), preferred_element_type=F32, **kw)


def _tn(a, b, **kw):
    return lax.dot_general(a, b, (((0,), (0,)), ((), ())), preferred_element_type=F32, **kw)


def _mm(a, b, **kw):
    return jnp.dot(a, b, preferred_element_type=F32, **kw)


def _rms(x, w):
    return x * lax.rsqrt(jnp.mean(x * x, axis=-1, keepdims=True) + NORM_EPS) * w


def _silu(x):
    return x * jax.nn.sigmoid(x)


def _softplus(x):
    return jnp.maximum(x, 0.0) + jnp.log1p(jnp.exp(-jnp.abs(x)))


def _params(*semantics):
    return pltpu.CompilerParams(dimension_semantics=semantics, vmem_limit_bytes=VMEM_LIMIT_BYTES)


def _resident(shape):
    return pl.BlockSpec(shape, lambda *_: (0,) * len(shape), pipeline_mode=pl.Buffered(1))


FFN_TOKENS = 512
FFN_CHUNK = 256


def _ffn_kernel(x_ref, npre_ref, wgu_ref, wd_ref, npost_ref, o_ref):
    d_ff = wd_ref.shape[0]
    x = x_ref[...]
    h = _rms(x, npre_ref[...]).astype(BF16)
    acc = jnp.zeros(x.shape, F32)
    for c in range(d_ff // FFN_CHUNK):
        lo = c * FFN_CHUNK
        g = _mm(h, wgu_ref[:, lo:lo + FFN_CHUNK])
        u = _mm(h, wgu_ref[:, d_ff + lo:d_ff + lo + FFN_CHUNK])
        a = (_silu(g) * u).astype(BF16)
        acc = acc + _mm(a, wd_ref[lo:lo + FFN_CHUNK, :])
    o_ref[...] = x + FFN_RES_SCALE * _rms(acc, npost_ref[...])


def _ffn(x, norm_pre, w_gate_up, w_down, norm_post):
    m, d = x.shape
    d_ff = w_down.shape[0]
    row = pl.BlockSpec((FFN_TOKENS, d), lambda i: (i, 0))
    return pl.pallas_call(
        _ffn_kernel,
        grid=(m // FFN_TOKENS,),
        in_specs=[row, _resident((1, d)), _resident((d, 2 * d_ff)), _resident((d_ff, d)), _resident((1, d))],
        out_specs=row,
        out_shape=jax.ShapeDtypeStruct((m, d), F32),
        compiler_params=_params("parallel"),
        name="ffn",
    )(x, norm_pre.reshape(1, d), w_gate_up.astype(BF16), w_down.astype(BF16), norm_post.reshape(1, d))


PROJ_TOKENS = 256
PROJ_CHUNK = 512


def _proj_kernel(x_ref, nw_ref, w_ref, *o_refs, segs):
    h = _rms(x_ref[...], nw_ref[...]).astype(BF16)
    for (off, width, epilogue), o_ref in zip(segs, o_refs):
        for lo in range(0, width, PROJ_CHUNK):
            hi = min(lo + PROJ_CHUNK, width)
            y = _mm(h, w_ref[:, off + lo:off + hi])
            if epilogue == "sigmoid":
                y = jax.nn.sigmoid(y)
            elif epilogue == "ones_hi":
                lane = lax.broadcasted_iota(jnp.int32, y.shape, 1) % LANE
                y = jnp.where(lane >= LANE // 2, 1.0, y)
            elif epilogue is not None:
                y = y * epilogue
            o_ref[:, lo:hi] = y.astype(o_ref.dtype)


def _proj(x, norm_w, pieces):
    m, d = x.shape
    segs, cols, off = [], [], 0
    for w, _, epilogue in pieces:
        width = w.shape[1]
        pad = (-width) % LANE
        segs.append((off, width, epilogue))
        cols.append(w)
        if pad:
            cols.append(jnp.zeros((d, pad), w.dtype))
        off += width + pad
    w_all = jnp.concatenate(cols, axis=1).astype(BF16)
    row = pl.BlockSpec((PROJ_TOKENS, d), lambda i: (i, 0))
    return pl.pallas_call(
        functools.partial(_proj_kernel, segs=tuple(segs)),
        grid=(m // PROJ_TOKENS,),
        in_specs=[row, _resident((1, d)), _resident((d, off))],
        out_specs=[pl.BlockSpec((PROJ_TOKENS, w.shape[1]), lambda i: (i, 0)) for w, _, _ in pieces],
        out_shape=[jax.ShapeDtypeStruct((m, w.shape[1]), dt) for w, dt, _ in pieces],
        compiler_params=_params("parallel"),
        name="in_proj",
    )(x, norm_w.reshape(1, d), w_all)


GDN_TOKENS = 512
CONV_PAD = 8


def _split(x):
    hi = x.astype(BF16)
    return hi, (x - hi.astype(F32)).astype(BF16)


def _gdn_kernel(q_ref, k_ref, v_ref, z_ref, ba_ref, cwq_ref, cwk_ref, cwv_ref, alog_ref, dtb_ref, nw_ref,
                o_ref, xq_ref, xk_ref, xv_ref, gc_ref, cdec_ref,
                q16_ref, k16_ref, kb16_ref, qd_ref, kd_ref, vb_ref, kbe_ref, state_ref):
    t_blk = q_ref.shape[1]
    c_len = GDN_CHUNK
    width = GDN_HEADS * LANE

    @pl.when(pl.program_id(1) == 0)
    def _():
        state_ref[...] = jnp.zeros_like(state_ref)
        for x_ref in (xq_ref, xk_ref, xv_ref):
            x_ref[0:CONV_PAD, :] = jnp.zeros((CONV_PAD, width), F32)

    def conv_silu(raw_ref, x_ref, cw_ref):
        x_ref[CONV_PAD:CONV_PAD + t_blk, :] = raw_ref[0]
        y = jnp.zeros((t_blk, width), F32)
        for tap in range(GDN_CONV):
            y = y + cw_ref[tap:tap + 1, :] * x_ref[pl.ds(CONV_PAD - (GDN_CONV - 1) + tap, t_blk), :]
        x_ref[0:CONV_PAD, :] = x_ref[t_blk:t_blk + CONV_PAD, :]
        return _silu(y)

    q = conv_silu(q_ref, xq_ref, cwq_ref)
    k = conv_silu(k_ref, xk_ref, cwk_ref)
    v = conv_silu(v_ref, xv_ref, cwv_ref)

    ba = ba_ref[0]
    g_all = -jnp.exp(alog_ref[...]) * _softplus(ba + dtb_ref[...])
    src = lax.broadcasted_iota(jnp.int32, (LANE, width), 0)
    dst_head = lax.broadcasted_iota(jnp.int32, (LANE, width), 1) // LANE
    pick_b = (src == dst_head).astype(BF16)
    pick_a = (src == dst_head + GDN_HEADS).astype(BF16)
    b_hi, b_lo = _split(ba)
    g_hi, g_lo = _split(g_all)
    beta = jax.nn.sigmoid(_mm(b_hi, pick_b) + _mm(b_lo, pick_b))
    g_hi, g_lo = _split(_mm(g_hi, pick_a) + _mm(g_lo, pick_a))

    ti = lax.broadcasted_iota(jnp.int32, (t_blk, t_blk), 0)
    tj = lax.broadcasted_iota(jnp.int32, (t_blk, t_blk), 1)
    same = (ti // c_len) == (tj // c_len)
    in_chunk_tri = (same & (tj <= ti)).astype(BF16)
    in_chunk = same.astype(BF16)
    gc_all = _mm(in_chunk_tri, g_hi) + _mm(in_chunk_tri, g_lo)
    gl_all = _mm(in_chunk, g_hi) + _mm(in_chunk, g_lo)
    gc_ref[...] = gc_all
    cdec_ref[...] = jnp.exp(gl_all)
    e_gc = jnp.exp(gc_all)
    e_rest = jnp.exp(gl_all - gc_all)
    for h in range(GDN_HEADS):
        cols = slice(h * LANE, (h + 1) * LANE)
        qh, kh = q[:, cols], k[:, cols]
        qh = qh * lax.rsqrt(jnp.sum(qh * qh, axis=-1, keepdims=True) + L2_EPS) * (GDN_DK ** -0.5)
        kh = kh * lax.rsqrt(jnp.sum(kh * kh, axis=-1, keepdims=True) + L2_EPS)
        kb = kh * beta[:, cols]
        q16_ref[:, cols] = qh.astype(BF16)
        k16_ref[:, cols] = kh.astype(BF16)
        kb16_ref[:, cols] = kb.astype(BF16)
        qd_ref[:, cols] = (qh * e_gc[:, cols]).astype(BF16)
        kd_ref[:, cols] = (kh * e_rest[:, cols]).astype(BF16)
        vb_ref[:, cols] = (v[:, cols] * beta[:, cols]).astype(BF16)
        kbe_ref[:, cols] = (kb * e_gc[:, cols]).astype(BF16)

    ci = lax.broadcasted_iota(jnp.int32, (c_len, c_len), 0)
    cj = lax.broadcasted_iota(jnp.int32, (c_len, c_len), 1)
    lower = ci >= cj
    strict = ci > cj
    eye = (ci == cj).astype(F32)
    lane0 = (lax.broadcasted_iota(jnp.int32, (c_len, LANE), 1) == 0).astype(BF16)
    nw = nw_ref[...]

    def chunk_body(c, states):
        start = pl.multiple_of(c * c_len, c_len)
        rows = pl.ds(start, c_len)
        new_states = []
        for h in range(GDN_HEADS):
            cols = slice(h * LANE, (h + 1) * LANE)
            gc = gc_ref[rows, cols]
            gc_hi, gc_lo = _split(gc)
            gc_row = _nt(lane0, gc_hi) + _nt(lane0, gc_lo)
            gc_col = gc[:, :c_len]
            decay = jnp.exp(jnp.where(lower, gc_col - gc_row, MASK_VALUE))
            k16 = k16_ref[rows, cols]
            lmat = jnp.where(strict, _nt(kb16_ref[rows, cols], k16) * decay, 0.0)
            pw = -lmat
            inv = eye + pw
            for _ in range(5):
                p16 = pw.astype(BF16)
                pw = _mm(p16, p16)
                inv = inv + _mm(inv.astype(BF16), pw.astype(BF16))
            inv16 = inv.astype(BF16)
            u = _mm(inv16, vb_ref[rows, cols])
            w = _mm(inv16, kbe_ref[rows, cols])
            attn = _nt(q16_ref[rows, cols], k16) * decay
            s16 = states[h].astype(BF16)
            v_new = u - _mm(w.astype(BF16), s16)
            v16 = v_new.astype(BF16)
            o = _mm(qd_ref[rows, cols], s16) + _mm(attn.astype(BF16), v16)
            new_states.append(states[h] * cdec_ref[pl.ds(start, 1), cols] + _tn(kd_ref[rows, cols], v16))
            o_ref[0, rows, cols] = _rms(o, nw) * _silu(z_ref[0, rows, cols])
        return tuple(new_states)

    states = lax.fori_loop(0, t_blk // c_len, chunk_body, tuple(state_ref[h] for h in range(GDN_HEADS)))
    for h in range(GDN_HEADS):
        state_ref[h] = states[h]


def _gdn(qkv, z, ba, conv_w, a_log, dt_bias, norm_w, batch):
    m = qkv.shape[0]
    s = m // batch
    h, dk, dv = GDN_HEADS, GDN_DK, GDN_DV
    width = h * LANE
    qkv = qkv.reshape(batch, s, 3 * width)
    tok = lambda col: pl.BlockSpec((1, GDN_TOKENS, width), lambda b, i, col=col: (b, i, col))
    cw = lambda col: pl.BlockSpec((GDN_CONV, width), lambda b, i, col=col: (0, col))
    lane_pad = jnp.zeros((LANE - 2 * h,), F32)
    alog_row = jnp.concatenate([jnp.zeros((h,), F32), a_log, lane_pad]).reshape(1, LANE)
    dtb_row = jnp.concatenate([jnp.zeros((h,), F32), dt_bias, lane_pad]).reshape(1, LANE)
    out = pl.pallas_call(
        _gdn_kernel,
        grid=(batch, s // GDN_TOKENS),
        in_specs=[tok(0), tok(1), tok(2), tok(0),
                  pl.BlockSpec((1, GDN_TOKENS, LANE), lambda b, i: (b, i, 0)),
                  cw(0), cw(1), cw(2),
                  _resident((1, LANE)), _resident((1, LANE)), _resident((1, dv))],
        out_specs=tok(0),
        out_shape=jax.ShapeDtypeStruct((batch, s, width), F32),
        scratch_shapes=[pltpu.VMEM((GDN_TOKENS + CONV_PAD, width), F32)] * 3
        + [pltpu.VMEM((GDN_TOKENS, width), F32)] * 2 + [pltpu.VMEM((GDN_TOKENS, width), BF16)] * 7
        + [pltpu.VMEM((h, dk, dv), F32)],
        compiler_params=_params("parallel", "arbitrary"),
        name="gdn",
    )(qkv, qkv, qkv, z.reshape(batch, s, width), ba.reshape(batch, s, LANE),
      conv_w, conv_w, conv_w, alog_row, dtb_row, norm_w.reshape(1, dv))
    return out.reshape(m, width)


GLA_TOKENS = 256


def _log_sigmoid(x):
    return jnp.minimum(x, 0.0) - jnp.log1p(jnp.exp(-jnp.abs(x)))


def _gla_kernel(qk_ref, v_ref, r_ref, a_ref, gw_ref, gb_ref, nw_ref, o_ref, ks_ref, bs_ref, vs_ref, state_ref):
    t_blk = qk_ref.shape[1]
    c_len = GLA_CHUNK
    hdk = GLA_HEADS * GLA_DK
    hdv = GLA_HEADS * GLA_DV

    @pl.when(pl.program_id(1) == 0)
    def _():
        state_ref[...] = jnp.zeros_like(state_ref)
        ks_ref[0:c_len, :] = jnp.zeros((c_len, hdk), F32)
        bs_ref[0:c_len, :] = jnp.zeros((c_len, hdk), F32)
        vs_ref[0:c_len, :] = jnp.zeros((c_len, hdv), F32)

    qk = qk_ref[0]
    q = qk[:, :hdk] * (GLA_DK ** -0.5)
    k = qk[:, hdk:]
    v = v_ref[0]
    log_a = _log_sigmoid(_mm(a_ref[0].astype(BF16), gw_ref[...]) + gb_ref[...]) * (1.0 / GLA_TAU)

    ti = lax.broadcasted_iota(jnp.int32, (t_blk, t_blk), 0)
    tj = lax.broadcasted_iota(jnp.int32, (t_blk, t_blk), 1)
    same = (ti // c_len) == (tj // c_len)
    cum = _mm((same & (tj <= ti)).astype(F32), log_a, precision=HIGHEST)
    tot = _mm(same.astype(F32), log_a, precision=HIGHEST)

    ks_ref[c_len:, :] = k
    bs_ref[c_len:, :] = cum
    vs_ref[c_len:, :] = v
    pos = lax.broadcasted_iota(jnp.int32, (t_blk, 1), 0) % c_len
    hd = lax.broadcasted_iota(jnp.int32, (hdk, hdv), 0) // GLA_DK
    he = lax.broadcasted_iota(jnp.int32, (hdk, hdv), 1) // GLA_DV
    spread = (hd == he).astype(BF16)
    o = jnp.zeros((t_blk, hdv), F32)
    for off in range(c_len):
        k_o = ks_ref[pl.ds(c_len - off, t_blk), :]
        b_o = bs_ref[pl.ds(c_len - off, t_blk), :]
        v_o = vs_ref[pl.ds(c_len - off, t_blk), :]
        w = jnp.exp(jnp.where(pos >= off, cum - b_o, MASK_VALUE))
        o = o + _mm((q * k_o * w).astype(BF16), spread) * v_o

    q_dec = (q * jnp.exp(cum)).astype(BF16)
    k_dec = (k * jnp.exp(tot - cum)).astype(BF16)
    c_dec = jnp.exp(tot)
    v16 = v.astype(BF16)
    states = [state_ref[h] for h in range(GLA_HEADS)]
    inter = []
    for n in range(t_blk // c_len):
        rows = slice(n * c_len, (n + 1) * c_len)
        parts = []
        for h in range(GLA_HEADS):
            dks = slice(h * GLA_DK, (h + 1) * GLA_DK)
            dvs = slice(h * GLA_DV, (h + 1) * GLA_DV)
            parts.append(_nt(q_dec[rows, dks], states[h].astype(BF16)))
            states[h] = states[h] * c_dec[n * c_len:n * c_len + 1, dks] + _tn(v16[rows, dvs], k_dec[rows, dks])
        inter.append(jnp.concatenate(parts, axis=-1))
    for h in range(GLA_HEADS):
        state_ref[h] = states[h]
    o = o + jnp.concatenate(inter, axis=0)

    r = r_ref[0]
    nw = nw_ref[...]
    for h in range(GLA_HEADS):
        dvs = slice(h * GLA_DV, (h + 1) * GLA_DV)
        o_ref[0, :, dvs] = _rms(o[:, dvs], nw) * _silu(r[:, dvs])


def _gla(qk, v, r, a_low, gate_w, gate_b, norm_w, batch):
    m = qk.shape[0]
    s = m // batch
    hdk, hdv = GLA_HEADS * GLA_DK, GLA_HEADS * GLA_DV
    gw = jnp.zeros((LANE, hdk), F32).at[:GLA_RANK].set(gate_w).astype(BF16)
    blk = lambda w: pl.BlockSpec((1, GLA_TOKENS, w), lambda b, i: (b, i, 0))
    out = pl.pallas_call(
        _gla_kernel,
        grid=(batch, s // GLA_TOKENS),
        in_specs=[blk(2 * hdk), blk(hdv), blk(hdv), blk(LANE),
                  _resident((LANE, hdk)), _resident((1, hdk)), _resident((1, GLA_DV))],
        out_specs=blk(hdv),
        out_shape=jax.ShapeDtypeStruct((batch, s, hdv), F32),
        scratch_shapes=[pltpu.VMEM((GLA_TOKENS + GLA_CHUNK, hdk), F32),
                        pltpu.VMEM((GLA_TOKENS + GLA_CHUNK, hdk), F32),
                        pltpu.VMEM((GLA_TOKENS + GLA_CHUNK, hdv), F32),
                        pltpu.VMEM((GLA_HEADS, GLA_DV, GLA_DK), F32)],
        compiler_params=_params("parallel", "arbitrary"),
        name="gla",
    )(qk.reshape(batch, s, 2 * hdk), v.reshape(batch, s, hdv), r.reshape(batch, s, hdv),
      a_low.reshape(batch, s, LANE), gw, gate_b.reshape(1, hdk), norm_w.reshape(1, GLA_DV))
    return out.reshape(m, hdv)


def _gelu_tanh(x):
    return 0.5 * x * (1.0 + jnp.tanh(math.sqrt(2.0 / math.pi) * (x + 0.044715 * x * x * x)))


def _cmp_kernel(seg_ref, wa_ref, wb_ref, pek_ref, w1k_ref, pev_ref, w1v_ref, w2_ref, k_ref, v_ref):
    seg = seg_ref[0]
    n_seg = seg.shape[0]
    first = _mm(seg, wa_ref[...])
    second = _mm(seg, wb_ref[...])
    bias_k = _mm(pek_ref[...], w1k_ref[...])[0:1]
    bias_v = _mm(pev_ref[...], w1v_ref[...])[0:1]
    bias = jnp.concatenate([bias_k, bias_v] * NSA_GROUPS, axis=-1)
    hid = _gelu_tanh(first + pltpu.roll(second, n_seg - 1, 0) + bias).astype(BF16)
    out = _mm(hid, w2_ref[...])
    kw = NSA_GROUPS * LANE
    k_ref[0] = out[:, :kw].astype(BF16)
    v_ref[0] = out[:, kw:].astype(BF16)


def _cmp_weights(pe_k, w1k, w2k, pe_v, w1v, w2v):
    g, dk, dv, hid = NSA_GROUPS, NSA_DK, NSA_DV, CMP_HIDDEN
    w1k = w1k.reshape(CMP_LEN, dk, hid)
    w1v = w1v.reshape(CMP_LEN, dv, hid)

    def half(lo):
        blk = jnp.zeros((CMP_STRIDE, g, dk + dv, g, 2, hid), F32)
        for gg in range(g):
            blk = blk.at[:, gg, :dk, gg, 0].set(w1k[lo:lo + CMP_STRIDE])
            blk = blk.at[:, gg, dk:, gg, 1].set(w1v[lo:lo + CMP_STRIDE])
        return blk.reshape(CMP_STRIDE * g * (dk + dv), g * 2 * hid).astype(BF16)

    w2 = jnp.zeros((g, 2, hid, 2 * g * LANE), F32)
    for gg in range(g):
        w2 = w2.at[gg, 0, :, gg * LANE:gg * LANE + dk].set(w2k)
        w2 = w2.at[gg, 1, :, (g + gg) * LANE:(g + gg) * LANE + dv].set(w2v)
    w2 = w2.reshape(g * 2 * hid, -1).astype(BF16)
    pad8 = lambda pe: jnp.zeros((8, pe.size), F32).at[0].set(pe.reshape(-1)).astype(BF16)
    return half(0), half(CMP_STRIDE), pad8(pe_k), w1k.reshape(-1, hid).astype(BF16), pad8(pe_v), \
        w1v.reshape(-1, hid).astype(BF16), w2


def _compress(kv_cmp, weights, batch):
    m, width = kv_cmp.shape
    s = m // batch
    n_seg = s // CMP_STRIDE
    seg = kv_cmp.reshape(batch, n_seg, CMP_STRIDE * width)
    wa, wb, pek, w1k, pev, w1v, w2 = weights
    full = lambda a: _resident(a.shape)
    kw = vw = NSA_GROUPS * LANE
    return pl.pallas_call(
        _cmp_kernel,
        grid=(batch,),
        in_specs=[pl.BlockSpec((1, n_seg, CMP_STRIDE * width), lambda b: (b, 0, 0)),
                  full(wa), full(wb), full(pek), full(w1k), full(pev), full(w1v), full(w2)],
        out_specs=[pl.BlockSpec((1, n_seg, kw), lambda b: (b, 0, 0)),
                   pl.BlockSpec((1, n_seg, vw), lambda b: (b, 0, 0))],
        out_shape=[jax.ShapeDtypeStruct((batch, n_seg, kw), BF16), jax.ShapeDtypeStruct((batch, n_seg, vw), BF16)],
        compiler_params=_params("parallel"),
        name="nsa_compress",
    )(seg, wa, wb, pek, w1k, pev, w1v, w2)


NSA_KEY_CHUNK = 512


def _online_step(carry, s, v):
    m_old, acc = carry
    m_new = jnp.maximum(m_old, jnp.max(s, axis=-1, keepdims=True))
    p = jnp.exp(s - m_new).astype(BF16)
    return m_new, jnp.exp(m_old - m_new) * acc + _mm(p, v)


def _normalised(acc):
    low = lax.broadcasted_iota(jnp.int32, acc.shape, 1) < NSA_DV
    return jnp.where(low, acc / pltpu.roll(acc, NSA_DV, 1), 0.0)


def _nsa_kernel(far_ref, q_ref, gate_ref, kc_ref, vc_ref, ks_ref, vs_ref, eb_ref, kw_ref, vw_ref, gcb_ref, bpd_ref,
                o_ref, *, n_sel, k_top):
    qb = pl.program_id(1)
    tq = Q_BLOCK
    rows = NSA_HPG * tq
    n_cmp_pad = kc_ref.shape[1]
    q0 = qb * tq
    qi = lax.broadcasted_iota(jnp.int32, (rows, 1), 0) % tq
    row_t = q0 + qi

    cmp_n = lax.broadcasted_iota(jnp.int32, (rows, n_cmp_pad), 1)
    valid_c = row_t >= cmp_n * CMP_STRIDE + (CMP_LEN - 1)
    any_c = (row_t >= CMP_LEN - 1).astype(F32)
    band_n = lax.broadcasted_iota(jnp.int32, (n_cmp_pad, LANE), 0)
    band_l = lax.broadcasted_iota(jnp.int32, (n_cmp_pad, LANE), 1)
    band_m = band_l % CMP_BAND_PAD
    band_keys = ((band_l < 2 * CMP_BAND_PAD)
                 & ((band_n - CMP_PER_Q * qb == band_m + CMP_BAND_LO) | (band_m == CMP_BAND_PAD - 1))).astype(BF16)
    ov_s = lax.broadcasted_iota(jnp.int32, (n_sel, n_cmp_pad), 0) * SEL_LEN
    ov_c = lax.broadcasted_iota(jnp.int32, (n_sel, n_cmp_pad), 1) * CMP_STRIDE
    overlap_t = ((ov_c < ov_s + SEL_LEN) & (ov_c + CMP_LEN > ov_s)).astype(BF16)

    sel_s = lax.broadcasted_iota(jnp.int32, (n_sel, tq), 0)
    blk_t = (q0 + lax.broadcasted_iota(jnp.int32, (n_sel, tq), 1)) // SEL_LEN
    forced = (sel_s == 0) | (sel_s == blk_t) | (sel_s == blk_t - 1)
    future = sel_s > blk_t

    q_all = q_ref[0]
    gates = gate_ref[0]
    hw = NSA_HEADS * LANE
    prev_base = pl.multiple_of(jnp.maximum(qb - 1, 0) * tq, tq)
    diag_base = pl.multiple_of(q0, tq)

    def add_far(s, heads):
        return jnp.concatenate([s[i * tq:(i + 1) * tq] + far_ref[h] for i, h in enumerate(heads)], axis=0)

    for g in range(NSA_GROUPS):
        heads = range(g * NSA_HPG, (g + 1) * NSA_HPG)
        q4 = jnp.concatenate([q_all[:, h * LANE:(h + 1) * LANE] for h in heads], axis=0)
        col = slice(g * LANE, (g + 1) * LANE)

        s = _nt(jnp.concatenate([q4, gcb_ref[g]], axis=1), jnp.concatenate([kc_ref[0, :, col], band_keys], axis=1))
        s = jnp.where(valid_c, s, MASK_VALUE)
        p = jnp.exp(s - jnp.max(s, axis=-1, keepdims=True))
        p = p / jnp.sum(p, axis=-1, keepdims=True) * any_c
        o_cmp = _mm(p.astype(BF16), vc_ref[0, :, col])
        p_sum = p[0:tq]
        for i in range(1, NSA_HPG):
            p_sum = p_sum + p[i * tq:(i + 1) * tq]

        p_hi, p_lo = _split(p_sum)
        p_lo2 = (p_sum - p_hi.astype(F32) - p_lo.astype(F32)).astype(BF16)
        imp = _nt(overlap_t, p_hi) + (_nt(overlap_t, p_lo) + _nt(overlap_t, p_lo2))
        imp = jnp.where(forced, jnp.inf, jnp.where(future, -jnp.inf, imp))
        chosen = jnp.zeros((n_sel, tq), F32)
        for _ in range(k_top):
            best = jnp.max(imp, axis=0, keepdims=True)
            first = jnp.min(jnp.where(imp == best, sel_s, n_sel), axis=0, keepdims=True)
            pick = sel_s == first
            chosen = jnp.where(pick, 1.0, chosen)
            imp = jnp.where(pick, -jnp.inf, imp)
        skipped = 1.0 - chosen
        skipped_far = jnp.where(sel_s >= (tq // SEL_LEN) * (qb - 1), 1.0, skipped)
        q_far = jnp.concatenate([q4, jnp.concatenate([skipped_far.T.astype(BF16)] * NSA_HPG, axis=0)], axis=1)
        q_near = jnp.concatenate([q4, jnp.concatenate([skipped.T.astype(BF16)] * NSA_HPG, axis=0)], axis=1)

        def far_body(c, carry):
            base = pl.multiple_of(c * NSA_KEY_CHUNK, NSA_KEY_CHUNK)
            k_aug = jnp.concatenate([ks_ref[0, pl.ds(base, NSA_KEY_CHUNK), col],
                                     eb_ref[pl.ds(base, NSA_KEY_CHUNK), :]], axis=1)
            s = add_far(_nt(q_far, k_aug), heads)
            return _online_step(carry, s, vs_ref[0, pl.ds(base, NSA_KEY_CHUNK), col])

        n_far = (jnp.maximum(qb - 1, 0) * tq + NSA_KEY_CHUNK - 1) // NSA_KEY_CHUNK
        carry = lax.fori_loop(0, n_far, far_body,
                              (jnp.full((rows, 1), MASK_VALUE, F32), jnp.zeros((rows, LANE), F32)))

        k_aug = jnp.concatenate(
            [jnp.concatenate([ks_ref[0, pl.ds(prev_base, tq), col], ks_ref[0, pl.ds(diag_base, tq), col]], axis=0),
             jnp.concatenate([eb_ref[pl.ds(prev_base, tq), :], eb_ref[pl.ds(diag_base, tq), :]], axis=0)], axis=1)
        v_near = jnp.concatenate([vs_ref[0, pl.ds(prev_base, tq), col], vs_ref[0, pl.ds(diag_base, tq), col]], axis=0)
        key = lax.broadcasted_iota(jnp.int32, (rows, 2 * tq), 1)
        first_key = jnp.where(qb >= 1, 0, tq)
        valid = (key >= first_key) & (key <= qi + tq)
        s = jnp.where(valid, _nt(q_near, k_aug) + bpd_ref[g], MASK_VALUE)
        o_sel = _normalised(_online_step(carry, s, v_near)[1])

        n_win = WINDOW + tq
        s = _nt(q4, kw_ref[0, pl.ds(diag_base, n_win), col])
        s = jnp.concatenate([add_far(s[:, :n_win - 2 * tq], heads), s[:, n_win - 2 * tq:] + bpd_ref[g]], axis=1)
        key = lax.broadcasted_iota(jnp.int32, (rows, n_win), 1)
        valid = (key > qi) & (key <= qi + WINDOW) & (key >= WINDOW - q0)
        s = jnp.where(valid, s, MASK_VALUE)
        p = jnp.exp(s - jnp.max(s, axis=-1, keepdims=True)).astype(BF16)
        o_win = _normalised(_mm(p, vw_ref[0, pl.ds(diag_base, n_win), col]))

        for i, h in enumerate(heads):
            r = slice(i * tq, (i + 1) * tq)
            c0 = h * LANE
            o_ref[0, :, c0:c0 + LANE] = (gates[:, c0:c0 + LANE] * o_cmp[r]
                                         + gates[:, hw + c0:hw + c0 + LANE] * o_sel[r]
                                         + gates[:, 2 * hw + c0:2 * hw + c0 + LANE] * o_win[r])


def _rel_bucket(dist):
    n = jnp.maximum(dist, 0)
    max_exact = REL_BUCKETS // 2
    nf = jnp.maximum(n, 1).astype(F32)
    large = max_exact + (jnp.log(nf / max_exact) / math.log(REL_MAX_DIST / max_exact)
                         * (REL_BUCKETS - max_exact)).astype(jnp.int32)
    large = jnp.minimum(large, REL_BUCKETS - 1)
    return jnp.where(n < max_exact, n, large)


def _bias_tables(rel_bias):
    tbl = rel_bias.astype(F32).T
    i = jnp.arange(Q_BLOCK)
    dist = i[:, None] - i[None, :]
    diag = tbl[:, _rel_bucket(dist)]
    prev = tbl[:, _rel_bucket(dist + Q_BLOCK)]
    far = tbl[:, REL_BUCKETS - 1]
    m = jnp.arange(CMP_BAND_PAD)
    d_c = i[:, None] - CMP_STRIDE * (m[None, :] + CMP_BAND_LO) - (CMP_LEN - 1)
    band = tbl[:, _rel_bucket(d_c)] - far[:, None, None]
    band = jnp.where((m < CMP_BAND)[None, None, :], band, 0.0)
    band = band.at[:, :, CMP_BAND_PAD - 1].set(jnp.broadcast_to(far[:, None], (NSA_HEADS, Q_BLOCK)))
    band = band.reshape(NSA_GROUPS, NSA_HPG * Q_BLOCK, CMP_BAND_PAD)
    band_hi = band.astype(BF16)
    band_lo = (band - band_hi.astype(F32)).astype(BF16)
    band = jnp.pad(jnp.concatenate([band_hi, band_lo], axis=-1), ((0, 0), (0, 0), (0, LANE - 2 * CMP_BAND_PAD)))
    near = jnp.concatenate([prev, diag], axis=-1).reshape(NSA_GROUPS, NSA_HPG * Q_BLOCK, 2 * Q_BLOCK)
    return far, band, near


def _nsa(q, gates, k_cmp, v_cmp, k_sel, v_sel, k_win, v_win, tables, batch):
    m = q.shape[0]
    s = m // batch
    assert s % NSA_KEY_CHUNK == 0
    n_sel = s // SEL_LEN
    k_top = min(SEL_TOPK, n_sel)
    hw = NSA_HEADS * LANE
    kvw = NSA_GROUPS * LANE
    far, band, near = tables
    n_seg = k_cmp.shape[1]
    blk_of_key = jnp.arange(s)[:, None] // SEL_LEN == jnp.arange(n_sel)[None, :]
    key_blk = jnp.where(blk_of_key, MASK_VALUE, 0.0).astype(BF16)
    front = ((0, 0), (WINDOW, 0), (0, 0))
    k_win = jnp.pad(k_win.reshape(batch, s, kvw), front)
    v_win = jnp.pad(v_win.reshape(batch, s, kvw), front)
    per_batch = lambda n: pl.BlockSpec((1, n, kvw), lambda b, i: (b, 0, 0), pipeline_mode=pl.Buffered(1))
    tok = lambda w: pl.BlockSpec((1, Q_BLOCK, w), lambda b, i: (b, i, 0))
    out = pl.pallas_call(
        functools.partial(_nsa_kernel, n_sel=n_sel, k_top=k_top),
        grid=(batch, s // Q_BLOCK),
        in_specs=[pl.BlockSpec(memory_space=pltpu.SMEM),
                  tok(hw), tok(N_BRANCH * hw),
                  per_batch(n_seg), per_batch(n_seg),
                  per_batch(s), per_batch(s), _resident(key_blk.shape), per_batch(s + WINDOW), per_batch(s + WINDOW),
                  _resident(band.shape), _resident(near.shape)],
        out_specs=tok(hw),
        out_shape=jax.ShapeDtypeStruct((batch, s, hw), F32),
        compiler_params=_params("parallel", "arbitrary"),
        name="nsa_attend",
    )(far, q.reshape(batch, s, hw), gates.reshape(batch, s, N_BRANCH * hw), k_cmp, v_cmp,
      k_sel.reshape(batch, s, kvw), v_sel.reshape(batch, s, kvw), key_blk, k_win, v_win, band, near)
    return out.reshape(m, hw)


MERGE_TOKENS = 512


def _merge_kernel(x_ref, oa_ref, ob_ref, oc_ref, gate_ref, wa_ref, wb_ref, wc_ref, wo_ref, nw_ref, o_ref):
    d = x_ref.shape[1]
    merged = (gate_ref[:, 0:d] * _mm(oa_ref[...].astype(BF16), wa_ref[...])
              + gate_ref[:, d:2 * d] * _mm(ob_ref[...].astype(BF16), wb_ref[...])
              + gate_ref[:, 2 * d:3 * d] * _mm(oc_ref[...].astype(BF16), wc_ref[...]))
    y = _mm(merged.astype(BF16), wo_ref[...])
    o_ref[...] = x_ref[...] + _rms(y, nw_ref[...])


def _merge(x, o_a, o_b, o_c, gates, w_a, w_b, w_c, w_out, norm_post):
    m, d = x.shape
    row = lambda w: pl.BlockSpec((MERGE_TOKENS, w), lambda i: (i, 0))
    full = lambda a: _resident(a.shape)
    ws = [w.astype(BF16) for w in (w_a, w_b, w_c, w_out)]
    return pl.pallas_call(
        _merge_kernel,
        grid=(m // MERGE_TOKENS,),
        in_specs=[row(d), row(o_a.shape[1]), row(o_b.shape[1]), row(o_c.shape[1]), row(N_BRANCH * d),
                  full(ws[0]), full(ws[1]), full(ws[2]), full(ws[3]), _resident((1, d))],
        out_specs=row(d),
        out_shape=jax.ShapeDtypeStruct((m, d), F32),
        compiler_params=_params("parallel"),
        name="merge_out",
    )(x, o_a, o_b, o_c, gates, *ws, norm_post.reshape(1, d))


_IN_COLUMNS = (
    ("gdn_q", GDN_HEADS * GDN_DK), ("gdn_k", GDN_HEADS * GDN_DK), ("gdn_v", GDN_HEADS * GDN_DV),
    ("gdn_z", GDN_HEADS * GDN_DV), ("gdn_b", GDN_HEADS), ("gdn_a", GDN_HEADS),
    ("nsa_q", NSA_HEADS * NSA_DK), ("nsa_kv_cmp", NSA_GROUPS * (NSA_DK + NSA_DV)),
    ("nsa_kv_sel", NSA_GROUPS * (NSA_DK + NSA_DV)), ("nsa_kv_win", NSA_GROUPS * (NSA_DK + NSA_DV)),
    ("nsa_gate", 3 * NSA_HEADS),
    ("gla_q", GLA_HEADS * GLA_DK), ("gla_k", GLA_HEADS * GLA_DK), ("gla_v", GLA_HEADS * GLA_DV),
    ("gla_r", GLA_HEADS * GLA_DV), ("gla_a", GLA_RANK), ("merge_gate", None),
)


def _split_w_in(w_in):
    out, off = {}, 0
    for name, width in _IN_COLUMNS:
        width = w_in.shape[1] - off if width is None else width
        out[name] = w_in[:, off:off + width]
        off += width
    return out


def _pad_cols(w, width):
    return jnp.pad(w, ((0, 0), (0, width - w.shape[1])))


def _kv_layout(w):
    d = w.shape[0]
    w = w.reshape(d, NSA_GROUPS, NSA_DK + NSA_DV)
    k = jnp.pad(w[:, :, :NSA_DK], ((0, 0), (0, 0), (0, LANE - NSA_DK))).reshape(d, NSA_GROUPS * LANE)
    v = jnp.pad(w[:, :, NSA_DK:], ((0, 0), (0, 0), (0, LANE - NSA_DV))).reshape(d, NSA_GROUPS * LANE)
    return k, v


def _mix_pieces(w_in):
    d = w_in.shape[0]
    c = _split_w_in(w_in)
    gdn = [
        (jnp.concatenate([c["gdn_q"], c["gdn_k"], c["gdn_v"]], axis=1), F32, None),
        (c["gdn_z"], F32, None),
        (_pad_cols(jnp.concatenate([c["gdn_b"], c["gdn_a"]], axis=1), LANE), F32, None),
    ]
    gla = [
        (jnp.concatenate([c["gla_q"], c["gla_k"]], axis=1), F32, None),
        (c["gla_v"], F32, None),
        (c["gla_r"], F32, None),
        (_pad_cols(c["gla_a"], LANE), F32, None),
    ]
    q = jnp.pad(c["nsa_q"].reshape(d, NSA_HEADS, NSA_DK), ((0, 0), (0, 0), (0, LANE - NSA_DK)))
    k_sel, v_sel = _kv_layout(c["nsa_kv_sel"])
    k_win, v_win = _kv_layout(c["nsa_kv_win"])
    gate = c["nsa_gate"].reshape(d, NSA_HEADS, N_BRANCH).transpose(0, 2, 1)
    gate = jnp.broadcast_to(gate[..., None], (d, N_BRANCH, NSA_HEADS, LANE)).reshape(d, -1)
    nsa = [
        (q.reshape(d, NSA_HEADS * LANE), BF16, NSA_DK ** -0.5),
        (c["nsa_kv_cmp"], BF16, None),
        (k_sel, BF16, None), (v_sel, BF16, "ones_hi"), (k_win, BF16, None), (v_win, BF16, "ones_hi"),
        (gate, F32, "sigmoid"),
        (c["merge_gate"], F32, "sigmoid"),
    ]
    return gdn + gla, nsa


def _layer(x, batch, tables, p):
    x = _ffn(x, p["ffn1_norm_pre"], p["ffn1_w_gate_up"], p["ffn1_w_down"], p["ffn1_norm_post"])
    rec_pieces, nsa_pieces = _mix_pieces(p["w_in"])
    qkv, z, ba, gla_qk, gla_v, gla_r, gla_a = _proj(x, p["mix_norm_pre"], rec_pieces)
    nsa_q, kv_cmp, k_sel, v_sel, k_win, v_win, nsa_gate, merge_gate = _proj(x, p["mix_norm_pre"], nsa_pieces)
    o_a = _gdn(qkv, z, ba, p["gdn_conv_w"], p["gdn_a_log"], p["gdn_dt_bias"], p["gdn_norm_w"], batch)
    o_c = _gla(gla_qk, gla_v, gla_r, gla_a, p["gla_gate_w"], p["gla_gate_b"], p["gla_norm_w"], batch)
    cmp_w = _cmp_weights(p["nsa_pe_k"], p["nsa_cmp_k_w1"], p["nsa_cmp_k_w2"],
                         p["nsa_pe_v"], p["nsa_cmp_v_w1"], p["nsa_cmp_v_w2"])
    k_cmp, v_cmp = _compress(kv_cmp, cmp_w, batch)
    o_b = _nsa(nsa_q, nsa_gate, k_cmp, v_cmp, k_sel, v_sel, k_win, v_win, tables, batch)
    w_nsa = p["w_branch_nsa"].reshape(NSA_HEADS, NSA_DV, -1)
    w_nsa = jnp.pad(w_nsa, ((0, 0), (0, LANE - NSA_DV), (0, 0))).reshape(NSA_HEADS * LANE, -1)
    x = _merge(x, o_a, o_b, o_c, merge_gate, p["w_branch_gdn"], w_nsa, p["w_branch_gla"],
               p["w_out"], p["mix_norm_post"])
    return _ffn(x, p["ffn2_norm_pre"], p["ffn2_w_gate_up"], p["ffn2_w_down"], p["ffn2_norm_post"])


_LAYER_PARAMS = (
    "ffn1_norm_pre", "ffn1_w_gate_up", "ffn1_w_down", "ffn1_norm_post", "mix_norm_pre", "w_in",
    "gdn_conv_w", "gdn_a_log", "gdn_dt_bias", "gdn_norm_w",
    "nsa_pe_k", "nsa_cmp_k_w1", "nsa_cmp_k_w2", "nsa_pe_v", "nsa_cmp_v_w1", "nsa_cmp_v_w2",
    "gla_gate_w", "gla_gate_b", "gla_norm_w",
    "w_branch_gdn", "w_branch_nsa", "w_branch_gla", "w_out", "mix_norm_post",
    "ffn2_norm_pre", "ffn2_w_gate_up", "ffn2_w_down", "ffn2_norm_post",
)


def kernel(x, rel_bias, ffn1_norm_pre, ffn1_w_gate_up, ffn1_w_down, ffn1_norm_post, mix_norm_pre, w_in, gdn_conv_w, gdn_a_log, gdn_dt_bias, gdn_norm_w, nsa_pe_k, nsa_cmp_k_w1, nsa_cmp_k_w2, nsa_pe_v, nsa_cmp_v_w1, nsa_cmp_v_w2, gla_gate_w, gla_gate_b, gla_norm_w, w_branch_gdn, w_branch_nsa, w_branch_gla, w_out, mix_norm_post, ffn2_norm_pre, ffn2_w_gate_up, ffn2_w_down, ffn2_norm_post):
    stacked = dict(zip(_LAYER_PARAMS, (
        ffn1_norm_pre, ffn1_w_gate_up, ffn1_w_down, ffn1_norm_post, mix_norm_pre, w_in,
        gdn_conv_w, gdn_a_log, gdn_dt_bias, gdn_norm_w,
        nsa_pe_k, nsa_cmp_k_w1, nsa_cmp_k_w2, nsa_pe_v, nsa_cmp_v_w1, nsa_cmp_v_w2,
        gla_gate_w, gla_gate_b, gla_norm_w,
        w_branch_gdn, w_branch_nsa, w_branch_gla, w_out, mix_norm_post,
        ffn2_norm_pre, ffn2_w_gate_up, ffn2_w_down, ffn2_norm_post)))
    batch, seq, d = x.shape
    tables = _bias_tables(rel_bias)
    h = x.reshape(batch * seq, d)
    for layer in range(ffn1_norm_pre.shape[0]):
        h = _layer(h, batch, tables, {name: value[layer] for name, value in stacked.items()})
    return h.reshape(batch, seq, d)
```

```python
import functools
import math

import jax
import jax.numpy as jnp
from jax import lax
from jax.experimental import pallas as pl
from jax.experimental.pallas import tpu as pltpu

F32 = jnp.float32
BF16 = jnp.bfloat16
HIGHEST = lax.Precision.HIGHEST

LANE = 128
VMEM_LIMIT_BYTES = 56 * 1024 * 1024

NORM_EPS = 1e-6
FFN_RES_SCALE = 0.5
REL_BUCKETS = 32
REL_MAX_DIST = 128
GDN_HEADS, GDN_DK, GDN_DV, GDN_CONV, GDN_CHUNK = 4, 128, 128, 4, 64
NSA_HEADS, NSA_GROUPS, NSA_DK, NSA_DV = 8, 2, 96, 64
NSA_HPG = NSA_HEADS // NSA_GROUPS
CMP_LEN, CMP_STRIDE, CMP_HIDDEN = 32, 16, 256
SEL_LEN, SEL_TOPK, WINDOW, Q_BLOCK = 64, 16, 512, 128
GLA_HEADS, GLA_DK, GLA_DV, GLA_RANK, GLA_TAU, GLA_CHUNK = 4, 64, 128, 16, 16, 16
N_BRANCH = 3
MASK_VALUE = -1e30
L2_EPS = 1e-6

CMP_PER_Q = Q_BLOCK // CMP_STRIDE
CMP_BAND_LO = -10
CMP_BAND = 17
CMP_BAND_PAD = 32


def _nt(a, b, **kw):
    return lax.dot_general(a, b, (((1,), (1,)), ((), ())), preferred_element_type=F32, **kw)


def _tn(a, b, **kw):
    return lax.dot_general(a, b, (((0,), (0,)), ((), ())), preferred_element_type=F32, **kw)


def _mm(a, b, **kw):
    return jnp.dot(a, b, preferred_element_type=F32, **kw)


def _rms(x, w):
    return x * lax.rsqrt(jnp.mean(x * x, axis=-1, keepdims=True) + NORM_EPS) * w


def _silu(x):
    return x * jax.nn.sigmoid(x)


def _softplus(x):
    return jnp.maximum(x, 0.0) + jnp.log1p(jnp.exp(-jnp.abs(x)))


def _params(*semantics):
    return pltpu.CompilerParams(dimension_semantics=semantics, vmem_limit_bytes=VMEM_LIMIT_BYTES)


def _resident(shape):
    return pl.BlockSpec(shape, lambda *_: (0,) * len(shape), pipeline_mode=pl.Buffered(1))


FFN_TOKENS = 512
FFN_CHUNK = 256


def _ffn_kernel(x_ref, npre_ref, wgu_ref, wd_ref, npost_ref, o_ref):
    d_ff = wd_ref.shape[0]
    x = x_ref[...]
    h = _rms(x, npre_ref[...]).astype(BF16)
    acc = jnp.zeros(x.shape, F32)
    for c in range(d_ff // FFN_CHUNK):
        lo = c * FFN_CHUNK
        g = _mm(h, wgu_ref[:, lo:lo + FFN_CHUNK])
        u = _mm(h, wgu_ref[:, d_ff + lo:d_ff + lo + FFN_CHUNK])
        a = (_silu(g) * u).astype(BF16)
        acc = acc + _mm(a, wd_ref[lo:lo + FFN_CHUNK, :])
    o_ref[...] = x + FFN_RES_SCALE * _rms(acc, npost_ref[...])


def _ffn(x, norm_pre, w_gate_up, w_down, norm_post):
    m, d = x.shape
    d_ff = w_down.shape[0]
    row = pl.BlockSpec((FFN_TOKENS, d), lambda i: (i, 0))
    return pl.pallas_call(
        _ffn_kernel,
        grid=(m // FFN_TOKENS,),
        in_specs=[row, _resident((1, d)), _resident((d, 2 * d_ff)), _resident((d_ff, d)), _resident((1, d))],
        out_specs=row,
        out_shape=jax.ShapeDtypeStruct((m, d), F32),
        compiler_params=_params("parallel"),
        name="ffn",
    )(x, norm_pre.reshape(1, d), w_gate_up.astype(BF16), w_down.astype(BF16), norm_post.reshape(1, d))


PROJ_TOKENS = 256
PROJ_CHUNK = 512


def _proj_kernel(x_ref, nw_ref, w_ref, *o_refs, segs):
    h = _rms(x_ref[...], nw_ref[...]).astype(BF16)
    for (off, width, epilogue), o_ref in zip(segs, o_refs):
        for lo in range(0, width, PROJ_CHUNK):
            hi = min(lo + PROJ_CHUNK, width)
            y = _mm(h, w_ref[:, off + lo:off + hi])
            if epilogue == "sigmoid":
                y = jax.nn.sigmoid(y)
            elif epilogue == "ones_hi":
                lane = lax.broadcasted_iota(jnp.int32, y.shape, 1) % LANE
                y = jnp.where(lane >= LANE // 2, 1.0, y)
            elif epilogue is not None:
                y = y * epilogue
            o_ref[:, lo:hi] = y.astype(o_ref.dtype)


def _proj(x, norm_w, pieces):
    m, d = x.shape
    segs, cols, off = [], [], 0
    for w, _, epilogue in pieces:
        width = w.shape[1]
        pad = (-width) % LANE
        segs.append((off, width, epilogue))
        cols.append(w)
        if pad:
            cols.append(jnp.zeros((d, pad), w.dtype))
        off += width + pad
    w_all = jnp.concatenate(cols, axis=1).astype(BF16)
    row = pl.BlockSpec((PROJ_TOKENS, d), lambda i: (i, 0))
    return pl.pallas_call(
        functools.partial(_proj_kernel, segs=tuple(segs)),
        grid=(m // PROJ_TOKENS,),
        in_specs=[row, _resident((1, d)), _resident((d, off))],
        out_specs=[pl.BlockSpec((PROJ_TOKENS, w.shape[1]), lambda i: (i, 0)) for w, _, _ in pieces],
        out_shape=[jax.ShapeDtypeStruct((m, w.shape[1]), dt) for w, dt, _ in pieces],
        compiler_params=_params("parallel"),
        name="in_proj",
    )(x, norm_w.reshape(1, d), w_all)


GDN_TOKENS = 512
CONV_PAD = 8


def _split(x):
    hi = x.astype(BF16)
    return hi, (x - hi.astype(F32)).astype(BF16)


def _gdn_kernel(q_ref, k_ref, v_ref, z_ref, ba_ref, cwq_ref, cwk_ref, cwv_ref, alog_ref, dtb_ref, nw_ref,
                o_ref, xq_ref, xk_ref, xv_ref, gc_ref, cdec_ref,
                q16_ref, k16_ref, kb16_ref, qd_ref, kd_ref, vb_ref, kbe_ref, state_ref):
    t_blk = q_ref.shape[1]
    c_len = GDN_CHUNK
    width = GDN_HEADS * LANE

    @pl.when(pl.program_id(1) == 0)
    def _():
        state_ref[...] = jnp.zeros_like(state_ref)
        for x_ref in (xq_ref, xk_ref, xv_ref):
            x_ref[0:CONV_PAD, :] = jnp.zeros((CONV_PAD, width), F32)

    def conv_silu(raw_ref, x_ref, cw_ref):
        x_ref[CONV_PAD:CONV_PAD + t_blk, :] = raw_ref[0]
        y = jnp.zeros((t_blk, width), F32)
        for tap in range(GDN_CONV):
            y = y + cw_ref[tap:tap + 1, :] * x_ref[pl.ds(CONV_PAD - (GDN_CONV - 1) + tap, t_blk), :]
        x_ref[0:CONV_PAD, :] = x_ref[t_blk:t_blk + CONV_PAD, :]
        return _silu(y)

    q = conv_silu(q_ref, xq_ref, cwq_ref)
    k = conv_silu(k_ref, xk_ref, cwk_ref)
    v = conv_silu(v_ref, xv_ref, cwv_ref)

    ba = ba_ref[0]
    g_all = -jnp.exp(alog_ref[...]) * _softplus(ba + dtb_ref[...])
    src = lax.broadcasted_iota(jnp.int32, (LANE, width), 0)
    dst_head = lax.broadcasted_iota(jnp.int32, (LANE, width), 1) // LANE
    pick_b = (src == dst_head).astype(BF16)
    pick_a = (src == dst_head + GDN_HEADS).astype(BF16)
    b_hi, b_lo = _split(ba)
    g_hi, g_lo = _split(g_all)
    beta = jax.nn.sigmoid(_mm(b_hi, pick_b) + _mm(b_lo, pick_b))
    g_hi, g_lo = _split(_mm(g_hi, pick_a) + _mm(g_lo, pick_a))

    ti = lax.broadcasted_iota(jnp.int32, (t_blk, t_blk), 0)
    tj = lax.broadcasted_iota(jnp.int32, (t_blk, t_blk), 1)
    same = (ti // c_len) == (tj // c_len)
    in_chunk_tri = (same & (tj <= ti)).astype(BF16)
    in_chunk = same.astype(BF16)
    gc_all = _mm(in_chunk_tri, g_hi) + _mm(in_chunk_tri, g_lo)
    gl_all = _mm(in_chunk, g_hi) + _mm(in_chunk, g_lo)
    gc_ref[...] = gc_all
    cdec_ref[...] = jnp.exp(gl_all)
    e_gc = jnp.exp(gc_all)
    e_rest = jnp.exp(gl_all - gc_all)
    for h in range(GDN_HEADS):
        cols = slice(h * LANE, (h + 1) * LANE)
        qh, kh = q[:, cols], k[:, cols]
        qh = qh * lax.rsqrt(jnp.sum(qh * qh, axis=-1, keepdims=True) + L2_EPS) * (GDN_DK ** -0.5)
        kh = kh * lax.rsqrt(jnp.sum(kh * kh, axis=-1, keepdims=True) + L2_EPS)
        kb = kh * beta[:, cols]
        q16_ref[:, cols] = qh.astype(BF16)
        k16_ref[:, cols] = kh.astype(BF16)
        kb16_ref[:, cols] = kb.astype(BF16)
        qd_ref[:, cols] = (qh * e_gc[:, cols]).astype(BF16)
        kd_ref[:, cols] = (kh * e_rest[:, cols]).astype(BF16)
        vb_ref[:, cols] = (v[:, cols] * beta[:, cols]).astype(BF16)
        kbe_ref[:, cols] = (kb * e_gc[:, cols]).astype(BF16)

    ci = lax.broadcasted_iota(jnp.int32, (c_len, c_len), 0)
    cj = lax.broadcasted_iota(jnp.int32, (c_len, c_len), 1)
    lower = ci >= cj
    strict = ci > cj
    eye = (ci == cj).astype(F32)
    lane0 = (lax.broadcasted_iota(jnp.int32, (c_len, LANE), 1) == 0).astype(BF16)
    nw = nw_ref[...]

    def chunk_body(c, states):
        start = pl.multiple_of(c * c_len, c_len)
        rows = pl.ds(start, c_len)
        heads = range(GDN_HEADS)
        cols = [slice(h * LANE, (h + 1) * LANE) for h in heads]
        gc = [gc_ref[rows, cols[h]] for h in heads]
        gc_split = [_split(x) for x in gc]
        gc_row = [_nt(lane0, hi) + _nt(lane0, lo) for hi, lo in gc_split]
        decay = [jnp.exp(jnp.where(lower, gc[h][:, :c_len] - gc_row[h], MASK_VALUE)) for h in heads]
        k16 = [k16_ref[rows, cols[h]] for h in heads]
        kk = [_nt(kb16_ref[rows, cols[h]], k16[h]) for h in heads]
        pw = [jnp.where(strict, -(kk[h] * decay[h]), 0.0) for h in heads]
        inv = [eye + p for p in pw]
        for _ in range(5):
            p16 = [p.astype(BF16) for p in pw]
            pw = [_mm(p, p) for p in p16]
            inv = [inv[h] + _mm(inv[h].astype(BF16), pw[h].astype(BF16)) for h in heads]
        inv16 = [x.astype(BF16) for x in inv]
        u = [_mm(inv16[h], vb_ref[rows, cols[h]]) for h in heads]
        w = [_mm(inv16[h], kbe_ref[rows, cols[h]]).astype(BF16) for h in heads]
        attn = [(_nt(q16_ref[rows, cols[h]], k16[h]) * decay[h]).astype(BF16) for h in heads]
        s16 = [states[h].astype(BF16) for h in heads]
        v16 = [(u[h] - _mm(w[h], s16[h])).astype(BF16) for h in heads]
        o = [_mm(qd_ref[rows, cols[h]], s16[h]) + _mm(attn[h], v16[h]) for h in heads]
        new_states = [states[h] * cdec_ref[pl.ds(start, 1), cols[h]] + _tn(kd_ref[rows, cols[h]], v16[h])
                      for h in heads]
        o_ref[0, rows, :] = jnp.concatenate([_rms(x, nw) for x in o], axis=-1) * _silu(z_ref[0, rows, :])
        return jnp.stack(new_states)

    state_ref[...] = lax.fori_loop(0, t_blk // c_len, chunk_body, state_ref[...])


def _gdn(qkv, z, ba, conv_w, a_log, dt_bias, norm_w, batch):
    m = qkv.shape[0]
    s = m // batch
    h, dk, dv = GDN_HEADS, GDN_DK, GDN_DV
    width = h * LANE
    qkv = qkv.reshape(batch, s, 3 * width)
    tok = lambda col: pl.BlockSpec((1, GDN_TOKENS, width), lambda b, i, col=col: (b, i, col))
    cw = lambda col: pl.BlockSpec((GDN_CONV, width), lambda b, i, col=col: (0, col))
    lane_pad = jnp.zeros((LANE - 2 * h,), F32)
    alog_row = jnp.concatenate([jnp.zeros((h,), F32), a_log, lane_pad]).reshape(1, LANE)
    dtb_row = jnp.concatenate([jnp.zeros((h,), F32), dt_bias, lane_pad]).reshape(1, LANE)
    out = pl.pallas_call(
        _gdn_kernel,
        grid=(batch, s // GDN_TOKENS),
        in_specs=[tok(0), tok(1), tok(2), tok(0),
                  pl.BlockSpec((1, GDN_TOKENS, LANE), lambda b, i: (b, i, 0)),
                  cw(0), cw(1), cw(2),
                  _resident((1, LANE)), _resident((1, LANE)), _resident((1, dv))],
        out_specs=tok(0),
        out_shape=jax.ShapeDtypeStruct((batch, s, width), F32),
        scratch_shapes=[pltpu.VMEM((GDN_TOKENS + CONV_PAD, width), F32)] * 3
        + [pltpu.VMEM((GDN_TOKENS, width), F32)] * 2 + [pltpu.VMEM((GDN_TOKENS, width), BF16)] * 7
        + [pltpu.VMEM((h, dk, dv), F32)],
        compiler_params=_params("parallel", "arbitrary"),
        name="gdn",
    )(qkv, qkv, qkv, z.reshape(batch, s, width), ba.reshape(batch, s, LANE),
      conv_w, conv_w, conv_w, alog_row, dtb_row, norm_w.reshape(1, dv))
    return out.reshape(m, width)


GLA_TOKENS = 256


def _log_sigmoid(x):
    return jnp.minimum(x, 0.0) - jnp.log1p(jnp.exp(-jnp.abs(x)))


def _gla_kernel(qk_ref, v_ref, r_ref, a_ref, gw_ref, gb_ref, nw_ref, o_ref, ks_ref, bs_ref, vs_ref, state_ref):
    t_blk = qk_ref.shape[1]
    c_len = GLA_CHUNK
    hdk = GLA_HEADS * GLA_DK
    hdv = GLA_HEADS * GLA_DV

    @pl.when(pl.program_id(1) == 0)
    def _():
        state_ref[...] = jnp.zeros_like(state_ref)
        ks_ref[0:c_len, :] = jnp.zeros((c_len, hdk), F32)
        bs_ref[0:c_len, :] = jnp.zeros((c_len, hdk), F32)
        vs_ref[0:c_len, :] = jnp.zeros((c_len, hdv), F32)

    qk = qk_ref[0]
    q = qk[:, :hdk] * (GLA_DK ** -0.5)
    k = qk[:, hdk:]
    v = v_ref[0]
    log_a = _log_sigmoid(_mm(a_ref[0].astype(BF16), gw_ref[...]) + gb_ref[...]) * (1.0 / GLA_TAU)

    ti = lax.broadcasted_iota(jnp.int32, (t_blk, t_blk), 0)
    tj = lax.broadcasted_iota(jnp.int32, (t_blk, t_blk), 1)
    same = (ti // c_len) == (tj // c_len)
    cum = _mm((same & (tj <= ti)).astype(F32), log_a, precision=HIGHEST)
    tot = _mm(same.astype(F32), log_a, precision=HIGHEST)

    ks_ref[c_len:, :] = k
    bs_ref[c_len:, :] = cum
    vs_ref[c_len:, :] = v
    pos = lax.broadcasted_iota(jnp.int32, (t_blk, 1), 0) % c_len
    hd = lax.broadcasted_iota(jnp.int32, (hdk, hdv), 0) // GLA_DK
    he = lax.broadcasted_iota(jnp.int32, (hdk, hdv), 1) // GLA_DV
    spread = (hd == he).astype(BF16)
    o = jnp.zeros((t_blk, hdv), F32)
    for off in range(c_len):
        k_o = ks_ref[pl.ds(c_len - off, t_blk), :]
        b_o = bs_ref[pl.ds(c_len - off, t_blk), :]
        v_o = vs_ref[pl.ds(c_len - off, t_blk), :]
        w = jnp.exp(jnp.where(pos >= off, cum - b_o, MASK_VALUE))
        o = o + _mm((q * k_o * w).astype(BF16), spread) * v_o

    q_dec = (q * jnp.exp(cum)).astype(BF16)
    k_dec = (k * jnp.exp(tot - cum)).astype(BF16)
    c_dec = jnp.exp(tot)
    v16 = v.astype(BF16)
    states = [state_ref[h] for h in range(GLA_HEADS)]
    inter = []
    for n in range(t_blk // c_len):
        rows = slice(n * c_len, (n + 1) * c_len)
        parts = []
        for h in range(GLA_HEADS):
            dks = slice(h * GLA_DK, (h + 1) * GLA_DK)
            dvs = slice(h * GLA_DV, (h + 1) * GLA_DV)
            parts.append(_nt(q_dec[rows, dks], states[h].astype(BF16)))
            states[h] = states[h] * c_dec[n * c_len:n * c_len + 1, dks] + _tn(v16[rows, dvs], k_dec[rows, dks])
        inter.append(jnp.concatenate(parts, axis=-1))
    for h in range(GLA_HEADS):
        state_ref[h] = states[h]
    o = o + jnp.concatenate(inter, axis=0)

    r = r_ref[0]
    nw = nw_ref[...]
    for h in range(GLA_HEADS):
        dvs = slice(h * GLA_DV, (h + 1) * GLA_DV)
        o_ref[0, :, dvs] = _rms(o[:, dvs], nw) * _silu(r[:, dvs])


def _gla(qk, v, r, a_low, gate_w, gate_b, norm_w, batch):
    m = qk.shape[0]
    s = m // batch
    hdk, hdv = GLA_HEADS * GLA_DK, GLA_HEADS * GLA_DV
    gw = jnp.zeros((LANE, hdk), F32).at[:GLA_RANK].set(gate_w).astype(BF16)
    blk = lambda w: pl.BlockSpec((1, GLA_TOKENS, w), lambda b, i: (b, i, 0))
    out = pl.pallas_call(
        _gla_kernel,
        grid=(batch, s // GLA_TOKENS),
        in_specs=[blk(2 * hdk), blk(hdv), blk(hdv), blk(LANE),
                  _resident((LANE, hdk)), _resident((1, hdk)), _resident((1, GLA_DV))],
        out_specs=blk(hdv),
        out_shape=jax.ShapeDtypeStruct((batch, s, hdv), F32),
        scratch_shapes=[pltpu.VMEM((GLA_TOKENS + GLA_CHUNK, hdk), F32),
                        pltpu.VMEM((GLA_TOKENS + GLA_CHUNK, hdk), F32),
                        pltpu.VMEM((GLA_TOKENS + GLA_CHUNK, hdv), F32),
                        pltpu.VMEM((GLA_HEADS, GLA_DV, GLA_DK), F32)],
        compiler_params=_params("parallel", "arbitrary"),
        name="gla",
    )(qk.reshape(batch, s, 2 * hdk), v.reshape(batch, s, hdv), r.reshape(batch, s, hdv),
      a_low.reshape(batch, s, LANE), gw, gate_b.reshape(1, hdk), norm_w.reshape(1, GLA_DV))
    return out.reshape(m, hdv)


def _gelu_tanh(x):
    return 0.5 * x * (1.0 + jnp.tanh(math.sqrt(2.0 / math.pi) * (x + 0.044715 * x * x * x)))


def _cmp_kernel(seg_ref, wa_ref, wb_ref, pek_ref, w1k_ref, pev_ref, w1v_ref, w2_ref, k_ref, v_ref):
    seg = seg_ref[0]
    n_seg = seg.shape[0]
    first = _mm(seg, wa_ref[...])
    second = _mm(seg, wb_ref[...])
    bias_k = _mm(pek_ref[...], w1k_ref[...])[0:1]
    bias_v = _mm(pev_ref[...], w1v_ref[...])[0:1]
    bias = jnp.concatenate([bias_k, bias_v] * NSA_GROUPS, axis=-1)
    hid = _gelu_tanh(first + pltpu.roll(second, n_seg - 1, 0) + bias).astype(BF16)
    out = _mm(hid, w2_ref[...])
    kw = NSA_GROUPS * LANE
    k_ref[0] = out[:, :kw].astype(BF16)
    v_ref[0] = out[:, kw:].astype(BF16)


def _cmp_weights(pe_k, w1k, w2k, pe_v, w1v, w2v):
    g, dk, dv, hid = NSA_GROUPS, NSA_DK, NSA_DV, CMP_HIDDEN
    w1k = w1k.reshape(CMP_LEN, dk, hid)
    w1v = w1v.reshape(CMP_LEN, dv, hid)

    def half(lo):
        blk = jnp.zeros((CMP_STRIDE, g, dk + dv, g, 2, hid), F32)
        for gg in range(g):
            blk = blk.at[:, gg, :dk, gg, 0].set(w1k[lo:lo + CMP_STRIDE])
            blk = blk.at[:, gg, dk:, gg, 1].set(w1v[lo:lo + CMP_STRIDE])
        return blk.reshape(CMP_STRIDE * g * (dk + dv), g * 2 * hid).astype(BF16)

    w2 = jnp.zeros((g, 2, hid, 2 * g * LANE), F32)
    for gg in range(g):
        w2 = w2.at[gg, 0, :, gg * LANE:gg * LANE + dk].set(w2k)
        w2 = w2.at[gg, 1, :, (g + gg) * LANE:(g + gg) * LANE + dv].set(w2v)
    w2 = w2.reshape(g * 2 * hid, -1).astype(BF16)
    pad8 = lambda pe: jnp.zeros((8, pe.size), F32).at[0].set(pe.reshape(-1)).astype(BF16)
    return half(0), half(CMP_STRIDE), pad8(pe_k), w1k.reshape(-1, hid).astype(BF16), pad8(pe_v), \
        w1v.reshape(-1, hid).astype(BF16), w2


def _compress(kv_cmp, weights, batch):
    m, width = kv_cmp.shape
    s = m // batch
    n_seg = s // CMP_STRIDE
    seg = kv_cmp.reshape(batch, n_seg, CMP_STRIDE * width)
    wa, wb, pek, w1k, pev, w1v, w2 = weights
    full = lambda a: _resident(a.shape)
    kw = vw = NSA_GROUPS * LANE
    return pl.pallas_call(
        _cmp_kernel,
        grid=(batch,),
        in_specs=[pl.BlockSpec((1, n_seg, CMP_STRIDE * width), lambda b: (b, 0, 0)),
                  full(wa), full(wb), full(pek), full(w1k), full(pev), full(w1v), full(w2)],
        out_specs=[pl.BlockSpec((1, n_seg, kw), lambda b: (b, 0, 0)),
                   pl.BlockSpec((1, n_seg, vw), lambda b: (b, 0, 0))],
        out_shape=[jax.ShapeDtypeStruct((batch, n_seg, kw), BF16), jax.ShapeDtypeStruct((batch, n_seg, vw), BF16)],
        compiler_params=_params("parallel"),
        name="nsa_compress",
    )(seg, wa, wb, pek, w1k, pev, w1v, w2)


NSA_KEY_CHUNK = 512


def _online_step(carry, s, v):
    m_old, acc = carry
    m_new = jnp.maximum(m_old, jnp.max(s, axis=-1, keepdims=True))
    p = jnp.exp(s - m_new).astype(BF16)
    return m_new, jnp.exp(m_old - m_new) * acc + _mm(p, v)


def _normalised(acc):
    low = lax.broadcasted_iota(jnp.int32, acc.shape, 1) < NSA_DV
    return jnp.where(low, acc / pltpu.roll(acc, NSA_DV, 1), 0.0)


def _nsa_kernel(far_ref, q_ref, gate_ref, kc_ref, vc_ref, ks_ref, vs_ref, eb_ref, kw_ref, vw_ref, gcb_ref, bpd_ref,
                o_ref, *, n_sel, k_top):
    qb = pl.program_id(1)
    tq = Q_BLOCK
    rows = NSA_HPG * tq
    n_cmp_pad = kc_ref.shape[1]
    q0 = qb * tq
    qi = lax.broadcasted_iota(jnp.int32, (rows, 1), 0) % tq
    row_t = q0 + qi

    cmp_n = lax.broadcasted_iota(jnp.int32, (rows, n_cmp_pad), 1)
    valid_c = row_t >= cmp_n * CMP_STRIDE + (CMP_LEN - 1)
    any_c = (row_t >= CMP_LEN - 1).astype(F32)
    band_n = lax.broadcasted_iota(jnp.int32, (n_cmp_pad, LANE), 0)
    band_l = lax.broadcasted_iota(jnp.int32, (n_cmp_pad, LANE), 1)
    band_m = band_l % CMP_BAND_PAD
    band_keys = ((band_l < 2 * CMP_BAND_PAD)
                 & ((band_n - CMP_PER_Q * qb == band_m + CMP_BAND_LO) | (band_m == CMP_BAND_PAD - 1))).astype(BF16)
    ov_s = lax.broadcasted_iota(jnp.int32, (n_sel, n_cmp_pad), 0) * SEL_LEN
    ov_c = lax.broadcasted_iota(jnp.int32, (n_sel, n_cmp_pad), 1) * CMP_STRIDE
    overlap_t = ((ov_c < ov_s + SEL_LEN) & (ov_c + CMP_LEN > ov_s)).astype(BF16)

    sel_s = lax.broadcasted_iota(jnp.int32, (n_sel, tq), 0)
    blk_t = (q0 + lax.broadcasted_iota(jnp.int32, (n_sel, tq), 1)) // SEL_LEN
    forced = (sel_s == 0) | (sel_s == blk_t) | (sel_s == blk_t - 1)
    future = sel_s > blk_t

    q_all = q_ref[0]
    gates = gate_ref[0]
    hw = NSA_HEADS * LANE
    prev_base = pl.multiple_of(jnp.maximum(qb - 1, 0) * tq, tq)
    diag_base = pl.multiple_of(q0, tq)

    def add_far(s, heads):
        return jnp.concatenate([s[i * tq:(i + 1) * tq] + far_ref[h] for i, h in enumerate(heads)], axis=0)

    for g in range(NSA_GROUPS):
        heads = range(g * NSA_HPG, (g + 1) * NSA_HPG)
        q4 = jnp.concatenate([q_all[:, h * LANE:(h + 1) * LANE] for h in heads], axis=0)
        col = slice(g * LANE, (g + 1) * LANE)

        s = _nt(jnp.concatenate([q4, gcb_ref[g]], axis=1), jnp.concatenate([kc_ref[0, :, col], band_keys], axis=1))
        s = jnp.where(valid_c, s, MASK_VALUE)
        p = jnp.exp(s - jnp.max(s, axis=-1, keepdims=True))
        p = p / jnp.sum(p, axis=-1, keepdims=True) * any_c
        o_cmp = _mm(p.astype(BF16), vc_ref[0, :, col])
        p_sum = p[0:tq]
        for i in range(1, NSA_HPG):
            p_sum = p_sum + p[i * tq:(i + 1) * tq]

        p_hi, p_lo = _split(p_sum)
        p_lo2 = (p_sum - p_hi.astype(F32) - p_lo.astype(F32)).astype(BF16)
        imp = _nt(overlap_t, p_hi) + (_nt(overlap_t, p_lo) + _nt(overlap_t, p_lo2))
        imp = jnp.where(forced, jnp.inf, jnp.where(future, -jnp.inf, imp))
        chosen = jnp.zeros((n_sel, tq), F32)
        for _ in range(k_top):
            best = jnp.max(imp, axis=0, keepdims=True)
            first = jnp.min(jnp.where(imp == best, sel_s, n_sel), axis=0, keepdims=True)
            pick = sel_s == first
            chosen = jnp.where(pick, 1.0, chosen)
            imp = jnp.where(pick, -jnp.inf, imp)
        skipped = 1.0 - chosen
        skipped_far = jnp.where(sel_s >= (tq // SEL_LEN) * (qb - 1), 1.0, skipped)
        q_far = jnp.concatenate([q4, jnp.concatenate([skipped_far.T.astype(BF16)] * NSA_HPG, axis=0)], axis=1)
        q_near = jnp.concatenate([q4, jnp.concatenate([skipped.T.astype(BF16)] * NSA_HPG, axis=0)], axis=1)

        def far_logits(base):
            k_aug = jnp.concatenate([ks_ref[0, pl.ds(base, NSA_KEY_CHUNK), col],
                                     eb_ref[pl.ds(base, NSA_KEY_CHUNK), :]], axis=1)
            return add_far(_nt(q_far, k_aug), heads)

        def far_body(c, carry):
            base0 = pl.multiple_of(c * (2 * NSA_KEY_CHUNK), NSA_KEY_CHUNK)
            base1 = pl.multiple_of(base0 + NSA_KEY_CHUNK, NSA_KEY_CHUNK)
            s0, s1 = far_logits(base0), far_logits(base1)
            carry = _online_step(carry, s0, vs_ref[0, pl.ds(base0, NSA_KEY_CHUNK), col])
            return _online_step(carry, s1, vs_ref[0, pl.ds(base1, NSA_KEY_CHUNK), col])

        n_far = (jnp.maximum(qb - 1, 0) * tq + 2 * NSA_KEY_CHUNK - 1) // (2 * NSA_KEY_CHUNK)
        carry = lax.fori_loop(0, n_far, far_body,
                              (jnp.full((rows, 1), MASK_VALUE, F32), jnp.zeros((rows, LANE), F32)))

        k_aug = jnp.concatenate(
            [jnp.concatenate([ks_ref[0, pl.ds(prev_base, tq), col], ks_ref[0, pl.ds(diag_base, tq), col]], axis=0),
             jnp.concatenate([eb_ref[pl.ds(prev_base, tq), :], eb_ref[pl.ds(diag_base, tq), :]], axis=0)], axis=1)
        v_near = jnp.concatenate([vs_ref[0, pl.ds(prev_base, tq), col], vs_ref[0, pl.ds(diag_base, tq), col]], axis=0)
        key = lax.broadcasted_iota(jnp.int32, (rows, 2 * tq), 1)
        first_key = jnp.where(qb >= 1, 0, tq)
        valid = (key >= first_key) & (key <= qi + tq)
        s = jnp.where(valid, _nt(q_near, k_aug) + bpd_ref[g], MASK_VALUE)
        o_sel = _normalised(_online_step(carry, s, v_near)[1])

        n_win = WINDOW + tq
        s = _nt(q4, kw_ref[0, pl.ds(diag_base, n_win), col])
        s = jnp.concatenate([add_far(s[:, :n_win - 2 * tq], heads), s[:, n_win - 2 * tq:] + bpd_ref[g]], axis=1)
        key = lax.broadcasted_iota(jnp.int32, (rows, n_win), 1)
        valid = (key > qi) & (key <= qi + WINDOW) & (key >= WINDOW - q0)
        s = jnp.where(valid, s, MASK_VALUE)
        p = jnp.exp(s - jnp.max(s, axis=-1, keepdims=True)).astype(BF16)
        o_win = _normalised(_mm(p, vw_ref[0, pl.ds(diag_base, n_win), col]))

        for i, h in enumerate(heads):
            r = slice(i * tq, (i + 1) * tq)
            c0 = h * LANE
            o_ref[0, :, c0:c0 + LANE] = (gates[:, c0:c0 + LANE] * o_cmp[r]
                                         + gates[:, hw + c0:hw + c0 + LANE] * o_sel[r]
                                         + gates[:, 2 * hw + c0:2 * hw + c0 + LANE] * o_win[r])


def _rel_bucket(dist):
    n = jnp.maximum(dist, 0)
    max_exact = REL_BUCKETS // 2
    nf = jnp.maximum(n, 1).astype(F32)
    large = max_exact + (jnp.log(nf / max_exact) / math.log(REL_MAX_DIST / max_exact)
                         * (REL_BUCKETS - max_exact)).astype(jnp.int32)
    large = jnp.minimum(large, REL_BUCKETS - 1)
    return jnp.where(n < max_exact, n, large)


def _bias_tables(rel_bias):
    tbl = rel_bias.astype(F32).T
    i = jnp.arange(Q_BLOCK)
    dist = i[:, None] - i[None, :]
    diag = tbl[:, _rel_bucket(dist)]
    prev = tbl[:, _rel_bucket(dist + Q_BLOCK)]
    far = tbl[:, REL_BUCKETS - 1]
    m = jnp.arange(CMP_BAND_PAD)
    d_c = i[:, None] - CMP_STRIDE * (m[None, :] + CMP_BAND_LO) - (CMP_LEN - 1)
    band = tbl[:, _rel_bucket(d_c)] - far[:, None, None]
    band = jnp.where((m < CMP_BAND)[None, None, :], band, 0.0)
    band = band.at[:, :, CMP_BAND_PAD - 1].set(jnp.broadcast_to(far[:, None], (NSA_HEADS, Q_BLOCK)))
    band = band.reshape(NSA_GROUPS, NSA_HPG * Q_BLOCK, CMP_BAND_PAD)
    band_hi = band.astype(BF16)
    band_lo = (band - band_hi.astype(F32)).astype(BF16)
    band = jnp.pad(jnp.concatenate([band_hi, band_lo], axis=-1), ((0, 0), (0, 0), (0, LANE - 2 * CMP_BAND_PAD)))
    near = jnp.concatenate([prev, diag], axis=-1).reshape(NSA_GROUPS, NSA_HPG * Q_BLOCK, 2 * Q_BLOCK)
    return far, band, near


def _nsa(q, gates, k_cmp, v_cmp, k_sel, v_sel, k_win, v_win, tables, batch):
    m = q.shape[0]
    s = m // batch
    assert s % (2 * NSA_KEY_CHUNK) == 0
    n_sel = s // SEL_LEN
    k_top = min(SEL_TOPK, n_sel)
    hw = NSA_HEADS * LANE
    kvw = NSA_GROUPS * LANE
    far, band, near = tables
    n_seg = k_cmp.shape[1]
    blk_of_key = jnp.arange(s)[:, None] // SEL_LEN == jnp.arange(n_sel)[None, :]
    key_blk = jnp.where(blk_of_key, MASK_VALUE, 0.0).astype(BF16)
    front = ((0, 0), (WINDOW, 0), (0, 0))
    k_win = jnp.pad(k_win.reshape(batch, s, kvw), front)
    v_win = jnp.pad(v_win.reshape(batch, s, kvw), front)
    per_batch = lambda n: pl.BlockSpec((1, n, kvw), lambda b, i: (b, 0, 0), pipeline_mode=pl.Buffered(1))
    tok = lambda w: pl.BlockSpec((1, Q_BLOCK, w), lambda b, i: (b, i, 0))
    out = pl.pallas_call(
        functools.partial(_nsa_kernel, n_sel=n_sel, k_top=k_top),
        grid=(batch, s // Q_BLOCK),
        in_specs=[pl.BlockSpec(memory_space=pltpu.SMEM),
                  tok(hw), tok(N_BRANCH * hw),
                  per_batch(n_seg), per_batch(n_seg),
                  per_batch(s), per_batch(s), _resident(key_blk.shape), per_batch(s + WINDOW), per_batch(s + WINDOW),
                  _resident(band.shape), _resident(near.shape)],
        out_specs=tok(hw),
        out_shape=jax.ShapeDtypeStruct((batch, s, hw), F32),
        compiler_params=_params("parallel", "arbitrary"),
        name="nsa_attend",
    )(far, q.reshape(batch, s, hw), gates.reshape(batch, s, N_BRANCH * hw), k_cmp, v_cmp,
      k_sel.reshape(batch, s, kvw), v_sel.reshape(batch, s, kvw), key_blk, k_win, v_win, band, near)
    return out.reshape(m, hw)


MERGE_TOKENS = 512


def _merge_kernel(x_ref, oa_ref, ob_ref, oc_ref, gate_ref, wa_ref, wb_ref, wc_ref, wo_ref, nw_ref, o_ref):
    d = x_ref.shape[1]
    merged = (gate_ref[:, 0:d] * _mm(oa_ref[...].astype(BF16), wa_ref[...])
              + gate_ref[:, d:2 * d] * _mm(ob_ref[...].astype(BF16), wb_ref[...])
              + gate_ref[:, 2 * d:3 * d] * _mm(oc_ref[...].astype(BF16), wc_ref[...]))
    y = _mm(merged.astype(BF16), wo_ref[...])
    o_ref[...] = x_ref[...] + _rms(y, nw_ref[...])


def _merge(x, o_a, o_b, o_c, gates, w_a, w_b, w_c, w_out, norm_post):
    m, d = x.shape
    row = lambda w: pl.BlockSpec((MERGE_TOKENS, w), lambda i: (i, 0))
    full = lambda a: _resident(a.shape)
    ws = [w.astype(BF16) for w in (w_a, w_b, w_c, w_out)]
    return pl.pallas_call(
        _merge_kernel,
        grid=(m // MERGE_TOKENS,),
        in_specs=[row(d), row(o_a.shape[1]), row(o_b.shape[1]), row(o_c.shape[1]), row(N_BRANCH * d),
                  full(ws[0]), full(ws[1]), full(ws[2]), full(ws[3]), _resident((1, d))],
        out_specs=row(d),
        out_shape=jax.ShapeDtypeStruct((m, d), F32),
        compiler_params=_params("parallel"),
        name="merge_out",
    )(x, o_a, o_b, o_c, gates, *ws, norm_post.reshape(1, d))


_IN_COLUMNS = (
    ("gdn_q", GDN_HEADS * GDN_DK), ("gdn_k", GDN_HEADS * GDN_DK), ("gdn_v", GDN_HEADS * GDN_DV),
    ("gdn_z", GDN_HEADS * GDN_DV), ("gdn_b", GDN_HEADS), ("gdn_a", GDN_HEADS),
    ("nsa_q", NSA_HEADS * NSA_DK), ("nsa_kv_cmp", NSA_GROUPS * (NSA_DK + NSA_DV)),
    ("nsa_kv_sel", NSA_GROUPS * (NSA_DK + NSA_DV)), ("nsa_kv_win", NSA_GROUPS * (NSA_DK + NSA_DV)),
    ("nsa_gate", 3 * NSA_HEADS),
    ("gla_q", GLA_HEADS * GLA_DK), ("gla_k", GLA_HEADS * GLA_DK), ("gla_v", GLA_HEADS * GLA_DV),
    ("gla_r", GLA_HEADS * GLA_DV), ("gla_a", GLA_RANK), ("merge_gate", None),
)


def _split_w_in(w_in):
    out, off = {}, 0
    for name, width in _IN_COLUMNS:
        width = w_in.shape[1] - off if width is None else width
        out[name] = w_in[:, off:off + width]
        off += width
    return out


def _pad_cols(w, width):
    return jnp.pad(w, ((0, 0), (0, width - w.shape[1])))


def _kv_layout(w):
    d = w.shape[0]
    w = w.reshape(d, NSA_GROUPS, NSA_DK + NSA_DV)
    k = jnp.pad(w[:, :, :NSA_DK], ((0, 0), (0, 0), (0, LANE - NSA_DK))).reshape(d, NSA_GROUPS * LANE)
    v = jnp.pad(w[:, :, NSA_DK:], ((0, 0), (0, 0), (0, LANE - NSA_DV))).reshape(d, NSA_GROUPS * LANE)
    return k, v


def _mix_pieces(w_in):
    d = w_in.shape[0]
    c = _split_w_in(w_in)
    gdn = [
        (jnp.concatenate([c["gdn_q"], c["gdn_k"], c["gdn_v"]], axis=1), F32, None),
        (c["gdn_z"], F32, None),
        (_pad_cols(jnp.concatenate([c["gdn_b"], c["gdn_a"]], axis=1), LANE), F32, None),
    ]
    gla = [
        (jnp.concatenate([c["gla_q"], c["gla_k"]], axis=1), F32, None),
        (c["gla_v"], F32, None),
        (c["gla_r"], F32, None),
        (_pad_cols(c["gla_a"], LANE), F32, None),
    ]
    q = jnp.pad(c["nsa_q"].reshape(d, NSA_HEADS, NSA_DK), ((0, 0), (0, 0), (0, LANE - NSA_DK)))
    k_sel, v_sel = _kv_layout(c["nsa_kv_sel"])
    k_win, v_win = _kv_layout(c["nsa_kv_win"])
    gate = c["nsa_gate"].reshape(d, NSA_HEADS, N_BRANCH).transpose(0, 2, 1)
    gate = jnp.broadcast_to(gate[..., None], (d, N_BRANCH, NSA_HEADS, LANE)).reshape(d, -1)
    nsa = [
        (q.reshape(d, NSA_HEADS * LANE), BF16, NSA_DK ** -0.5),
        (c["nsa_kv_cmp"], BF16, None),
        (k_sel, BF16, None), (v_sel, BF16, "ones_hi"), (k_win, BF16, None), (v_win, BF16, "ones_hi"),
        (gate, F32, "sigmoid"),
        (c["merge_gate"], F32, "sigmoid"),
    ]
    return gdn + gla, nsa


def _layer(x, batch, tables, p):
    x = _ffn(x, p["ffn1_norm_pre"], p["ffn1_w_gate_up"], p["ffn1_w_down"], p["ffn1_norm_post"])
    rec_pieces, nsa_pieces = _mix_pieces(p["w_in"])
    qkv, z, ba, gla_qk, gla_v, gla_r, gla_a = _proj(x, p["mix_norm_pre"], rec_pieces)
    nsa_q, kv_cmp, k_sel, v_sel, k_win, v_win, nsa_gate, merge_gate = _proj(x, p["mix_norm_pre"], nsa_pieces)
    o_a = _gdn(qkv, z, ba, p["gdn_conv_w"], p["gdn_a_log"], p["gdn_dt_bias"], p["gdn_norm_w"], batch)
    o_c = _gla(gla_qk, gla_v, gla_r, gla_a, p["gla_gate_w"], p["gla_gate_b"], p["gla_norm_w"], batch)
    cmp_w = _cmp_weights(p["nsa_pe_k"], p["nsa_cmp_k_w1"], p["nsa_cmp_k_w2"],
                         p["nsa_pe_v"], p["nsa_cmp_v_w1"], p["nsa_cmp_v_w2"])
    k_cmp, v_cmp = _compress(kv_cmp, cmp_w, batch)
    o_b = _nsa(nsa_q, nsa_gate, k_cmp, v_cmp, k_sel, v_sel, k_win, v_win, tables, batch)
    w_nsa = p["w_branch_nsa"].reshape(NSA_HEADS, NSA_DV, -1)
    w_nsa = jnp.pad(w_nsa, ((0, 0), (0, LANE - NSA_DV), (0, 0))).reshape(NSA_HEADS * LANE, -1)
    x = _merge(x, o_a, o_b, o_c, merge_gate, p["w_branch_gdn"], w_nsa, p["w_branch_gla"],
               p["w_out"], p["mix_norm_post"])
    return _ffn(x, p["ffn2_norm_pre"], p["ffn2_w_gate_up"], p["ffn2_w_down"], p["ffn2_norm_post"])


_LAYER_PARAMS = (
    "ffn1_norm_pre", "ffn1_w_gate_up", "ffn1_w_down", "ffn1_norm_post", "mix_norm_pre", "w_in",
    "gdn_conv_w", "gdn_a_log", "gdn_dt_bias", "gdn_norm_w",
    "nsa_pe_k", "nsa_cmp_k_w1", "nsa_cmp_k_w2", "nsa_pe_v", "nsa_cmp_v_w1", "nsa_cmp_v_w2",
    "gla_gate_w", "gla_gate_b", "gla_norm_w",
    "w_branch_gdn", "w_branch_nsa", "w_branch_gla", "w_out", "mix_norm_post",
    "ffn2_norm_pre", "ffn2_w_gate_up", "ffn2_w_down", "ffn2_norm_post",
)


def kernel(x, rel_bias, ffn1_norm_pre, ffn1_w_gate_up, ffn1_w_down, ffn1_norm_post, mix_norm_pre, w_in, gdn_conv_w, gdn_a_log, gdn_dt_bias, gdn_norm_w, nsa_pe_k, nsa_cmp_k_w1, nsa_cmp_k_w2, nsa_pe_v, nsa_cmp_v_w1, nsa_cmp_v_w2, gla_gate_w, gla_gate_b, gla_norm_w, w_branch_gdn, w_branch_nsa, w_branch_gla, w_out, mix_norm_post, ffn2_norm_pre, ffn2_w_gate_up, ffn2_w_down, ffn2_norm_post):
    stacked = dict(zip(_LAYER_PARAMS, (
        ffn1_norm_pre, ffn1_w_gate_up, ffn1_w_down, ffn1_norm_post, mix_norm_pre, w_in,
        gdn_conv_w, gdn_a_log, gdn_dt_bias, gdn_norm_w,
        nsa_pe_k, nsa_cmp_k_w1, nsa_cmp_k_w2, nsa_pe_v, nsa_cmp_v_w1, nsa_cmp_v_w2,
        gla_gate_w, gla_gate_b, gla_norm_w,
        w_branch_gdn, w_branch_nsa, w_branch_gla, w_out, mix_norm_post,
        ffn2_norm_pre, ffn2_w_gate_up, ffn2_w_down, ffn2_norm_post)))
    batch, seq, d = x.shape
    tables = _bias_tables(rel_bias)
    h = x.reshape(batch * seq, d)
    for layer in range(ffn1_norm_pre.shape[0]):
        h = _layer(h, batch, tables, {name: value[layer] for name, value in stacked.items()})
    return h.reshape(batch, seq, d)
```

```python
import functools
import math

import jax
import jax.numpy as jnp
from jax import lax
from jax.experimental import pallas as pl
from jax.experimental.pallas import tpu as pltpu

F32 = jnp.float32
BF16 = jnp.bfloat16
HIGHEST = lax.Precision.HIGHEST

LANE = 128
VMEM_LIMIT_BYTES = 56 * 1024 * 1024

NORM_EPS = 1e-6
FFN_RES_SCALE = 0.5
REL_BUCKETS = 32
REL_MAX_DIST = 128
GDN_HEADS, GDN_DK, GDN_DV, GDN_CONV, GDN_CHUNK = 4, 128, 128, 4, 64
NSA_HEADS, NSA_GROUPS, NSA_DK, NSA_DV = 8, 2, 96, 64
NSA_HPG = NSA_HEADS // NSA_GROUPS
CMP_LEN, CMP_STRIDE, CMP_HIDDEN = 32, 16, 256
SEL_LEN, SEL_TOPK, WINDOW, Q_BLOCK = 64, 16, 512, 128
GLA_HEADS, GLA_DK, GLA_DV, GLA_RANK, GLA_TAU, GLA_CHUNK = 4, 64, 128, 16, 16, 16
N_BRANCH = 3
MASK_VALUE = -1e30
L2_EPS = 1e-6

CMP_PER_Q = Q_BLOCK // CMP_STRIDE
CMP_BAND_LO = -10
CMP_BAND = 17
CMP_BAND_PAD = 32


def _nt(a, b, **kw):
    return lax.dot_general(a, b, (((1,), (1,)), ((), ())), preferred_element_type=F32, **kw)


def _tn(a, b, **kw):
    return lax.dot_general(a, b, (((0,), (0,)), ((), ())), preferred_element_type=F32, **kw)


def _mm(a, b, **kw):
    return jnp.dot(a, b, preferred_element_type=F32, **kw)


def _rms(x, w):
    return x * lax.rsqrt(jnp.mean(x * x, axis=-1, keepdims=True) + NORM_EPS) * w


def _silu(x):
    return x * jax.nn.sigmoid(x)


def _softplus(x):
    return jnp.maximum(x, 0.0) + jnp.log1p(jnp.exp(-jnp.abs(x)))


def _params(*semantics):
    return pltpu.CompilerParams(dimension_semantics=semantics, vmem_limit_bytes=VMEM_LIMIT_BYTES)


def _resident(shape):
    return pl.BlockSpec(shape, lambda *_: (0,) * len(shape), pipeline_mode=pl.Buffered(1))


FFN_TOKENS = 512
FFN_CHUNK = 256


def _ffn_kernel(x_ref, npre_ref, wgu_ref, wd_ref, npost_ref, o_ref):
    d_ff = wd_ref.shape[0]
    x = x_ref[...]
    h = _rms(x, npre_ref[...]).astype(BF16)
    acc = jnp.zeros(x.shape, F32)
    for c in range(d_ff // FFN_CHUNK):
        lo = c * FFN_CHUNK
        g = _mm(h, wgu_ref[:, lo:lo + FFN_CHUNK])
        u = _mm(h, wgu_ref[:, d_ff + lo:d_ff + lo + FFN_CHUNK])
        a = (_silu(g) * u).astype(BF16)
        acc = acc + _mm(a, wd_ref[lo:lo + FFN_CHUNK, :])
    o_ref[...] = x + FFN_RES_SCALE * _rms(acc, npost_ref[...])


def _ffn(x, norm_pre, w_gate_up, w_down, norm_post):
    m, d = x.shape
    d_ff = w_down.shape[0]
    row = pl.BlockSpec((FFN_TOKENS, d), lambda i: (i, 0))
    return pl.pallas_call(
        _ffn_kernel,
        grid=(m // FFN_TOKENS,),
        in_specs=[row, _resident((1, d)), _resident((d, 2 * d_ff)), _resident((d_ff, d)), _resident((1, d))],
        out_specs=row,
        out_shape=jax.ShapeDtypeStruct((m, d), F32),
        compiler_params=_params("parallel"),
        name="ffn",
    )(x, norm_pre.reshape(1, d), w_gate_up.astype(BF16), w_down.astype(BF16), norm_post.reshape(1, d))


PROJ_TOKENS = 256
PROJ_CHUNK = 512


def _proj_kernel(x_ref, nw_ref, w_ref, *o_refs, segs):
    h = _rms(x_ref[...], nw_ref[...]).astype(BF16)
    for (off, width, epilogue), o_ref in zip(segs, o_refs):
        for lo in range(0, width, PROJ_CHUNK):
            hi = min(lo + PROJ_CHUNK, width)
            y = _mm(h, w_ref[:, off + lo:off + hi])
            if epilogue == "sigmoid":
                y = jax.nn.sigmoid(y)
            elif epilogue == "ones_hi":
                lane = lax.broadcasted_iota(jnp.int32, y.shape, 1) % LANE
                y = jnp.where(lane >= LANE // 2, 1.0, y)
            elif epilogue is not None:
                y = y * epilogue
            o_ref[:, lo:hi] = y.astype(o_ref.dtype)


def _proj(x, norm_w, pieces):
    m, d = x.shape
    segs, cols, off = [], [], 0
    for w, _, epilogue in pieces:
        width = w.shape[1]
        pad = (-width) % LANE
        segs.append((off, width, epilogue))
        cols.append(w)
        if pad:
            cols.append(jnp.zeros((d, pad), w.dtype))
        off += width + pad
    w_all = jnp.concatenate(cols, axis=1).astype(BF16)
    row = pl.BlockSpec((PROJ_TOKENS, d), lambda i: (i, 0))
    return pl.pallas_call(
        functools.partial(_proj_kernel, segs=tuple(segs)),
        grid=(m // PROJ_TOKENS,),
        in_specs=[row, _resident((1, d)), _resident((d, off))],
        out_specs=[pl.BlockSpec((PROJ_TOKENS, w.shape[1]), lambda i: (i, 0)) for w, _, _ in pieces],
        out_shape=[jax.ShapeDtypeStruct((m, w.shape[1]), dt) for w, dt, _ in pieces],
        compiler_params=_params("parallel"),
        name="in_proj",
    )(x, norm_w.reshape(1, d), w_all)


GDN_TOKENS = 512
CONV_PAD = 8


def _split(x):
    hi = x.astype(BF16)
    return hi, (x - hi.astype(F32)).astype(BF16)


def _gdn_kernel(q_ref, k_ref, v_ref, z_ref, ba_ref, cwq_ref, cwk_ref, cwv_ref, alog_ref, dtb_ref, nw_ref,
                o_ref, xq_ref, xk_ref, xv_ref, gc_ref, cdec_ref,
                q16_ref, k16_ref, kb16_ref, qd_ref, kd_ref, vb_ref, kbe_ref, state_ref):
    t_blk = q_ref.shape[1]
    c_len = GDN_CHUNK
    width = GDN_HEADS * LANE

    @pl.when(pl.program_id(1) == 0)
    def _():
        state_ref[...] = jnp.zeros_like(state_ref)
        for x_ref in (xq_ref, xk_ref, xv_ref):
            x_ref[0:CONV_PAD, :] = jnp.zeros((CONV_PAD, width), F32)

    def conv_silu(raw_ref, x_ref, cw_ref):
        x_ref[CONV_PAD:CONV_PAD + t_blk, :] = raw_ref[0]
        y = jnp.zeros((t_blk, width), F32)
        for tap in range(GDN_CONV):
            y = y + cw_ref[tap:tap + 1, :] * x_ref[pl.ds(CONV_PAD - (GDN_CONV - 1) + tap, t_blk), :]
        x_ref[0:CONV_PAD, :] = x_ref[t_blk:t_blk + CONV_PAD, :]
        return _silu(y)

    q = conv_silu(q_ref, xq_ref, cwq_ref)
    k = conv_silu(k_ref, xk_ref, cwk_ref)
    v = conv_silu(v_ref, xv_ref, cwv_ref)

    ba = ba_ref[0]
    g_all = -jnp.exp(alog_ref[...]) * _softplus(ba + dtb_ref[...])
    src = lax.broadcasted_iota(jnp.int32, (LANE, width), 0)
    dst_head = lax.broadcasted_iota(jnp.int32, (LANE, width), 1) // LANE
    pick_b = (src == dst_head).astype(BF16)
    pick_a = (src == dst_head + GDN_HEADS).astype(BF16)
    b_hi, b_lo = _split(ba)
    g_hi, g_lo = _split(g_all)
    beta = jax.nn.sigmoid(_mm(b_hi, pick_b) + _mm(b_lo, pick_b))
    g_hi, g_lo = _split(_mm(g_hi, pick_a) + _mm(g_lo, pick_a))

    ti = lax.broadcasted_iota(jnp.int32, (t_blk, t_blk), 0)
    tj = lax.broadcasted_iota(jnp.int32, (t_blk, t_blk), 1)
    same = (ti // c_len) == (tj // c_len)
    in_chunk_tri = (same & (tj <= ti)).astype(BF16)
    in_chunk = same.astype(BF16)
    gc_all = _mm(in_chunk_tri, g_hi) + _mm(in_chunk_tri, g_lo)
    gl_all = _mm(in_chunk, g_hi) + _mm(in_chunk, g_lo)
    gc_ref[...] = gc_all
    cdec_ref[...] = jnp.exp(gl_all)
    e_gc = jnp.exp(gc_all)
    e_rest = jnp.exp(gl_all - gc_all)
    for h in range(GDN_HEADS):
        cols = slice(h * LANE, (h + 1) * LANE)
        qh, kh = q[:, cols], k[:, cols]
        qh = qh * lax.rsqrt(jnp.sum(qh * qh, axis=-1, keepdims=True) + L2_EPS) * (GDN_DK ** -0.5)
        kh = kh * lax.rsqrt(jnp.sum(kh * kh, axis=-1, keepdims=True) + L2_EPS)
        kb = kh * beta[:, cols]
        q16_ref[:, cols] = qh.astype(BF16)
        k16_ref[:, cols] = kh.astype(BF16)
        kb16_ref[:, cols] = kb.astype(BF16)
        qd_ref[:, cols] = (qh * e_gc[:, cols]).astype(BF16)
        kd_ref[:, cols] = (kh * e_rest[:, cols]).astype(BF16)
        vb_ref[:, cols] = (v[:, cols] * beta[:, cols]).astype(BF16)
        kbe_ref[:, cols] = (kb * e_gc[:, cols]).astype(BF16)

    ci = lax.broadcasted_iota(jnp.int32, (c_len, c_len), 0)
    cj = lax.broadcasted_iota(jnp.int32, (c_len, c_len), 1)
    lower = ci >= cj
    strict = ci > cj
    eye = (ci == cj).astype(F32)
    lane0 = (lax.broadcasted_iota(jnp.int32, (c_len, LANE), 1) == 0).astype(BF16)
    nw = nw_ref[...]

    def chunk_body(c, states):
        start = pl.multiple_of(c * c_len, c_len)
        rows = pl.ds(start, c_len)
        heads = range(GDN_HEADS)
        cols = [slice(h * LANE, (h + 1) * LANE) for h in heads]
        gc = [gc_ref[rows, cols[h]] for h in heads]
        gc_split = [_split(x) for x in gc]
        gc_row = [_nt(lane0, hi) + _nt(lane0, lo) for hi, lo in gc_split]
        decay = [jnp.exp(jnp.where(lower, gc[h][:, :c_len] - gc_row[h], MASK_VALUE)) for h in heads]
        k16 = [k16_ref[rows, cols[h]] for h in heads]
        kk = [_nt(kb16_ref[rows, cols[h]], k16[h]) for h in heads]
        pw = [jnp.where(strict, -(kk[h] * decay[h]), 0.0) for h in heads]
        inv = [eye + p for p in pw]
        for _ in range(5):
            p16 = [p.astype(BF16) for p in pw]
            pw = [_mm(p, p) for p in p16]
            inv = [inv[h] + _mm(inv[h].astype(BF16), pw[h].astype(BF16)) for h in heads]
        inv16 = [x.astype(BF16) for x in inv]
        u = [_mm(inv16[h], vb_ref[rows, cols[h]]) for h in heads]
        w = [_mm(inv16[h], kbe_ref[rows, cols[h]]).astype(BF16) for h in heads]
        attn = [(_nt(q16_ref[rows, cols[h]], k16[h]) * decay[h]).astype(BF16) for h in heads]
        s16 = [states[h].astype(BF16) for h in heads]
        v16 = [(u[h] - _mm(w[h], s16[h])).astype(BF16) for h in heads]
        o = [_mm(qd_ref[rows, cols[h]], s16[h]) + _mm(attn[h], v16[h]) for h in heads]
        new_states = [states[h] * cdec_ref[pl.ds(start, 1), cols[h]] + _tn(kd_ref[rows, cols[h]], v16[h])
                      for h in heads]
        o_ref[0, rows, :] = jnp.concatenate([_rms(x, nw) for x in o], axis=-1) * _silu(z_ref[0, rows, :])
        return jnp.stack(new_states)

    state_ref[...] = lax.fori_loop(0, t_blk // c_len, chunk_body, state_ref[...])


def _gdn(qkv, z, ba, conv_w, a_log, dt_bias, norm_w, batch):
    m = qkv.shape[0]
    s = m // batch
    h, dk, dv = GDN_HEADS, GDN_DK, GDN_DV
    width = h * LANE
    qkv = qkv.reshape(batch, s, 3 * width)
    tok = lambda col: pl.BlockSpec((1, GDN_TOKENS, width), lambda b, i, col=col: (b, i, col))
    cw = lambda col: pl.BlockSpec((GDN_CONV, width), lambda b, i, col=col: (0, col))
    lane_pad = jnp.zeros((LANE - 2 * h,), F32)
    alog_row = jnp.concatenate([jnp.zeros((h,), F32), a_log, lane_pad]).reshape(1, LANE)
    dtb_row = jnp.concatenate([jnp.zeros((h,), F32), dt_bias, lane_pad]).reshape(1, LANE)
    out = pl.pallas_call(
        _gdn_kernel,
        grid=(batch, s // GDN_TOKENS),
        in_specs=[tok(0), tok(1), tok(2), tok(0),
                  pl.BlockSpec((1, GDN_TOKENS, LANE), lambda b, i: (b, i, 0)),
                  cw(0), cw(1), cw(2),
                  _resident((1, LANE)), _resident((1, LANE)), _resident((1, dv))],
        out_specs=tok(0),
        out_shape=jax.ShapeDtypeStruct((batch, s, width), F32),
        scratch_shapes=[pltpu.VMEM((GDN_TOKENS + CONV_PAD, width), F32)] * 3
        + [pltpu.VMEM((GDN_TOKENS, width), F32)] * 2 + [pltpu.VMEM((GDN_TOKENS, width), BF16)] * 7
        + [pltpu.VMEM((h, dk, dv), F32)],
        compiler_params=_params("parallel", "arbitrary"),
        name="gdn",
    )(qkv, qkv, qkv, z.reshape(batch, s, width), ba.reshape(batch, s, LANE),
      conv_w, conv_w, conv_w, alog_row, dtb_row, norm_w.reshape(1, dv))
    return out.reshape(m, width)


GLA_TOKENS = 256


def _log_sigmoid(x):
    return jnp.minimum(x, 0.0) - jnp.log1p(jnp.exp(-jnp.abs(x)))


def _gla_kernel(qk_ref, v_ref, r_ref, a_ref, gw_ref, gb_ref, nw_ref, o_ref, ks_ref, bs_ref, vs_ref, state_ref):
    t_blk = qk_ref.shape[1]
    c_len = GLA_CHUNK
    hdk = GLA_HEADS * GLA_DK
    hdv = GLA_HEADS * GLA_DV

    @pl.when(pl.program_id(1) == 0)
    def _():
        state_ref[...] = jnp.zeros_like(state_ref)
        ks_ref[0:c_len, :] = jnp.zeros((c_len, hdk), F32)
        bs_ref[0:c_len, :] = jnp.zeros((c_len, hdk), F32)
        vs_ref[0:c_len, :] = jnp.zeros((c_len, hdv), F32)

    qk = qk_ref[0]
    q = qk[:, :hdk] * (GLA_DK ** -0.5)
    k = qk[:, hdk:]
    v = v_ref[0]
    log_a = _log_sigmoid(_mm(a_ref[0].astype(BF16), gw_ref[...]) + gb_ref[...]) * (1.0 / GLA_TAU)

    ti = lax.broadcasted_iota(jnp.int32, (t_blk, t_blk), 0)
    tj = lax.broadcasted_iota(jnp.int32, (t_blk, t_blk), 1)
    same = (ti // c_len) == (tj // c_len)
    cum = _mm((same & (tj <= ti)).astype(F32), log_a, precision=HIGHEST)
    tot = _mm(same.astype(F32), log_a, precision=HIGHEST)

    ks_ref[c_len:, :] = k
    bs_ref[c_len:, :] = cum
    vs_ref[c_len:, :] = v
    pos = lax.broadcasted_iota(jnp.int32, (t_blk, 1), 0) % c_len
    hd = lax.broadcasted_iota(jnp.int32, (hdk, hdv), 0) // GLA_DK
    he = lax.broadcasted_iota(jnp.int32, (hdk, hdv), 1) // GLA_DV
    spread = (hd == he).astype(BF16)
    o = jnp.zeros((t_blk, hdv), F32)
    for off in range(c_len):
        k_o = ks_ref[pl.ds(c_len - off, t_blk), :]
        b_o = bs_ref[pl.ds(c_len - off, t_blk), :]
        v_o = vs_ref[pl.ds(c_len - off, t_blk), :]
        w = jnp.exp(jnp.where(pos >= off, cum - b_o, MASK_VALUE))
        o = o + _mm((q * k_o * w).astype(BF16), spread) * v_o

    q_dec = (q * jnp.exp(cum)).astype(BF16)
    k_dec = (k * jnp.exp(tot - cum)).astype(BF16)
    c_dec = jnp.exp(tot)
    v16 = v.astype(BF16)
    states = [state_ref[h] for h in range(GLA_HEADS)]
    inter = []
    for n in range(t_blk // c_len):
        rows = slice(n * c_len, (n + 1) * c_len)
        parts = []
        for h in range(GLA_HEADS):
            dks = slice(h * GLA_DK, (h + 1) * GLA_DK)
            dvs = slice(h * GLA_DV, (h + 1) * GLA_DV)
            parts.append(_nt(q_dec[rows, dks], states[h].astype(BF16)))
            states[h] = states[h] * c_dec[n * c_len:n * c_len + 1, dks] + _tn(v16[rows, dvs], k_dec[rows, dks])
        inter.append(jnp.concatenate(parts, axis=-1))
    for h in range(GLA_HEADS):
        state_ref[h] = states[h]
    o = o + jnp.concatenate(inter, axis=0)

    r = r_ref[0]
    nw = nw_ref[...]
    for h in range(GLA_HEADS):
        dvs = slice(h * GLA_DV, (h + 1) * GLA_DV)
        o_ref[0, :, dvs] = _rms(o[:, dvs], nw) * _silu(r[:, dvs])


def _gla(qk, v, r, a_low, gate_w, gate_b, norm_w, batch):
    m = qk.shape[0]
    s = m // batch
    hdk, hdv = GLA_HEADS * GLA_DK, GLA_HEADS * GLA_DV
    gw = jnp.zeros((LANE, hdk), F32).at[:GLA_RANK].set(gate_w).astype(BF16)
    blk = lambda w: pl.BlockSpec((1, GLA_TOKENS, w), lambda b, i: (b, i, 0))
    out = pl.pallas_call(
        _gla_kernel,
        grid=(batch, s // GLA_TOKENS),
        in_specs=[blk(2 * hdk), blk(hdv), blk(hdv), blk(LANE),
                  _resident((LANE, hdk)), _resident((1, hdk)), _resident((1, GLA_DV))],
        out_specs=blk(hdv),
        out_shape=jax.ShapeDtypeStruct((batch, s, hdv), F32),
        scratch_shapes=[pltpu.VMEM((GLA_TOKENS + GLA_CHUNK, hdk), F32),
                        pltpu.VMEM((GLA_TOKENS + GLA_CHUNK, hdk), F32),
                        pltpu.VMEM((GLA_TOKENS + GLA_CHUNK, hdv), F32),
                        pltpu.VMEM((GLA_HEADS, GLA_DV, GLA_DK), F32)],
        compiler_params=_params("parallel", "arbitrary"),
        name="gla",
    )(qk.reshape(batch, s, 2 * hdk), v.reshape(batch, s, hdv), r.reshape(batch, s, hdv),
      a_low.reshape(batch, s, LANE), gw, gate_b.reshape(1, hdk), norm_w.reshape(1, GLA_DV))
    return out.reshape(m, hdv)


def _gelu_tanh(x):
    return 0.5 * x * (1.0 + jnp.tanh(math.sqrt(2.0 / math.pi) * (x + 0.044715 * x * x * x)))


def _cmp_kernel(seg_ref, wa_ref, wb_ref, pek_ref, w1k_ref, pev_ref, w1v_ref, w2_ref, k_ref, v_ref):
    seg = seg_ref[0]
    n_seg = seg.shape[0]
    first = _mm(seg, wa_ref[...])
    second = _mm(seg, wb_ref[...])
    bias_k = _mm(pek_ref[...], w1k_ref[...])[0:1]
    bias_v = _mm(pev_ref[...], w1v_ref[...])[0:1]
    bias = jnp.concatenate([bias_k, bias_v] * NSA_GROUPS, axis=-1)
    hid = _gelu_tanh(first + pltpu.roll(second, n_seg - 1, 0) + bias).astype(BF16)
    out = _mm(hid, w2_ref[...])
    kw = NSA_GROUPS * LANE
    k_ref[0] = out[:, :kw].astype(BF16)
    v_ref[0] = out[:, kw:].astype(BF16)


def _cmp_weights(pe_k, w1k, w2k, pe_v, w1v, w2v):
    g, dk, dv, hid = NSA_GROUPS, NSA_DK, NSA_DV, CMP_HIDDEN
    w1k = w1k.reshape(CMP_LEN, dk, hid)
    w1v = w1v.reshape(CMP_LEN, dv, hid)

    def half(lo):
        blk = jnp.zeros((CMP_STRIDE, g, dk + dv, g, 2, hid), F32)
        for gg in range(g):
            blk = blk.at[:, gg, :dk, gg, 0].set(w1k[lo:lo + CMP_STRIDE])
            blk = blk.at[:, gg, dk:, gg, 1].set(w1v[lo:lo + CMP_STRIDE])
        return blk.reshape(CMP_STRIDE * g * (dk + dv), g * 2 * hid).astype(BF16)

    w2 = jnp.zeros((g, 2, hid, 2 * g * LANE), F32)
    for gg in range(g):
        w2 = w2.at[gg, 0, :, gg * LANE:gg * LANE + dk].set(w2k)
        w2 = w2.at[gg, 1, :, (g + gg) * LANE:(g + gg) * LANE + dv].set(w2v)
    w2 = w2.reshape(g * 2 * hid, -1).astype(BF16)
    pad8 = lambda pe: jnp.zeros((8, pe.size), F32).at[0].set(pe.reshape(-1)).astype(BF16)
    return half(0), half(CMP_STRIDE), pad8(pe_k), w1k.reshape(-1, hid).astype(BF16), pad8(pe_v), \
        w1v.reshape(-1, hid).astype(BF16), w2


def _compress(kv_cmp, weights, batch):
    m, width = kv_cmp.shape
    s = m // batch
    n_seg = s // CMP_STRIDE
    seg = kv_cmp.reshape(batch, n_seg, CMP_STRIDE * width)
    wa, wb, pek, w1k, pev, w1v, w2 = weights
    full = lambda a: _resident(a.shape)
    kw = vw = NSA_GROUPS * LANE
    return pl.pallas_call(
        _cmp_kernel,
        grid=(batch,),
        in_specs=[pl.BlockSpec((1, n_seg, CMP_STRIDE * width), lambda b: (b, 0, 0)),
                  full(wa), full(wb), full(pek), full(w1k), full(pev), full(w1v), full(w2)],
        out_specs=[pl.BlockSpec((1, n_seg, kw), lambda b: (b, 0, 0)),
                   pl.BlockSpec((1, n_seg, vw), lambda b: (b, 0, 0))],
        out_shape=[jax.ShapeDtypeStruct((batch, n_seg, kw), BF16), jax.ShapeDtypeStruct((batch, n_seg, vw), BF16)],
        compiler_params=_params("parallel"),
        name="nsa_compress",
    )(seg, wa, wb, pek, w1k, pev, w1v, w2)


NSA_KEY_CHUNK = 512


def _online_step(carry, s, v):
    m_old, acc = carry
    m_new = jnp.maximum(m_old, jnp.max(s, axis=-1, keepdims=True))
    p = jnp.exp(s - m_new).astype(BF16)
    return m_new, jnp.exp(m_old - m_new) * acc + _mm(p, v)


def _normalised(acc):
    low = lax.broadcasted_iota(jnp.int32, acc.shape, 1) < NSA_DV
    return jnp.where(low, acc / pltpu.roll(acc, NSA_DV, 1), 0.0)


def _nsa_kernel(far_ref, q_ref, gate_ref, kc_ref, vc_ref, ks_ref, vs_ref, eb_ref, kw_ref, vw_ref, gcb_ref, bpd_ref,
                o_ref, *, n_sel, k_top):
    qb = pl.program_id(1)
    tq = Q_BLOCK
    rows = NSA_HPG * tq
    n_cmp_pad = kc_ref.shape[1]
    q0 = qb * tq
    qi = lax.broadcasted_iota(jnp.int32, (rows, 1), 0) % tq
    row_t = q0 + qi

    cmp_n = lax.broadcasted_iota(jnp.int32, (rows, n_cmp_pad), 1)
    valid_c = row_t >= cmp_n * CMP_STRIDE + (CMP_LEN - 1)
    any_c = (row_t >= CMP_LEN - 1).astype(F32)
    band_n = lax.broadcasted_iota(jnp.int32, (n_cmp_pad, LANE), 0)
    band_l = lax.broadcasted_iota(jnp.int32, (n_cmp_pad, LANE), 1)
    band_m = band_l % CMP_BAND_PAD
    band_keys = ((band_l < 2 * CMP_BAND_PAD)
                 & ((band_n - CMP_PER_Q * qb == band_m + CMP_BAND_LO) | (band_m == CMP_BAND_PAD - 1))).astype(BF16)
    ov_s = lax.broadcasted_iota(jnp.int32, (n_sel, n_cmp_pad), 0) * SEL_LEN
    ov_c = lax.broadcasted_iota(jnp.int32, (n_sel, n_cmp_pad), 1) * CMP_STRIDE
    overlap_t = ((ov_c < ov_s + SEL_LEN) & (ov_c + CMP_LEN > ov_s)).astype(BF16)

    sel_s = lax.broadcasted_iota(jnp.int32, (n_sel, tq), 0)
    blk_t = (q0 + lax.broadcasted_iota(jnp.int32, (n_sel, tq), 1)) // SEL_LEN
    forced = (sel_s == 0) | (sel_s == blk_t) | (sel_s == blk_t - 1)
    future = sel_s > blk_t

    q_all = q_ref[0]
    gates = gate_ref[0]
    hw = NSA_HEADS * LANE
    prev_base = pl.multiple_of(jnp.maximum(qb - 1, 0) * tq, tq)
    diag_base = pl.multiple_of(q0, tq)

    def add_far(s, heads):
        return jnp.concatenate([s[i * tq:(i + 1) * tq] + far_ref[h] for i, h in enumerate(heads)], axis=0)

    groups = range(NSA_GROUPS)
    heads = [range(g * NSA_HPG, (g + 1) * NSA_HPG) for g in groups]
    cols = [slice(g * LANE, (g + 1) * LANE) for g in groups]
    q4 = [jnp.concatenate([q_all[:, h * LANE:(h + 1) * LANE] for h in heads[g]], axis=0) for g in groups]

    s = [_nt(jnp.concatenate([q4[g], gcb_ref[g]], axis=1),
             jnp.concatenate([kc_ref[0, :, cols[g]], band_keys], axis=1)) for g in groups]
    s = [jnp.where(valid_c, x, MASK_VALUE) for x in s]
    p = [jnp.exp(x - jnp.max(x, axis=-1, keepdims=True)) for x in s]
    p = [x * (any_c / jnp.sum(x, axis=-1, keepdims=True)) for x in p]
    o_cmp = [_mm(p[g].astype(BF16), vc_ref[0, :, cols[g]]) for g in groups]
    p_sum = [sum(x[i * tq:(i + 1) * tq] for i in range(1, NSA_HPG)) + x[0:tq] for x in p]

    p_hi = [x.astype(BF16) for x in p_sum]
    p_lo = [(p_sum[g] - p_hi[g].astype(F32)).astype(BF16) for g in groups]
    p_lo2 = [(p_sum[g] - p_hi[g].astype(F32) - p_lo[g].astype(F32)).astype(BF16) for g in groups]
    imp = [_nt(overlap_t, p_hi[g]) + (_nt(overlap_t, p_lo[g]) + _nt(overlap_t, p_lo2[g]))
           for g in groups]
    imp = [jnp.where(forced, jnp.inf, jnp.where(future, -jnp.inf, x)) for x in imp]
    chosen = [jnp.zeros((n_sel, tq), F32) for _ in groups]
    for _ in range(k_top):
        best = [jnp.max(x, axis=0, keepdims=True) for x in imp]
        first = [jnp.min(jnp.where(imp[g] == best[g], sel_s, n_sel), axis=0, keepdims=True) for g in groups]
        pick = [sel_s == x for x in first]
        chosen = [jnp.where(pick[g], 1.0, chosen[g]) for g in groups]
        imp = [jnp.where(pick[g], -jnp.inf, imp[g]) for g in groups]
    skipped = [1.0 - x for x in chosen]
    older = sel_s < (tq // SEL_LEN) * (qb - 1)
    skipped_far = [jnp.where(older, x, 1.0) for x in skipped]
    q_far = [jnp.concatenate([q4[g], jnp.concatenate([skipped_far[g].T.astype(BF16)] * NSA_HPG, axis=0)], axis=1)
             for g in groups]
    q_near = [jnp.concatenate([q4[g], jnp.concatenate([skipped[g].T.astype(BF16)] * NSA_HPG, axis=0)], axis=1)
              for g in groups]

    def far_logits(g, base):
        k_aug = jnp.concatenate([ks_ref[0, pl.ds(base, NSA_KEY_CHUNK), cols[g]],
                                 eb_ref[pl.ds(base, NSA_KEY_CHUNK), :]], axis=1)
        return add_far(_nt(q_far[g], k_aug), heads[g])

    def far_body(c, carry):
        base0 = pl.multiple_of(c * (2 * NSA_KEY_CHUNK), NSA_KEY_CHUNK)
        base1 = pl.multiple_of(base0 + NSA_KEY_CHUNK, NSA_KEY_CHUNK)
        s0 = [far_logits(g, base0) for g in groups]
        s1 = [far_logits(g, base1) for g in groups]
        carry = [_online_step(carry[g], s0[g], vs_ref[0, pl.ds(base0, NSA_KEY_CHUNK), cols[g]]) for g in groups]
        carry = [_online_step(carry[g], s1[g], vs_ref[0, pl.ds(base1, NSA_KEY_CHUNK), cols[g]]) for g in groups]
        return tuple(carry)

    n_far = (jnp.maximum(qb - 1, 0) * tq + 2 * NSA_KEY_CHUNK - 1) // (2 * NSA_KEY_CHUNK)
    carry = lax.fori_loop(0, n_far, far_body,
                          tuple((jnp.full((rows, 1), MASK_VALUE, F32), jnp.zeros((rows, LANE), F32)) for _ in groups))

    e_near = jnp.concatenate([eb_ref[pl.ds(prev_base, tq), :], eb_ref[pl.ds(diag_base, tq), :]], axis=0)
    k_aug = [jnp.concatenate(
        [jnp.concatenate([ks_ref[0, pl.ds(prev_base, tq), cols[g]], ks_ref[0, pl.ds(diag_base, tq), cols[g]]], axis=0),
         e_near], axis=1) for g in groups]
    v_near = [jnp.concatenate([vs_ref[0, pl.ds(prev_base, tq), cols[g]], vs_ref[0, pl.ds(diag_base, tq), cols[g]]],
                              axis=0) for g in groups]
    key = lax.broadcasted_iota(jnp.int32, (rows, 2 * tq), 1)
    first_key = jnp.where(qb >= 1, 0, tq)
    valid = (key >= first_key) & (key <= qi + tq)
    s = [jnp.where(valid, _nt(q_near[g], k_aug[g]) + bpd_ref[g], MASK_VALUE) for g in groups]
    o_sel = [_normalised(_online_step(carry[g], s[g], v_near[g])[1]) for g in groups]

    n_win = WINDOW + tq
    s = [_nt(q4[g], kw_ref[0, pl.ds(diag_base, n_win), cols[g]]) for g in groups]
    s = [jnp.concatenate([add_far(s[g][:, :n_win - 2 * tq], heads[g]), s[g][:, n_win - 2 * tq:] + bpd_ref[g]], axis=1)
         for g in groups]
    key = lax.broadcasted_iota(jnp.int32, (rows, n_win), 1)
    valid = (key > qi) & (key <= qi + WINDOW) & (key >= WINDOW - q0)
    s = [jnp.where(valid, x, MASK_VALUE) for x in s]
    p = [jnp.exp(x - jnp.max(x, axis=-1, keepdims=True)).astype(BF16) for x in s]
    o_win = [_normalised(_mm(p[g], vw_ref[0, pl.ds(diag_base, n_win), cols[g]])) for g in groups]

    outs = []
    for g in groups:
        for i, h in enumerate(heads[g]):
            r = slice(i * tq, (i + 1) * tq)
            c0 = h * LANE
            outs.append(gates[:, c0:c0 + LANE] * o_cmp[g][r]
                        + gates[:, hw + c0:hw + c0 + LANE] * o_sel[g][r]
                        + gates[:, 2 * hw + c0:2 * hw + c0 + LANE] * o_win[g][r])
    o_ref[0] = jnp.concatenate(outs, axis=1)


def _rel_bucket(dist):
    n = jnp.maximum(dist, 0)
    max_exact = REL_BUCKETS // 2
    nf = jnp.maximum(n, 1).astype(F32)
    large = max_exact + (jnp.log(nf / max_exact) / math.log(REL_MAX_DIST / max_exact)
                         * (REL_BUCKETS - max_exact)).astype(jnp.int32)
    large = jnp.minimum(large, REL_BUCKETS - 1)
    return jnp.where(n < max_exact, n, large)


def _bias_tables(rel_bias):
    tbl = rel_bias.astype(F32).T
    i = jnp.arange(Q_BLOCK)
    dist = i[:, None] - i[None, :]
    diag = tbl[:, _rel_bucket(dist)]
    prev = tbl[:, _rel_bucket(dist + Q_BLOCK)]
    far = tbl[:, REL_BUCKETS - 1]
    m = jnp.arange(CMP_BAND_PAD)
    d_c = i[:, None] - CMP_STRIDE * (m[None, :] + CMP_BAND_LO) - (CMP_LEN - 1)
    band = tbl[:, _rel_bucket(d_c)] - far[:, None, None]
    band = jnp.where((m < CMP_BAND)[None, None, :], band, 0.0)
    band = band.at[:, :, CMP_BAND_PAD - 1].set(jnp.broadcast_to(far[:, None], (NSA_HEADS, Q_BLOCK)))
    band = band.reshape(NSA_GROUPS, NSA_HPG * Q_BLOCK, CMP_BAND_PAD)
    band_hi = band.astype(BF16)
    band_lo = (band - band_hi.astype(F32)).astype(BF16)
    band = jnp.pad(jnp.concatenate([band_hi, band_lo], axis=-1), ((0, 0), (0, 0), (0, LANE - 2 * CMP_BAND_PAD)))
    near = jnp.concatenate([prev, diag], axis=-1).reshape(NSA_GROUPS, NSA_HPG * Q_BLOCK, 2 * Q_BLOCK)
    return far, band, near


def _nsa(q, gates, k_cmp, v_cmp, k_sel, v_sel, k_win, v_win, tables, batch):
    m = q.shape[0]
    s = m // batch
    assert s % (2 * NSA_KEY_CHUNK) == 0
    n_sel = s // SEL_LEN
    k_top = min(SEL_TOPK, n_sel)
    hw = NSA_HEADS * LANE
    kvw = NSA_GROUPS * LANE
    far, band, near = tables
    n_seg = k_cmp.shape[1]
    blk_of_key = jnp.arange(s)[:, None] // SEL_LEN == jnp.arange(n_sel)[None, :]
    key_blk = jnp.where(blk_of_key, MASK_VALUE, 0.0).astype(BF16)
    front = ((0, 0), (WINDOW, 0), (0, 0))
    k_win = jnp.pad(k_win.reshape(batch, s, kvw), front)
    v_win = jnp.pad(v_win.reshape(batch, s, kvw), front)
    per_batch = lambda n: pl.BlockSpec((1, n, kvw), lambda b, i: (b, 0, 0), pipeline_mode=pl.Buffered(1))
    tok = lambda w: pl.BlockSpec((1, Q_BLOCK, w), lambda b, i: (b, i, 0))
    out = pl.pallas_call(
        functools.partial(_nsa_kernel, n_sel=n_sel, k_top=k_top),
        grid=(batch, s // Q_BLOCK),
        in_specs=[pl.BlockSpec(memory_space=pltpu.SMEM),
                  tok(hw), tok(N_BRANCH * hw),
                  per_batch(n_seg), per_batch(n_seg),
                  per_batch(s), per_batch(s), _resident(key_blk.shape), per_batch(s + WINDOW), per_batch(s + WINDOW),
                  _resident(band.shape), _resident(near.shape)],
        out_specs=tok(hw),
        out_shape=jax.ShapeDtypeStruct((batch, s, hw), F32),
        compiler_params=_params("parallel", "arbitrary"),
        name="nsa_attend",
    )(far, q.reshape(batch, s, hw), gates.reshape(batch, s, N_BRANCH * hw), k_cmp, v_cmp,
      k_sel.reshape(batch, s, kvw), v_sel.reshape(batch, s, kvw), key_blk, k_win, v_win, band, near)
    return out.reshape(m, hw)


MERGE_TOKENS = 512


def _merge_kernel(x_ref, oa_ref, ob_ref, oc_ref, gate_ref, wa_ref, wb_ref, wc_ref, wo_ref, nw_ref, o_ref):
    d = x_ref.shape[1]
    merged = (gate_ref[:, 0:d] * _mm(oa_ref[...].astype(BF16), wa_ref[...])
              + gate_ref[:, d:2 * d] * _mm(ob_ref[...].astype(BF16), wb_ref[...])
              + gate_ref[:, 2 * d:3 * d] * _mm(oc_ref[...].astype(BF16), wc_ref[...]))
    y = _mm(merged.astype(BF16), wo_ref[...])
    o_ref[...] = x_ref[...] + _rms(y, nw_ref[...])


def _merge(x, o_a, o_b, o_c, gates, w_a, w_b, w_c, w_out, norm_post):
    m, d = x.shape
    row = lambda w: pl.BlockSpec((MERGE_TOKENS, w), lambda i: (i, 0))
    full = lambda a: _resident(a.shape)
    ws = [w.astype(BF16) for w in (w_a, w_b, w_c, w_out)]
    return pl.pallas_call(
        _merge_kernel,
        grid=(m // MERGE_TOKENS,),
        in_specs=[row(d), row(o_a.shape[1]), row(o_b.shape[1]), row(o_c.shape[1]), row(N_BRANCH * d),
                  full(ws[0]), full(ws[1]), full(ws[2]), full(ws[3]), _resident((1, d))],
        out_specs=row(d),
        out_shape=jax.ShapeDtypeStruct((m, d), F32),
        compiler_params=_params("parallel"),
        name="merge_out",
    )(x, o_a, o_b, o_c, gates, *ws, norm_post.reshape(1, d))


_IN_COLUMNS = (
    ("gdn_q", GDN_HEADS * GDN_DK), ("gdn_k", GDN_HEADS * GDN_DK), ("gdn_v", GDN_HEADS * GDN_DV),
    ("gdn_z", GDN_HEADS * GDN_DV), ("gdn_b", GDN_HEADS), ("gdn_a", GDN_HEADS),
    ("nsa_q", NSA_HEADS * NSA_DK), ("nsa_kv_cmp", NSA_GROUPS * (NSA_DK + NSA_DV)),
    ("nsa_kv_sel", NSA_GROUPS * (NSA_DK + NSA_DV)), ("nsa_kv_win", NSA_GROUPS * (NSA_DK + NSA_DV)),
    ("nsa_gate", 3 * NSA_HEADS),
    ("gla_q", GLA_HEADS * GLA_DK), ("gla_k", GLA_HEADS * GLA_DK), ("gla_v", GLA_HEADS * GLA_DV),
    ("gla_r", GLA_HEADS * GLA_DV), ("gla_a", GLA_RANK), ("merge_gate", None),
)


def _split_w_in(w_in):
    out, off = {}, 0
    for name, width in _IN_COLUMNS:
        width = w_in.shape[1] - off if width is None else width
        out[name] = w_in[:, off:off + width]
        off += width
    return out


def _pad_cols(w, width):
    return jnp.pad(w, ((0, 0), (0, width - w.shape[1])))


def _kv_layout(w):
    d = w.shape[0]
    w = w.reshape(d, NSA_GROUPS, NSA_DK + NSA_DV)
    k = jnp.pad(w[:, :, :NSA_DK], ((0, 0), (0, 0), (0, LANE - NSA_DK))).reshape(d, NSA_GROUPS * LANE)
    v = jnp.pad(w[:, :, NSA_DK:], ((0, 0), (0, 0), (0, LANE - NSA_DV))).reshape(d, NSA_GROUPS * LANE)
    return k, v


def _mix_pieces(w_in):
    d = w_in.shape[0]
    c = _split_w_in(w_in)
    gdn = [
        (jnp.concatenate([c["gdn_q"], c["gdn_k"], c["gdn_v"]], axis=1), F32, None),
        (c["gdn_z"], F32, None),
        (_pad_cols(jnp.concatenate([c["gdn_b"], c["gdn_a"]], axis=1), LANE), F32, None),
    ]
    gla = [
        (jnp.concatenate([c["gla_q"], c["gla_k"]], axis=1), F32, None),
        (c["gla_v"], F32, None),
        (c["gla_r"], F32, None),
        (_pad_cols(c["gla_a"], LANE), F32, None),
    ]
    q = jnp.pad(c["nsa_q"].reshape(d, NSA_HEADS, NSA_DK), ((0, 0), (0, 0), (0, LANE - NSA_DK)))
    k_sel, v_sel = _kv_layout(c["nsa_kv_sel"])
    k_win, v_win = _kv_layout(c["nsa_kv_win"])
    gate = c["nsa_gate"].reshape(d, NSA_HEADS, N_BRANCH).transpose(0, 2, 1)
    gate = jnp.broadcast_to(gate[..., None], (d, N_BRANCH, NSA_HEADS, LANE)).reshape(d, -1)
    nsa = [
        (q.reshape(d, NSA_HEADS * LANE), BF16, NSA_DK ** -0.5),
        (c["nsa_kv_cmp"], BF16, None),
        (k_sel, BF16, None), (v_sel, BF16, "ones_hi"), (k_win, BF16, None), (v_win, BF16, "ones_hi"),
        (gate, F32, "sigmoid"),
        (c["merge_gate"], F32, "sigmoid"),
    ]
    return gdn + gla, nsa


def _layer(x, batch, tables, p):
    x = _ffn(x, p["ffn1_norm_pre"], p["ffn1_w_gate_up"], p["ffn1_w_down"], p["ffn1_norm_post"])
    rec_pieces, nsa_pieces = _mix_pieces(p["w_in"])
    qkv, z, ba, gla_qk, gla_v, gla_r, gla_a = _proj(x, p["mix_norm_pre"], rec_pieces)
    nsa_q, kv_cmp, k_sel, v_sel, k_win, v_win, nsa_gate, merge_gate = _proj(x, p["mix_norm_pre"], nsa_pieces)
    o_a = _gdn(qkv, z, ba, p["gdn_conv_w"], p["gdn_a_log"], p["gdn_dt_bias"], p["gdn_norm_w"], batch)
    o_c = _gla(gla_qk, gla_v, gla_r, gla_a, p["gla_gate_w"], p["gla_gate_b"], p["gla_norm_w"], batch)
    cmp_w = _cmp_weights(p["nsa_pe_k"], p["nsa_cmp_k_w1"], p["nsa_cmp_k_w2"],
                         p["nsa_pe_v"], p["nsa_cmp_v_w1"], p["nsa_cmp_v_w2"])
    k_cmp, v_cmp = _compress(kv_cmp, cmp_w, batch)
    o_b = _nsa(nsa_q, nsa_gate, k_cmp, v_cmp, k_sel, v_sel, k_win, v_win, tables, batch)
    w_nsa = p["w_branch_nsa"].reshape(NSA_HEADS, NSA_DV, -1)
    w_nsa = jnp.pad(w_nsa, ((0, 0), (0, LANE - NSA_DV), (0, 0))).reshape(NSA_HEADS * LANE, -1)
    x = _merge(x, o_a, o_b, o_c, merge_gate, p["w_branch_gdn"], w_nsa, p["w_branch_gla"],
               p["w_out"], p["mix_norm_post"])
    return _ffn(x, p["ffn2_norm_pre"], p["ffn2_w_gate_up"], p["ffn2_w_down"], p["ffn2_norm_post"])


_LAYER_PARAMS = (
    "ffn1_norm_pre", "ffn1_w_gate_up", "ffn1_w_down", "ffn1_norm_post", "mix_norm_pre", "w_in",
    "gdn_conv_w", "gdn_a_log", "gdn_dt_bias", "gdn_norm_w",
    "nsa_pe_k", "nsa_cmp_k_w1", "nsa_cmp_k_w2", "nsa_pe_v", "nsa_cmp_v_w1", "nsa_cmp_v_w2",
    "gla_gate_w", "gla_gate_b", "gla_norm_w",
    "w_branch_gdn", "w_branch_nsa", "w_branch_gla", "w_out", "mix_norm_post",
    "ffn2_norm_pre", "ffn2_w_gate_up", "ffn2_w_down", "ffn2_norm_post",
)


def kernel(x, rel_bias, ffn1_norm_pre, ffn1_w_gate_up, ffn1_w_down, ffn1_norm_post, mix_norm_pre, w_in, gdn_conv_w, gdn_a_log, gdn_dt_bias, gdn_norm_w, nsa_pe_k, nsa_cmp_k_w1, nsa_cmp_k_w2, nsa_pe_v, nsa_cmp_v_w1, nsa_cmp_v_w2, gla_gate_w, gla_gate_b, gla_norm_w, w_branch_gdn, w_branch_nsa, w_branch_gla, w_out, mix_norm_post, ffn2_norm_pre, ffn2_w_gate_up, ffn2_w_down, ffn2_norm_post):
    stacked = dict(zip(_LAYER_PARAMS, (
        ffn1_norm_pre, ffn1_w_gate_up, ffn1_w_down, ffn1_norm_post, mix_norm_pre, w_in,
        gdn_conv_w, gdn_a_log, gdn_dt_bias, gdn_norm_w,
        nsa_pe_k, nsa_cmp_k_w1, nsa_cmp_k_w2, nsa_pe_v, nsa_cmp_v_w1, nsa_cmp_v_w2,
        gla_gate_w, gla_gate_b, gla_norm_w,
        w_branch_gdn, w_branch_nsa, w_branch_gla, w_out, mix_norm_post,
        ffn2_norm_pre, ffn2_w_gate_up, ffn2_w_down, ffn2_norm_post)))
    batch, seq, d = x.shape
    tables = _bias_tables(rel_bias)
    h = x.reshape(batch * seq, d)
    for layer in range(ffn1_norm_pre.shape[0]):
        h = _layer(h, batch, tables, {name: value[layer] for name, value in stacked.items()})
    return h.reshape(batch, seq, d)
```

```python
import functools
import math

import jax
import jax.numpy as jnp
import numpy as np
from jax import lax
from jax.experimental import pallas as pl
from jax.experimental.pallas import tpu as pltpu

F32 = jnp.float32
BF16 = jnp.bfloat16
HIGHEST = lax.Precision.HIGHEST

LANE = 128
VMEM_LIMIT_BYTES = 56 * 1024 * 1024

NORM_EPS = 1e-6
FFN_RES_SCALE = 0.5
REL_BUCKETS = 32
REL_MAX_DIST = 128
GDN_HEADS, GDN_DK, GDN_DV, GDN_CONV, GDN_CHUNK = 4, 128, 128, 4, 64
NSA_HEADS, NSA_GROUPS, NSA_DK, NSA_DV = 8, 2, 96, 64
NSA_HPG = NSA_HEADS // NSA_GROUPS
CMP_LEN, CMP_STRIDE, CMP_HIDDEN = 32, 16, 256
SEL_LEN, SEL_TOPK, WINDOW, Q_BLOCK = 64, 16, 512, 128
GLA_HEADS, GLA_DK, GLA_DV, GLA_RANK, GLA_TAU, GLA_CHUNK = 4, 64, 128, 16, 16, 16
N_BRANCH = 3
MASK_VALUE = -1e30
L2_EPS = 1e-6

CMP_PER_Q = Q_BLOCK // CMP_STRIDE
CMP_BAND_LO = -10
CMP_BAND = 17
CMP_BAND_PAD = 32


def _nt(a, b, **kw):
    return lax.dot_general(a, b, (((1,), (1,)), ((), ())), preferred_element_type=F32, **kw)


def _tn(a, b, **kw):
    return lax.dot_general(a, b, (((0,), (0,)), ((), ())), preferred_element_type=F32, **kw)


def _mm(a, b, **kw):
    return jnp.dot(a, b, preferred_element_type=F32, **kw)


def _rms(x, w):
    return x * lax.rsqrt(jnp.mean(x * x, axis=-1, keepdims=True) + NORM_EPS) * w


def _silu(x):
    return x * jax.nn.sigmoid(x)


def _softplus(x):
    return jnp.maximum(x, 0.0) + jnp.log1p(jnp.exp(-jnp.abs(x)))


def _params(*semantics):
    return pltpu.CompilerParams(dimension_semantics=semantics, vmem_limit_bytes=VMEM_LIMIT_BYTES)


def _resident(shape):
    return pl.BlockSpec(shape, lambda *_: (0,) * len(shape), pipeline_mode=pl.Buffered(1))


FFN_TOKENS = 512
FFN_CHUNK = 256


def _ffn_kernel(x_ref, npre_ref, wgu_ref, wd_ref, npost_ref, o_ref):
    d_ff = wd_ref.shape[0]
    x = x_ref[...]
    h = _rms(x, npre_ref[...]).astype(BF16)
    acc = jnp.zeros(x.shape, F32)
    for c in range(d_ff // FFN_CHUNK):
        lo = c * FFN_CHUNK
        g = _mm(h, wgu_ref[:, lo:lo + FFN_CHUNK])
        u = _mm(h, wgu_ref[:, d_ff + lo:d_ff + lo + FFN_CHUNK])
        a = (_silu(g) * u).astype(BF16)
        acc = acc + _mm(a, wd_ref[lo:lo + FFN_CHUNK, :])
    o_ref[...] = x + FFN_RES_SCALE * _rms(acc, npost_ref[...])


def _ffn(x, norm_pre, w_gate_up, w_down, norm_post):
    m, d = x.shape
    d_ff = w_down.shape[0]
    row = pl.BlockSpec((FFN_TOKENS, d), lambda i: (i, 0))
    return pl.pallas_call(
        _ffn_kernel,
        grid=(m // FFN_TOKENS,),
        in_specs=[row, _resident((1, d)), _resident((d, 2 * d_ff)), _resident((d_ff, d)), _resident((1, d))],
        out_specs=row,
        out_shape=jax.ShapeDtypeStruct((m, d), F32),
        compiler_params=_params("parallel"),
        name="ffn",
    )(x, norm_pre.reshape(1, d), w_gate_up.astype(BF16), w_down.astype(BF16), norm_post.reshape(1, d))


PROJ_TOKENS = 256
PROJ_CHUNK = 512


def _proj_kernel(x_ref, nw_ref, w_ref, *o_refs, segs):
    h = _rms(x_ref[...], nw_ref[...]).astype(BF16)
    for (off, width, epilogue), o_ref in zip(segs, o_refs):
        for lo in range(0, width, PROJ_CHUNK):
            hi = min(lo + PROJ_CHUNK, width)
            y = _mm(h, w_ref[:, off + lo:off + hi])
            if epilogue == "sigmoid":
                y = jax.nn.sigmoid(y)
            elif epilogue == "ones_hi":
                lane = lax.broadcasted_iota(jnp.int32, y.shape, 1) % LANE
                y = jnp.where(lane >= LANE // 2, 1.0, y)
            elif epilogue is not None:
                y = y * epilogue
            o_ref[:, lo:hi] = y.astype(o_ref.dtype)


def _proj(x, norm_w, pieces):
    m, d = x.shape
    segs, cols, off = [], [], 0
    for w, _, epilogue in pieces:
        width = w.shape[1]
        pad = (-width) % LANE
        segs.append((off, width, epilogue))
        cols.append(w)
        if pad:
            cols.append(jnp.zeros((d, pad), w.dtype))
        off += width + pad
    w_all = jnp.concatenate(cols, axis=1).astype(BF16)
    row = pl.BlockSpec((PROJ_TOKENS, d), lambda i: (i, 0))
    return pl.pallas_call(
        functools.partial(_proj_kernel, segs=tuple(segs)),
        grid=(m // PROJ_TOKENS,),
        in_specs=[row, _resident((1, d)), _resident((d, off))],
        out_specs=[pl.BlockSpec((PROJ_TOKENS, w.shape[1]), lambda i: (i, 0)) for w, _, _ in pieces],
        out_shape=[jax.ShapeDtypeStruct((m, w.shape[1]), dt) for w, dt, _ in pieces],
        compiler_params=_params("parallel"),
        name="in_proj",
    )(x, norm_w.reshape(1, d), w_all)


GDN_TOKENS = 512
CONV_PAD = 8


def _split(x):
    hi = x.astype(BF16)
    return hi, (x - hi.astype(F32)).astype(BF16)


def _gdn_kernel(q_ref, k_ref, v_ref, z_ref, ba_ref, cwq_ref, cwk_ref, cwv_ref, alog_ref, dtb_ref, nw_ref,
                o_ref, xq_ref, xk_ref, xv_ref, gc_ref, cdec_ref,
                q16_ref, k16_ref, kb16_ref, qd_ref, kd_ref, vb_ref, kbe_ref, state_ref):
    t_blk = q_ref.shape[1]
    c_len = GDN_CHUNK
    width = GDN_HEADS * LANE

    @pl.when(pl.program_id(1) == 0)
    def _():
        state_ref[...] = jnp.zeros_like(state_ref)
        for x_ref in (xq_ref, xk_ref, xv_ref):
            x_ref[0:CONV_PAD, :] = jnp.zeros((CONV_PAD, width), F32)

    def conv_silu(raw_ref, x_ref, cw_ref):
        x_ref[CONV_PAD:CONV_PAD + t_blk, :] = raw_ref[0]
        y = jnp.zeros((t_blk, width), F32)
        for tap in range(GDN_CONV):
            y = y + cw_ref[tap:tap + 1, :] * x_ref[pl.ds(CONV_PAD - (GDN_CONV - 1) + tap, t_blk), :]
        x_ref[0:CONV_PAD, :] = x_ref[t_blk:t_blk + CONV_PAD, :]
        return _silu(y)

    q = conv_silu(q_ref, xq_ref, cwq_ref)
    k = conv_silu(k_ref, xk_ref, cwk_ref)
    v = conv_silu(v_ref, xv_ref, cwv_ref)

    ba = ba_ref[0]
    g_all = -jnp.exp(alog_ref[...]) * _softplus(ba + dtb_ref[...])
    src = lax.broadcasted_iota(jnp.int32, (LANE, width), 0)
    dst_head = lax.broadcasted_iota(jnp.int32, (LANE, width), 1) // LANE
    pick_b = (src == dst_head).astype(BF16)
    pick_a = (src == dst_head + GDN_HEADS).astype(BF16)
    b_hi, b_lo = _split(ba)
    g_hi, g_lo = _split(g_all)
    beta = jax.nn.sigmoid(_mm(b_hi, pick_b) + _mm(b_lo, pick_b))
    g_hi, g_lo = _split(_mm(g_hi, pick_a) + _mm(g_lo, pick_a))

    ti = lax.broadcasted_iota(jnp.int32, (t_blk, t_blk), 0)
    tj = lax.broadcasted_iota(jnp.int32, (t_blk, t_blk), 1)
    same = (ti // c_len) == (tj // c_len)
    in_chunk_tri = (same & (tj <= ti)).astype(BF16)
    in_chunk = same.astype(BF16)
    gc_all = _mm(in_chunk_tri, g_hi) + _mm(in_chunk_tri, g_lo)
    gl_all = _mm(in_chunk, g_hi) + _mm(in_chunk, g_lo)
    gc_ref[...] = gc_all
    cdec_ref[...] = jnp.exp(gl_all)
    e_gc = jnp.exp(gc_all)
    e_rest = jnp.exp(gl_all - gc_all)
    for h in range(GDN_HEADS):
        cols = slice(h * LANE, (h + 1) * LANE)
        qh, kh = q[:, cols], k[:, cols]
        qh = qh * lax.rsqrt(jnp.sum(qh * qh, axis=-1, keepdims=True) + L2_EPS) * (GDN_DK ** -0.5)
        kh = kh * lax.rsqrt(jnp.sum(kh * kh, axis=-1, keepdims=True) + L2_EPS)
        kb = kh * beta[:, cols]
        q16_ref[:, cols] = qh.astype(BF16)
        k16_ref[:, cols] = kh.astype(BF16)
        kb16_ref[:, cols] = kb.astype(BF16)
        qd_ref[:, cols] = (qh * e_gc[:, cols]).astype(BF16)
        kd_ref[:, cols] = (kh * e_rest[:, cols]).astype(BF16)
        vb_ref[:, cols] = (v[:, cols] * beta[:, cols]).astype(BF16)
        kbe_ref[:, cols] = (kb * e_gc[:, cols]).astype(BF16)

    ci = lax.broadcasted_iota(jnp.int32, (c_len, c_len), 0)
    cj = lax.broadcasted_iota(jnp.int32, (c_len, c_len), 1)
    lower = ci >= cj
    strict = ci > cj
    eye = (ci == cj).astype(F32)
    lane0 = (lax.broadcasted_iota(jnp.int32, (c_len, LANE), 1) == 0).astype(BF16)
    nw = nw_ref[...]

    def chunk_body(c, states):
        start = pl.multiple_of(c * c_len, c_len)
        rows = pl.ds(start, c_len)
        heads = range(GDN_HEADS)
        cols = [slice(h * LANE, (h + 1) * LANE) for h in heads]
        gc = [gc_ref[rows, cols[h]] for h in heads]
        gc_split = [_split(x) for x in gc]
        gc_row = [_nt(lane0, hi) + _nt(lane0, lo) for hi, lo in gc_split]
        decay = [jnp.exp(jnp.where(lower, gc[h][:, :c_len] - gc_row[h], MASK_VALUE)) for h in heads]
        k16 = [k16_ref[rows, cols[h]] for h in heads]
        kk = [_nt(kb16_ref[rows, cols[h]], k16[h]) for h in heads]
        pw = [jnp.where(strict, -(kk[h] * decay[h]), 0.0) for h in heads]
        inv = [eye + p for p in pw]
        for _ in range(5):
            p16 = [p.astype(BF16) for p in pw]
            pw = [_mm(p, p) for p in p16]
            inv = [inv[h] + _mm(inv[h].astype(BF16), pw[h].astype(BF16)) for h in heads]
        inv16 = [x.astype(BF16) for x in inv]
        u = [_mm(inv16[h], vb_ref[rows, cols[h]]) for h in heads]
        w = [_mm(inv16[h], kbe_ref[rows, cols[h]]).astype(BF16) for h in heads]
        attn = [(_nt(q16_ref[rows, cols[h]], k16[h]) * decay[h]).astype(BF16) for h in heads]
        s16 = [states[h].astype(BF16) for h in heads]
        v16 = [(u[h] - _mm(w[h], s16[h])).astype(BF16) for h in heads]
        o = [_mm(qd_ref[rows, cols[h]], s16[h]) + _mm(attn[h], v16[h]) for h in heads]
        new_states = [states[h] * cdec_ref[pl.ds(start, 1), cols[h]] + _tn(kd_ref[rows, cols[h]], v16[h])
                      for h in heads]
        o_ref[0, rows, :] = jnp.concatenate([_rms(x, nw) for x in o], axis=-1) * _silu(z_ref[0, rows, :])
        return jnp.stack(new_states)

    state_ref[...] = lax.fori_loop(0, t_blk // c_len, chunk_body, state_ref[...])


def _gdn(qkv, z, ba, conv_w, a_log, dt_bias, norm_w, batch):
    m = qkv.shape[0]
    s = m // batch
    h, dk, dv = GDN_HEADS, GDN_DK, GDN_DV
    width = h * LANE
    qkv = qkv.reshape(batch, s, 3 * width)
    tok = lambda col: pl.BlockSpec((1, GDN_TOKENS, width), lambda b, i, col=col: (b, i, col))
    cw = lambda col: pl.BlockSpec((GDN_CONV, width), lambda b, i, col=col: (0, col))
    lane_pad = jnp.zeros((LANE - 2 * h,), F32)
    alog_row = jnp.concatenate([jnp.zeros((h,), F32), a_log, lane_pad]).reshape(1, LANE)
    dtb_row = jnp.concatenate([jnp.zeros((h,), F32), dt_bias, lane_pad]).reshape(1, LANE)
    out = pl.pallas_call(
        _gdn_kernel,
        grid=(batch, s // GDN_TOKENS),
        in_specs=[tok(0), tok(1), tok(2), tok(0),
                  pl.BlockSpec((1, GDN_TOKENS, LANE), lambda b, i: (b, i, 0)),
                  cw(0), cw(1), cw(2),
                  _resident((1, LANE)), _resident((1, LANE)), _resident((1, dv))],
        out_specs=tok(0),
        out_shape=jax.ShapeDtypeStruct((batch, s, width), F32),
        scratch_shapes=[pltpu.VMEM((GDN_TOKENS + CONV_PAD, width), F32)] * 3
        + [pltpu.VMEM((GDN_TOKENS, width), F32)] * 2 + [pltpu.VMEM((GDN_TOKENS, width), BF16)] * 7
        + [pltpu.VMEM((h, dk, dv), F32)],
        compiler_params=_params("parallel", "arbitrary"),
        name="gdn",
    )(qkv, qkv, qkv, z.reshape(batch, s, width), ba.reshape(batch, s, LANE),
      conv_w, conv_w, conv_w, alog_row, dtb_row, norm_w.reshape(1, dv))
    return out.reshape(m, width)


GLA_TOKENS = 256


def _log_sigmoid(x):
    return jnp.minimum(x, 0.0) - jnp.log1p(jnp.exp(-jnp.abs(x)))


def _gla_kernel(qk_ref, v_ref, r_ref, a_ref, gw_ref, gb_ref, nw_ref, o_ref, ks_ref, bs_ref, vs_ref, state_ref):
    t_blk = qk_ref.shape[1]
    c_len = GLA_CHUNK
    hdk = GLA_HEADS * GLA_DK
    hdv = GLA_HEADS * GLA_DV

    @pl.when(pl.program_id(1) == 0)
    def _():
        state_ref[...] = jnp.zeros_like(state_ref)
        ks_ref[0:c_len, :] = jnp.zeros((c_len, hdk), F32)
        bs_ref[0:c_len, :] = jnp.zeros((c_len, hdk), F32)
        vs_ref[0:c_len, :] = jnp.zeros((c_len, hdv), F32)

    qk = qk_ref[0]
    q = qk[:, :hdk] * (GLA_DK ** -0.5)
    k = qk[:, hdk:]
    v = v_ref[0]
    log_a = _log_sigmoid(_mm(a_ref[0].astype(BF16), gw_ref[...]) + gb_ref[...]) * (1.0 / GLA_TAU)

    ti = lax.broadcasted_iota(jnp.int32, (t_blk, t_blk), 0)
    tj = lax.broadcasted_iota(jnp.int32, (t_blk, t_blk), 1)
    same = (ti // c_len) == (tj // c_len)
    cum = _mm((same & (tj <= ti)).astype(F32), log_a, precision=HIGHEST)
    tot = _mm(same.astype(F32), log_a, precision=HIGHEST)

    ks_ref[c_len:, :] = k
    bs_ref[c_len:, :] = cum
    vs_ref[c_len:, :] = v
    pos = lax.broadcasted_iota(jnp.int32, (t_blk, 1), 0) % c_len
    hd = lax.broadcasted_iota(jnp.int32, (hdk, hdv), 0) // GLA_DK
    he = lax.broadcasted_iota(jnp.int32, (hdk, hdv), 1) // GLA_DV
    spread = (hd == he).astype(BF16)
    o = jnp.zeros((t_blk, hdv), F32)
    for off in range(c_len):
        k_o = ks_ref[pl.ds(c_len - off, t_blk), :]
        b_o = bs_ref[pl.ds(c_len - off, t_blk), :]
        v_o = vs_ref[pl.ds(c_len - off, t_blk), :]
        w = jnp.exp(jnp.where(pos >= off, cum - b_o, MASK_VALUE))
        o = o + _mm((q * k_o * w).astype(BF16), spread) * v_o

    q_dec = (q * jnp.exp(cum)).astype(BF16)
    k_dec = (k * jnp.exp(tot - cum)).astype(BF16)
    c_dec = jnp.exp(tot)
    v16 = v.astype(BF16)
    states = [state_ref[h] for h in range(GLA_HEADS)]
    inter = []
    for n in range(t_blk // c_len):
        rows = slice(n * c_len, (n + 1) * c_len)
        parts = []
        for h in range(GLA_HEADS):
            dks = slice(h * GLA_DK, (h + 1) * GLA_DK)
            dvs = slice(h * GLA_DV, (h + 1) * GLA_DV)
            parts.append(_nt(q_dec[rows, dks], states[h].astype(BF16)))
            states[h] = states[h] * c_dec[n * c_len:n * c_len + 1, dks] + _tn(v16[rows, dvs], k_dec[rows, dks])
        inter.append(jnp.concatenate(parts, axis=-1))
    for h in range(GLA_HEADS):
        state_ref[h] = states[h]
    o = o + jnp.concatenate(inter, axis=0)

    r = r_ref[0]
    nw = nw_ref[...]
    for h in range(GLA_HEADS):
        dvs = slice(h * GLA_DV, (h + 1) * GLA_DV)
        o_ref[0, :, dvs] = _rms(o[:, dvs], nw) * _silu(r[:, dvs])


def _gla(qk, v, r, a_low, gate_w, gate_b, norm_w, batch):
    m = qk.shape[0]
    s = m // batch
    hdk, hdv = GLA_HEADS * GLA_DK, GLA_HEADS * GLA_DV
    gw = jnp.pad(gate_w, ((0, LANE - GLA_RANK), (0, 0))).astype(BF16)
    blk = lambda w: pl.BlockSpec((1, GLA_TOKENS, w), lambda b, i: (b, i, 0))
    out = pl.pallas_call(
        _gla_kernel,
        grid=(batch, s // GLA_TOKENS),
        in_specs=[blk(2 * hdk), blk(hdv), blk(hdv), blk(LANE),
                  _resident((LANE, hdk)), _resident((1, hdk)), _resident((1, GLA_DV))],
        out_specs=blk(hdv),
        out_shape=jax.ShapeDtypeStruct((batch, s, hdv), F32),
        scratch_shapes=[pltpu.VMEM((GLA_TOKENS + GLA_CHUNK, hdk), F32),
                        pltpu.VMEM((GLA_TOKENS + GLA_CHUNK, hdk), F32),
                        pltpu.VMEM((GLA_TOKENS + GLA_CHUNK, hdv), F32),
                        pltpu.VMEM((GLA_HEADS, GLA_DV, GLA_DK), F32)],
        compiler_params=_params("parallel", "arbitrary"),
        name="gla",
    )(qk.reshape(batch, s, 2 * hdk), v.reshape(batch, s, hdv), r.reshape(batch, s, hdv),
      a_low.reshape(batch, s, LANE), gw, gate_b.reshape(1, hdk), norm_w.reshape(1, GLA_DV))
    return out.reshape(m, hdv)


def _gelu_tanh(x):
    return 0.5 * x * (1.0 + jnp.tanh(math.sqrt(2.0 / math.pi) * (x + 0.044715 * x * x * x)))


def _cmp_kernel(seg_ref, wa_ref, wb_ref, pek_ref, w1k_ref, pev_ref, w1v_ref, w2_ref, k_ref, v_ref):
    seg = seg_ref[0]
    n_seg = seg.shape[0]
    first = _mm(seg, wa_ref[...])
    second = _mm(seg, wb_ref[...])
    bias_k = _mm(pek_ref[...], w1k_ref[...])[0:1]
    bias_v = _mm(pev_ref[...], w1v_ref[...])[0:1]
    bias = jnp.concatenate([bias_k, bias_v] * NSA_GROUPS, axis=-1)
    hid = _gelu_tanh(first + pltpu.roll(second, n_seg - 1, 0) + bias).astype(BF16)
    out = _mm(hid, w2_ref[...])
    kw = NSA_GROUPS * LANE
    k_ref[0] = out[:, :kw].astype(BF16)
    v_ref[0] = out[:, kw:].astype(BF16)


def _cmp_weights(pe_k, w1k, w2k, pe_v, w1v, w2v):
    g, dk, dv, hid = NSA_GROUPS, NSA_DK, NSA_DV, CMP_HIDDEN
    w1k = w1k.reshape(CMP_LEN, dk, hid)
    w1v = w1v.reshape(CMP_LEN, dv, hid)

    same_group = np.eye(g, dtype=np.float32)

    def half(lo):
        top = jnp.concatenate([w1k[lo:lo + CMP_STRIDE], jnp.zeros((CMP_STRIDE, dk, hid), F32)], axis=-1)
        bot = jnp.concatenate([jnp.zeros((CMP_STRIDE, dv, hid), F32), w1v[lo:lo + CMP_STRIDE]], axis=-1)
        per_group = jnp.concatenate([top, bot], axis=1)
        blk = per_group[:, None, :, None, :] * same_group[None, :, None, :, None]
        return blk.reshape(CMP_STRIDE * g * (dk + dv), g * 2 * hid).astype(BF16)

    padded = jnp.stack([jnp.pad(w2k, ((0, 0), (0, LANE - dk))), jnp.pad(w2v, ((0, 0), (0, LANE - dv)))])
    place = np.zeros((g, 2, 2 * g), np.float32)
    for gg in range(g):
        place[gg, 0, gg] = 1.0
        place[gg, 1, g + gg] = 1.0
    w2 = (padded[None, :, :, None, :] * place[:, :, None, :, None]).reshape(g * 2 * hid, 2 * g * LANE).astype(BF16)
    pad8 = lambda pe: jnp.pad(pe.reshape(1, -1), ((0, 7), (0, 0))).astype(BF16)
    return half(0), half(CMP_STRIDE), pad8(pe_k), w1k.reshape(-1, hid).astype(BF16), pad8(pe_v), \
        w1v.reshape(-1, hid).astype(BF16), w2


def _compress(kv_cmp, weights, batch):
    m, width = kv_cmp.shape
    s = m // batch
    n_seg = s // CMP_STRIDE
    seg = kv_cmp.reshape(batch, n_seg, CMP_STRIDE * width)
    wa, wb, pek, w1k, pev, w1v, w2 = weights
    full = lambda a: _resident(a.shape)
    kw = vw = NSA_GROUPS * LANE
    return pl.pallas_call(
        _cmp_kernel,
        grid=(batch,),
        in_specs=[pl.BlockSpec((1, n_seg, CMP_STRIDE * width), lambda b: (b, 0, 0)),
                  full(wa), full(wb), full(pek), full(w1k), full(pev), full(w1v), full(w2)],
        out_specs=[pl.BlockSpec((1, n_seg, kw), lambda b: (b, 0, 0)),
                   pl.BlockSpec((1, n_seg, vw), lambda b: (b, 0, 0))],
        out_shape=[jax.ShapeDtypeStruct((batch, n_seg, kw), BF16), jax.ShapeDtypeStruct((batch, n_seg, vw), BF16)],
        compiler_params=_params("parallel"),
        name="nsa_compress",
    )(seg, wa, wb, pek, w1k, pev, w1v, w2)


NSA_KEY_CHUNK = 512


def _online_step(carry, s, v):
    m_old, acc = carry
    m_new = jnp.maximum(m_old, jnp.max(s, axis=-1, keepdims=True))
    p = jnp.exp(s - m_new).astype(BF16)
    return m_new, jnp.exp(m_old - m_new) * acc + _mm(p, v)


def _normalised(acc):
    low = lax.broadcasted_iota(jnp.int32, acc.shape, 1) < NSA_DV
    return jnp.where(low, acc / pltpu.roll(acc, NSA_DV, 1), 0.0)


def _nsa_kernel(far_ref, q_ref, gate_ref, kc_ref, vc_ref, ks_ref, vs_ref, eb_ref, kw_ref, vw_ref, gcb_ref, bpd_ref,
                o_ref, *, n_sel, k_top):
    qb = pl.program_id(1)
    tq = Q_BLOCK
    rows = NSA_HPG * tq
    n_cmp_pad = kc_ref.shape[1]
    q0 = qb * tq
    qi = lax.broadcasted_iota(jnp.int32, (rows, 1), 0) % tq
    row_t = q0 + qi

    cmp_n = lax.broadcasted_iota(jnp.int32, (rows, n_cmp_pad), 1)
    valid_c = row_t >= cmp_n * CMP_STRIDE + (CMP_LEN - 1)
    any_c = (row_t >= CMP_LEN - 1).astype(F32)
    band_n = lax.broadcasted_iota(jnp.int32, (n_cmp_pad, LANE), 0)
    band_l = lax.broadcasted_iota(jnp.int32, (n_cmp_pad, LANE), 1)
    band_m = band_l % CMP_BAND_PAD
    band_keys = ((band_l < 2 * CMP_BAND_PAD)
                 & ((band_n - CMP_PER_Q * qb == band_m + CMP_BAND_LO) | (band_m == CMP_BAND_PAD - 1))).astype(BF16)
    ov_s = lax.broadcasted_iota(jnp.int32, (n_sel, n_cmp_pad), 0) * SEL_LEN
    ov_c = lax.broadcasted_iota(jnp.int32, (n_sel, n_cmp_pad), 1) * CMP_STRIDE
    overlap_t = ((ov_c < ov_s + SEL_LEN) & (ov_c + CMP_LEN > ov_s)).astype(BF16)

    sel_s = lax.broadcasted_iota(jnp.int32, (n_sel, tq), 0)
    blk_t = (q0 + lax.broadcasted_iota(jnp.int32, (n_sel, tq), 1)) // SEL_LEN
    forced = (sel_s == 0) | (sel_s == blk_t) | (sel_s == blk_t - 1)
    future = sel_s > blk_t

    q_all = q_ref[0]
    gates = gate_ref[0]
    prev_base = pl.multiple_of(jnp.maximum(qb - 1, 0) * tq, tq)
    diag_base = pl.multiple_of(q0, tq)

    def add_far(s, heads):
        return jnp.concatenate([s[i * tq:(i + 1) * tq] + far_ref[h] for i, h in enumerate(heads)], axis=0)

    groups = range(NSA_GROUPS)
    heads = [range(g * NSA_HPG, (g + 1) * NSA_HPG) for g in groups]
    cols = [slice(g * LANE, (g + 1) * LANE) for g in groups]
    q4 = [jnp.concatenate([q_all[:, h * LANE:(h + 1) * LANE] for h in heads[g]], axis=0) for g in groups]

    s = [_nt(jnp.concatenate([q4[g], gcb_ref[g]], axis=1),
             jnp.concatenate([kc_ref[0, :, cols[g]], band_keys], axis=1)) for g in groups]
    s = [jnp.where(valid_c, x, MASK_VALUE) for x in s]
    p = [jnp.exp(x - jnp.max(x, axis=-1, keepdims=True)) for x in s]
    p = [x * (any_c / jnp.sum(x, axis=-1, keepdims=True)) for x in p]
    o_cmp = [_mm(p[g].astype(BF16), vc_ref[0, :, cols[g]]) for g in groups]
    p_sum = [sum(x[i * tq:(i + 1) * tq] for i in range(1, NSA_HPG)) + x[0:tq] for x in p]

    p_hi = [x.astype(BF16) for x in p_sum]
    p_lo = [(p_sum[g] - p_hi[g].astype(F32)).astype(BF16) for g in groups]
    p_lo2 = [(p_sum[g] - p_hi[g].astype(F32) - p_lo[g].astype(F32)).astype(BF16) for g in groups]
    imp = [_nt(overlap_t, p_hi[g]) + (_nt(overlap_t, p_lo[g]) + _nt(overlap_t, p_lo2[g]))
           for g in groups]
    imp = [jnp.where(forced, jnp.inf, jnp.where(future, -jnp.inf, x)) for x in imp]
    chosen = [jnp.zeros((n_sel, tq), F32) for _ in groups]
    for _ in range(k_top):
        best = [jnp.max(x, axis=0, keepdims=True) for x in imp]
        first = [jnp.min(jnp.where(imp[g] == best[g], sel_s, n_sel), axis=0, keepdims=True) for g in groups]
        pick = [sel_s == x for x in first]
        chosen = [jnp.where(pick[g], 1.0, chosen[g]) for g in groups]
        imp = [jnp.where(pick[g], -jnp.inf, imp[g]) for g in groups]
    skipped = [1.0 - x for x in chosen]
    older = sel_s < (tq // SEL_LEN) * (qb - 1)
    skipped_far = [jnp.where(older, x, 1.0) for x in skipped]
    q_far = [jnp.concatenate([q4[g], jnp.concatenate([skipped_far[g].T.astype(BF16)] * NSA_HPG, axis=0)], axis=1)
             for g in groups]
    q_near = [jnp.concatenate([q4[g], jnp.concatenate([skipped[g].T.astype(BF16)] * NSA_HPG, axis=0)], axis=1)
              for g in groups]

    def far_logits(g, base):
        k_aug = jnp.concatenate([ks_ref[0, pl.ds(base, NSA_KEY_CHUNK), cols[g]],
                                 eb_ref[pl.ds(base, NSA_KEY_CHUNK), :]], axis=1)
        return add_far(_nt(q_far[g], k_aug), heads[g])

    def far_body(c, carry):
        base0 = pl.multiple_of(c * (2 * NSA_KEY_CHUNK), NSA_KEY_CHUNK)
        base1 = pl.multiple_of(base0 + NSA_KEY_CHUNK, NSA_KEY_CHUNK)
        s0 = [far_logits(g, base0) for g in groups]
        s1 = [far_logits(g, base1) for g in groups]
        carry = [_online_step(carry[g], s0[g], vs_ref[0, pl.ds(base0, NSA_KEY_CHUNK), cols[g]]) for g in groups]
        carry = [_online_step(carry[g], s1[g], vs_ref[0, pl.ds(base1, NSA_KEY_CHUNK), cols[g]]) for g in groups]
        return tuple(carry)

    n_far = (jnp.maximum(qb - 1, 0) * tq + 2 * NSA_KEY_CHUNK - 1) // (2 * NSA_KEY_CHUNK)
    carry = lax.fori_loop(0, n_far, far_body,
                          tuple((jnp.full((rows, 1), MASK_VALUE, F32), jnp.zeros((rows, LANE), F32)) for _ in groups))

    e_near = jnp.concatenate([eb_ref[pl.ds(prev_base, tq), :], eb_ref[pl.ds(diag_base, tq), :]], axis=0)
    k_aug = [jnp.concatenate(
        [jnp.concatenate([ks_ref[0, pl.ds(prev_base, tq), cols[g]], ks_ref[0, pl.ds(diag_base, tq), cols[g]]], axis=0),
         e_near], axis=1) for g in groups]
    v_near = [jnp.concatenate([vs_ref[0, pl.ds(prev_base, tq), cols[g]], vs_ref[0, pl.ds(diag_base, tq), cols[g]]],
                              axis=0) for g in groups]
    key = lax.broadcasted_iota(jnp.int32, (rows, 2 * tq), 1)
    first_key = jnp.where(qb >= 1, 0, tq)
    valid = (key >= first_key) & (key <= qi + tq)
    s = [jnp.where(valid, _nt(q_near[g], k_aug[g]) + bpd_ref[g], MASK_VALUE) for g in groups]
    o_sel = [_normalised(_online_step(carry[g], s[g], v_near[g])[1]) for g in groups]

    n_win = WINDOW + tq
    s = [_nt(q4[g], kw_ref[0, pl.ds(diag_base, n_win), cols[g]]) for g in groups]
    s = [jnp.concatenate([add_far(s[g][:, :n_win - 2 * tq], heads[g]), s[g][:, n_win - 2 * tq:] + bpd_ref[g]], axis=1)
         for g in groups]
    key = lax.broadcasted_iota(jnp.int32, (rows, n_win), 1)
    valid = (key > qi) & (key <= qi + WINDOW) & (key >= WINDOW - q0)
    s = [jnp.where(valid, x, MASK_VALUE) for x in s]
    p = [jnp.exp(x - jnp.max(x, axis=-1, keepdims=True)).astype(BF16) for x in s]
    o_win = [_normalised(_mm(p[g], vw_ref[0, pl.ds(diag_base, n_win), cols[g]])) for g in groups]

    outs = []
    for g in groups:
        for i, h in enumerate(heads[g]):
            r = slice(i * tq, (i + 1) * tq)
            c0 = h * N_BRANCH
            outs.append(gates[:, c0:c0 + 1] * o_cmp[g][r]
                        + gates[:, c0 + 1:c0 + 2] * o_sel[g][r]
                        + gates[:, c0 + 2:c0 + 3] * o_win[g][r])
    o_ref[0] = jnp.concatenate(outs, axis=1)


def _rel_bucket(dist):
    n = np.maximum(dist, 0)
    max_exact = REL_BUCKETS // 2
    nf = np.maximum(n, 1).astype(np.float32)
    large = max_exact + (np.log(nf / np.float32(max_exact)) / np.float32(math.log(REL_MAX_DIST / max_exact))
                         * np.float32(REL_BUCKETS - max_exact)).astype(np.int32)
    large = np.minimum(large, REL_BUCKETS - 1)
    return np.where(n < max_exact, n, large)


def _bias_tables(rel_bias):
    tbl = rel_bias.astype(F32).T

    def lookup(dist):
        onehot = np.eye(REL_BUCKETS, dtype=np.float32)[_rel_bucket(dist)]
        return jnp.einsum("hb,ijb->hij", tbl, onehot, precision=HIGHEST)

    i = np.arange(Q_BLOCK)
    dist = i[:, None] - i[None, :]
    diag = lookup(dist)
    prev = lookup(dist + Q_BLOCK)
    far = tbl[:, REL_BUCKETS - 1]
    m = np.arange(CMP_BAND_PAD)
    d_c = i[:, None] - CMP_STRIDE * (m[None, :] + CMP_BAND_LO) - (CMP_LEN - 1)
    band = lookup(d_c) - far[:, None, None]
    last = (m == CMP_BAND_PAD - 1)[None, None, :]
    band = jnp.where(last, far[:, None, None], jnp.where((m < CMP_BAND)[None, None, :], band, 0.0))
    band = band.reshape(NSA_GROUPS, NSA_HPG * Q_BLOCK, CMP_BAND_PAD)
    band_hi = band.astype(BF16)
    band_lo = (band - band_hi.astype(F32)).astype(BF16)
    band = jnp.pad(jnp.concatenate([band_hi, band_lo], axis=-1), ((0, 0), (0, 0), (0, LANE - 2 * CMP_BAND_PAD)))
    near = jnp.concatenate([prev, diag], axis=-1).reshape(NSA_GROUPS, NSA_HPG * Q_BLOCK, 2 * Q_BLOCK)
    return far, band, near


def _nsa(q, gates, k_cmp, v_cmp, k_sel, v_sel, k_win, v_win, tables, batch):
    m = q.shape[0]
    s = m // batch
    assert s % (2 * NSA_KEY_CHUNK) == 0
    n_sel = s // SEL_LEN
    k_top = min(SEL_TOPK, n_sel)
    hw = NSA_HEADS * LANE
    kvw = NSA_GROUPS * LANE
    far, band, near = tables
    n_seg = k_cmp.shape[1]
    blk_of_key = jnp.arange(s)[:, None] // SEL_LEN == jnp.arange(n_sel)[None, :]
    key_blk = jnp.where(blk_of_key, MASK_VALUE, 0.0).astype(BF16)
    front = ((0, 0), (WINDOW, 0), (0, 0))
    k_win = jnp.pad(k_win.reshape(batch, s, kvw), front)
    v_win = jnp.pad(v_win.reshape(batch, s, kvw), front)
    per_batch = lambda n: pl.BlockSpec((1, n, kvw), lambda b, i: (b, 0, 0), pipeline_mode=pl.Buffered(1))
    tok = lambda w: pl.BlockSpec((1, Q_BLOCK, w), lambda b, i: (b, i, 0))
    out = pl.pallas_call(
        functools.partial(_nsa_kernel, n_sel=n_sel, k_top=k_top),
        grid=(batch, s // Q_BLOCK),
        in_specs=[pl.BlockSpec(memory_space=pltpu.SMEM),
                  tok(hw), tok(LANE),
                  per_batch(n_seg), per_batch(n_seg),
                  per_batch(s), per_batch(s), _resident(key_blk.shape), per_batch(s + WINDOW), per_batch(s + WINDOW),
                  _resident(band.shape), _resident(near.shape)],
        out_specs=tok(hw),
        out_shape=jax.ShapeDtypeStruct((batch, s, hw), F32),
        compiler_params=_params("parallel", "arbitrary"),
        name="nsa_attend",
    )(far, q.reshape(batch, s, hw), gates.reshape(batch, s, LANE), k_cmp, v_cmp,
      k_sel.reshape(batch, s, kvw), v_sel.reshape(batch, s, kvw), key_blk, k_win, v_win, band, near)
    return out.reshape(m, hw)


MERGE_TOKENS = 512


def _merge_kernel(x_ref, oa_ref, ob_ref, oc_ref, gate_ref, wa_ref, wb_ref, wc_ref, wo_ref, nw_ref, o_ref):
    d = x_ref.shape[1]
    merged = (gate_ref[:, 0:d] * _mm(oa_ref[...].astype(BF16), wa_ref[...])
              + gate_ref[:, d:2 * d] * _mm(ob_ref[...].astype(BF16), wb_ref[...])
              + gate_ref[:, 2 * d:3 * d] * _mm(oc_ref[...].astype(BF16), wc_ref[...]))
    y = _mm(merged.astype(BF16), wo_ref[...])
    o_ref[...] = x_ref[...] + _rms(y, nw_ref[...])


def _merge(x, o_a, o_b, o_c, gates, w_a, w_b, w_c, w_out, norm_post):
    m, d = x.shape
    row = lambda w: pl.BlockSpec((MERGE_TOKENS, w), lambda i: (i, 0))
    full = lambda a: _resident(a.shape)
    ws = [w.astype(BF16) for w in (w_a, w_b, w_c, w_out)]
    return pl.pallas_call(
        _merge_kernel,
        grid=(m // MERGE_TOKENS,),
        in_specs=[row(d), row(o_a.shape[1]), row(o_b.shape[1]), row(o_c.shape[1]), row(N_BRANCH * d),
                  full(ws[0]), full(ws[1]), full(ws[2]), full(ws[3]), _resident((1, d))],
        out_specs=row(d),
        out_shape=jax.ShapeDtypeStruct((m, d), F32),
        compiler_params=_params("parallel"),
        name="merge_out",
    )(x, o_a, o_b, o_c, gates, *ws, norm_post.reshape(1, d))


_IN_COLUMNS = (
    ("gdn_q", GDN_HEADS * GDN_DK), ("gdn_k", GDN_HEADS * GDN_DK), ("gdn_v", GDN_HEADS * GDN_DV),
    ("gdn_z", GDN_HEADS * GDN_DV), ("gdn_b", GDN_HEADS), ("gdn_a", GDN_HEADS),
    ("nsa_q", NSA_HEADS * NSA_DK), ("nsa_kv_cmp", NSA_GROUPS * (NSA_DK + NSA_DV)),
    ("nsa_kv_sel", NSA_GROUPS * (NSA_DK + NSA_DV)), ("nsa_kv_win", NSA_GROUPS * (NSA_DK + NSA_DV)),
    ("nsa_gate", 3 * NSA_HEADS),
    ("gla_q", GLA_HEADS * GLA_DK), ("gla_k", GLA_HEADS * GLA_DK), ("gla_v", GLA_HEADS * GLA_DV),
    ("gla_r", GLA_HEADS * GLA_DV), ("gla_a", GLA_RANK), ("merge_gate", None),
)


def _split_w_in(w_in):
    out, off = {}, 0
    for name, width in _IN_COLUMNS:
        width = w_in.shape[1] - off if width is None else width
        out[name] = w_in[:, off:off + width]
        off += width
    return out


def _pad_cols(w, width):
    return jnp.pad(w, ((0, 0), (0, width - w.shape[1])))


def _kv_layout(w):
    d = w.shape[0]
    w = w.reshape(d, NSA_GROUPS, NSA_DK + NSA_DV)
    k = jnp.pad(w[:, :, :NSA_DK], ((0, 0), (0, 0), (0, LANE - NSA_DK))).reshape(d, NSA_GROUPS * LANE)
    v = jnp.pad(w[:, :, NSA_DK:], ((0, 0), (0, 0), (0, LANE - NSA_DV))).reshape(d, NSA_GROUPS * LANE)
    return k, v


def _mix_pieces(w_in):
    d = w_in.shape[0]
    c = _split_w_in(w_in)
    gdn = [
        (jnp.concatenate([c["gdn_q"], c["gdn_k"], c["gdn_v"]], axis=1), F32, None),
        (c["gdn_z"], F32, None),
        (_pad_cols(jnp.concatenate([c["gdn_b"], c["gdn_a"]], axis=1), LANE), F32, None),
    ]
    gla = [
        (jnp.concatenate([c["gla_q"], c["gla_k"]], axis=1), F32, None),
        (c["gla_v"], F32, None),
        (c["gla_r"], F32, None),
        (_pad_cols(c["gla_a"], LANE), F32, None),
    ]
    q = jnp.pad(c["nsa_q"].reshape(d, NSA_HEADS, NSA_DK), ((0, 0), (0, 0), (0, LANE - NSA_DK)))
    k_sel, v_sel = _kv_layout(c["nsa_kv_sel"])
    k_win, v_win = _kv_layout(c["nsa_kv_win"])
    gate = _pad_cols(c["nsa_gate"], LANE)
    nsa = [
        (q.reshape(d, NSA_HEADS * LANE), BF16, NSA_DK ** -0.5),
        (c["nsa_kv_cmp"], BF16, None),
        (k_sel, BF16, None), (v_sel, BF16, "ones_hi"), (k_win, BF16, None), (v_win, BF16, "ones_hi"),
        (gate, F32, "sigmoid"),
        (c["merge_gate"], F32, "sigmoid"),
    ]
    return gdn + gla, nsa


def _layer(x, batch, tables, p):
    x = _ffn(x, p["ffn1_norm_pre"], p["ffn1_w_gate_up"], p["ffn1_w_down"], p["ffn1_norm_post"])
    rec_pieces, nsa_pieces = _mix_pieces(p["w_in"])
    qkv, z, ba, gla_qk, gla_v, gla_r, gla_a = _proj(x, p["mix_norm_pre"], rec_pieces)
    nsa_q, kv_cmp, k_sel, v_sel, k_win, v_win, nsa_gate, merge_gate = _proj(x, p["mix_norm_pre"], nsa_pieces)
    o_a = _gdn(qkv, z, ba, p["gdn_conv_w"], p["gdn_a_log"], p["gdn_dt_bias"], p["gdn_norm_w"], batch)
    o_c = _gla(gla_qk, gla_v, gla_r, gla_a, p["gla_gate_w"], p["gla_gate_b"], p["gla_norm_w"], batch)
    cmp_w = _cmp_weights(p["nsa_pe_k"], p["nsa_cmp_k_w1"], p["nsa_cmp_k_w2"],
                         p["nsa_pe_v"], p["nsa_cmp_v_w1"], p["nsa_cmp_v_w2"])
    k_cmp, v_cmp = _compress(kv_cmp, cmp_w, batch)
    o_b = _nsa(nsa_q, nsa_gate, k_cmp, v_cmp, k_sel, v_sel, k_win, v_win, tables, batch)
    w_nsa = p["w_branch_nsa"].reshape(NSA_HEADS, NSA_DV, -1)
    w_nsa = jnp.pad(w_nsa, ((0, 0), (0, LANE - NSA_DV), (0, 0))).reshape(NSA_HEADS * LANE, -1)
    x = _merge(x, o_a, o_b, o_c, merge_gate, p["w_branch_gdn"], w_nsa, p["w_branch_gla"],
               p["w_out"], p["mix_norm_post"])
    return _ffn(x, p["ffn2_norm_pre"], p["ffn2_w_gate_up"], p["ffn2_w_down"], p["ffn2_norm_post"])


_LAYER_PARAMS = (
    "ffn1_norm_pre", "ffn1_w_gate_up", "ffn1_w_down", "ffn1_norm_post", "mix_norm_pre", "w_in",
    "gdn_conv_w", "gdn_a_log", "gdn_dt_bias", "gdn_norm_w",
    "nsa_pe_k", "nsa_cmp_k_w1", "nsa_cmp_k_w2", "nsa_pe_v", "nsa_cmp_v_w1", "nsa_cmp_v_w2",
    "gla_gate_w", "gla_gate_b", "gla_norm_w",
    "w_branch_gdn", "w_branch_nsa", "w_branch_gla", "w_out", "mix_norm_post",
    "ffn2_norm_pre", "ffn2_w_gate_up", "ffn2_w_down", "ffn2_norm_post",
)


def kernel(x, rel_bias, ffn1_norm_pre, ffn1_w_gate_up, ffn1_w_down, ffn1_norm_post, mix_norm_pre, w_in, gdn_conv_w, gdn_a_log, gdn_dt_bias, gdn_norm_w, nsa_pe_k, nsa_cmp_k_w1, nsa_cmp_k_w2, nsa_pe_v, nsa_cmp_v_w1, nsa_cmp_v_w2, gla_gate_w, gla_gate_b, gla_norm_w, w_branch_gdn, w_branch_nsa, w_branch_gla, w_out, mix_norm_post, ffn2_norm_pre, ffn2_w_gate_up, ffn2_w_down, ffn2_norm_post):
    stacked = dict(zip(_LAYER_PARAMS, (
        ffn1_norm_pre, ffn1_w_gate_up, ffn1_w_down, ffn1_norm_post, mix_norm_pre, w_in,
        gdn_conv_w, gdn_a_log, gdn_dt_bias, gdn_norm_w,
        nsa_pe_k, nsa_cmp_k_w1, nsa_cmp_k_w2, nsa_pe_v, nsa_cmp_v_w1, nsa_cmp_v_w2,
        gla_gate_w, gla_gate_b, gla_norm_w,
        w_branch_gdn, w_branch_nsa, w_branch_gla, w_out, mix_norm_post,
        ffn2_norm_pre, ffn2_w_gate_up, ffn2_w_down, ffn2_norm_post)))
    batch, seq, d = x.shape
    tables = _bias_tables(rel_bias)
    h = x.reshape(batch * seq, d)
    for layer in range(ffn1_norm_pre.shape[0]):
        h = _layer(h, batch, tables, {name: value[layer] for name, value in stacked.items()})
    return h.reshape(batch, seq, d)
```

```python
import functools
import math

import jax
import jax.numpy as jnp
import numpy as np
from jax import lax
from jax.experimental import pallas as pl
from jax.experimental.pallas import tpu as pltpu

F32 = jnp.float32
BF16 = jnp.bfloat16
HIGHEST = lax.Precision.HIGHEST

LANE = 128
VMEM_LIMIT_BYTES = 56 * 1024 * 1024

NORM_EPS = 1e-6
FFN_RES_SCALE = 0.5
REL_BUCKETS = 32
REL_MAX_DIST = 128
GDN_HEADS, GDN_DK, GDN_DV, GDN_CONV, GDN_CHUNK = 4, 128, 128, 4, 64
NSA_HEADS, NSA_GROUPS, NSA_DK, NSA_DV = 8, 2, 96, 64
NSA_HPG = NSA_HEADS // NSA_GROUPS
CMP_LEN, CMP_STRIDE, CMP_HIDDEN = 32, 16, 256
SEL_LEN, SEL_TOPK, WINDOW, Q_BLOCK = 64, 16, 512, 128
GLA_HEADS, GLA_DK, GLA_DV, GLA_RANK, GLA_TAU, GLA_CHUNK = 4, 64, 128, 16, 16, 16
N_BRANCH = 3
MASK_VALUE = -1e30
L2_EPS = 1e-6

CMP_PER_Q = Q_BLOCK // CMP_STRIDE
CMP_BAND_LO = -10
CMP_BAND = 17
CMP_BAND_PAD = 32


def _nt(a, b, **kw):
    return lax.dot_general(a, b, (((1,), (1,)), ((), ())), preferred_element_type=F32, **kw)


def _tn(a, b, **kw):
    return lax.dot_general(a, b, (((0,), (0,)), ((), ())), preferred_element_type=F32, **kw)


def _mm(a, b, **kw):
    return jnp.dot(a, b, preferred_element_type=F32, **kw)


def _rms(x, w):
    return x * lax.rsqrt(jnp.mean(x * x, axis=-1, keepdims=True) + NORM_EPS) * w


def _silu(x):
    return x * jax.nn.sigmoid(x)


def _softplus(x):
    return jnp.maximum(x, 0.0) + jnp.log1p(jnp.exp(-jnp.abs(x)))


def _params(*semantics):
    return pltpu.CompilerParams(dimension_semantics=semantics, vmem_limit_bytes=VMEM_LIMIT_BYTES)


def _resident(shape):
    return pl.BlockSpec(shape, lambda *_: (0,) * len(shape), pipeline_mode=pl.Buffered(1))


FFN_TOKENS = 512
FFN_CHUNK = 256


def _ffn_kernel(x_ref, npre_ref, wgu_ref, wd_ref, npost_ref, o_ref):
    d_ff = wd_ref.shape[0]
    x = x_ref[...]
    h = _rms(x, npre_ref[...]).astype(BF16)
    acc = jnp.zeros(x.shape, F32)
    for c in range(d_ff // FFN_CHUNK):
        lo = c * FFN_CHUNK
        g = _mm(h, wgu_ref[:, lo:lo + FFN_CHUNK])
        u = _mm(h, wgu_ref[:, d_ff + lo:d_ff + lo + FFN_CHUNK])
        a = (_silu(g) * u).astype(BF16)
        acc = acc + _mm(a, wd_ref[lo:lo + FFN_CHUNK, :])
    o_ref[...] = x + FFN_RES_SCALE * _rms(acc, npost_ref[...])


def _ffn(x, norm_pre, w_gate_up, w_down, norm_post):
    m, d = x.shape
    d_ff = w_down.shape[0]
    row = pl.BlockSpec((FFN_TOKENS, d), lambda i: (i, 0))
    return pl.pallas_call(
        _ffn_kernel,
        grid=(m // FFN_TOKENS,),
        in_specs=[row, _resident((1, d)), _resident((d, 2 * d_ff)), _resident((d_ff, d)), _resident((1, d))],
        out_specs=row,
        out_shape=jax.ShapeDtypeStruct((m, d), F32),
        compiler_params=_params("parallel"),
        name="ffn",
    )(x, norm_pre.reshape(1, d), w_gate_up.astype(BF16), w_down.astype(BF16), norm_post.reshape(1, d))


PROJ_TOKENS = 512
PROJ_CHUNK = 512


def _proj_kernel(x_ref, nw_ref, w_ref, *o_refs, segs):
    h = _rms(x_ref[...], nw_ref[...]).astype(BF16)
    for (off, width, epilogue), o_ref in zip(segs, o_refs):
        for lo in range(0, width, PROJ_CHUNK):
            hi = min(lo + PROJ_CHUNK, width)
            y = _mm(h, w_ref[:, off + lo:off + hi])
            if epilogue == "sigmoid":
                y = jax.nn.sigmoid(y)
            elif epilogue == "ones_hi":
                lane = lax.broadcasted_iota(jnp.int32, y.shape, 1) % LANE
                y = jnp.where(lane >= LANE // 2, 1.0, y)
            elif epilogue is not None:
                y = y * epilogue
            o_ref[:, lo:hi] = y.astype(o_ref.dtype)


def _proj(x, norm_w, pieces):
    m, d = x.shape
    segs, cols, off = [], [], 0
    for w, _, epilogue in pieces:
        width = w.shape[1]
        pad = (-width) % LANE
        segs.append((off, width, epilogue))
        cols.append(w)
        if pad:
            cols.append(jnp.zeros((d, pad), w.dtype))
        off += width + pad
    w_all = jnp.concatenate(cols, axis=1).astype(BF16)
    row = pl.BlockSpec((PROJ_TOKENS, d), lambda i: (i, 0))
    return pl.pallas_call(
        functools.partial(_proj_kernel, segs=tuple(segs)),
        grid=(m // PROJ_TOKENS,),
        in_specs=[row, _resident((1, d)), _resident((d, off))],
        out_specs=[pl.BlockSpec((PROJ_TOKENS, w.shape[1]), lambda i: (i, 0)) for w, _, _ in pieces],
        out_shape=[jax.ShapeDtypeStruct((m, w.shape[1]), dt) for w, dt, _ in pieces],
        compiler_params=_params("parallel"),
        name="in_proj",
    )(x, norm_w.reshape(1, d), w_all)


GDN_TOKENS = 512
CONV_PAD = 8


def _split(x):
    hi = x.astype(BF16)
    return hi, (x - hi.astype(F32)).astype(BF16)


def _gdn_kernel(q_ref, k_ref, v_ref, z_ref, ba_ref, cwq_ref, cwk_ref, cwv_ref, alog_ref, dtb_ref, nw_ref,
                o_ref, xq_ref, xk_ref, xv_ref, gc_ref, cdec_ref,
                q16_ref, k16_ref, kb16_ref, qd_ref, kd_ref, vb_ref, kbe_ref, state_ref):
    t_blk = q_ref.shape[1]
    c_len = GDN_CHUNK
    width = GDN_HEADS * LANE

    @pl.when(pl.program_id(1) == 0)
    def _():
        state_ref[...] = jnp.zeros_like(state_ref)
        for x_ref in (xq_ref, xk_ref, xv_ref):
            x_ref[0:CONV_PAD, :] = jnp.zeros((CONV_PAD, width), F32)

    def conv_silu(raw_ref, x_ref, cw_ref):
        x_ref[CONV_PAD:CONV_PAD + t_blk, :] = raw_ref[0]
        y = jnp.zeros((t_blk, width), F32)
        for tap in range(GDN_CONV):
            y = y + cw_ref[tap:tap + 1, :] * x_ref[pl.ds(CONV_PAD - (GDN_CONV - 1) + tap, t_blk), :]
        x_ref[0:CONV_PAD, :] = x_ref[t_blk:t_blk + CONV_PAD, :]
        return _silu(y)

    q = conv_silu(q_ref, xq_ref, cwq_ref)
    k = conv_silu(k_ref, xk_ref, cwk_ref)
    v = conv_silu(v_ref, xv_ref, cwv_ref)

    ba = ba_ref[0]
    g_all = -jnp.exp(alog_ref[...]) * _softplus(ba + dtb_ref[...])
    src = lax.broadcasted_iota(jnp.int32, (LANE, width), 0)
    dst_head = lax.broadcasted_iota(jnp.int32, (LANE, width), 1) // LANE
    pick_b = (src == dst_head).astype(BF16)
    pick_a = (src == dst_head + GDN_HEADS).astype(BF16)
    b_hi, b_lo = _split(ba)
    g_hi, g_lo = _split(g_all)
    beta = jax.nn.sigmoid(_mm(b_hi, pick_b) + _mm(b_lo, pick_b))
    g_hi, g_lo = _split(_mm(g_hi, pick_a) + _mm(g_lo, pick_a))

    ti = lax.broadcasted_iota(jnp.int32, (t_blk, t_blk), 0)
    tj = lax.broadcasted_iota(jnp.int32, (t_blk, t_blk), 1)
    same = (ti // c_len) == (tj // c_len)
    in_chunk_tri = (same & (tj <= ti)).astype(BF16)
    in_chunk = same.astype(BF16)
    gc_all = _mm(in_chunk_tri, g_hi) + _mm(in_chunk_tri, g_lo)
    gl_all = _mm(in_chunk, g_hi) + _mm(in_chunk, g_lo)
    gc_ref[...] = gc_all
    cdec_ref[...] = jnp.exp(gl_all)
    e_gc = jnp.exp(gc_all)
    e_rest = jnp.exp(gl_all - gc_all)
    for h in range(GDN_HEADS):
        cols = slice(h * LANE, (h + 1) * LANE)
        qh, kh = q[:, cols], k[:, cols]
        qh = qh * lax.rsqrt(jnp.sum(qh * qh, axis=-1, keepdims=True) + L2_EPS) * (GDN_DK ** -0.5)
        kh = kh * lax.rsqrt(jnp.sum(kh * kh, axis=-1, keepdims=True) + L2_EPS)
        kb = kh * beta[:, cols]
        q16_ref[:, cols] = qh.astype(BF16)
        k16_ref[:, cols] = kh.astype(BF16)
        kb16_ref[:, cols] = kb.astype(BF16)
        qd_ref[:, cols] = (qh * e_gc[:, cols]).astype(BF16)
        kd_ref[:, cols] = (kh * e_rest[:, cols]).astype(BF16)
        vb_ref[:, cols] = (v[:, cols] * beta[:, cols]).astype(BF16)
        kbe_ref[:, cols] = (kb * e_gc[:, cols]).astype(BF16)

    ci = lax.broadcasted_iota(jnp.int32, (c_len, c_len), 0)
    cj = lax.broadcasted_iota(jnp.int32, (c_len, c_len), 1)
    lower = ci >= cj
    strict = ci > cj
    eye = (ci == cj).astype(F32)
    lane0 = (lax.broadcasted_iota(jnp.int32, (c_len, LANE), 1) == 0).astype(BF16)
    nw = nw_ref[...]

    def chunk_body(c, states):
        start = pl.multiple_of(c * c_len, c_len)
        rows = pl.ds(start, c_len)
        heads = range(GDN_HEADS)
        cols = [slice(h * LANE, (h + 1) * LANE) for h in heads]
        gc = [gc_ref[rows, cols[h]] for h in heads]
        gc_split = [_split(x) for x in gc]
        gc_row = [_nt(lane0, hi) + _nt(lane0, lo) for hi, lo in gc_split]
        decay = [jnp.exp(jnp.where(lower, gc[h][:, :c_len] - gc_row[h], MASK_VALUE)) for h in heads]
        k16 = [k16_ref[rows, cols[h]] for h in heads]
        kk = [_nt(kb16_ref[rows, cols[h]], k16[h]) for h in heads]
        pw = [jnp.where(strict, -(kk[h] * decay[h]), 0.0) for h in heads]
        inv = [eye + p for p in pw]
        for _ in range(5):
            p16 = [p.astype(BF16) for p in pw]
            pw = [_mm(p, p) for p in p16]
            inv = [inv[h] + _mm(inv[h].astype(BF16), pw[h].astype(BF16)) for h in heads]
        inv16 = [x.astype(BF16) for x in inv]
        u = [_mm(inv16[h], vb_ref[rows, cols[h]]) for h in heads]
        w = [_mm(inv16[h], kbe_ref[rows, cols[h]]).astype(BF16) for h in heads]
        attn = [(_nt(q16_ref[rows, cols[h]], k16[h]) * decay[h]).astype(BF16) for h in heads]
        s16 = [states[h].astype(BF16) for h in heads]
        v16 = [(u[h] - _mm(w[h], s16[h])).astype(BF16) for h in heads]
        o = [_mm(qd_ref[rows, cols[h]], s16[h]) + _mm(attn[h], v16[h]) for h in heads]
        new_states = [states[h] * cdec_ref[pl.ds(start, 1), cols[h]] + _tn(kd_ref[rows, cols[h]], v16[h])
                      for h in heads]
        o_ref[0, rows, :] = jnp.concatenate([_rms(x, nw) for x in o], axis=-1) * _silu(z_ref[0, rows, :])
        return jnp.stack(new_states)

    state_ref[...] = lax.fori_loop(0, t_blk // c_len, chunk_body, state_ref[...])


def _gdn(qkv, z, ba, conv_w, a_log, dt_bias, norm_w, batch):
    m = qkv.shape[0]
    s = m // batch
    h, dk, dv = GDN_HEADS, GDN_DK, GDN_DV
    width = h * LANE
    qkv = qkv.reshape(batch, s, 3 * width)
    tok = lambda col: pl.BlockSpec((1, GDN_TOKENS, width), lambda b, i, col=col: (b, i, col))
    cw = lambda col: pl.BlockSpec((GDN_CONV, width), lambda b, i, col=col: (0, col))
    lane_pad = jnp.zeros((LANE - 2 * h,), F32)
    alog_row = jnp.concatenate([jnp.zeros((h,), F32), a_log, lane_pad]).reshape(1, LANE)
    dtb_row = jnp.concatenate([jnp.zeros((h,), F32), dt_bias, lane_pad]).reshape(1, LANE)
    out = pl.pallas_call(
        _gdn_kernel,
        grid=(batch, s // GDN_TOKENS),
        in_specs=[tok(0), tok(1), tok(2), tok(0),
                  pl.BlockSpec((1, GDN_TOKENS, LANE), lambda b, i: (b, i, 0)),
                  cw(0), cw(1), cw(2),
                  _resident((1, LANE)), _resident((1, LANE)), _resident((1, dv))],
        out_specs=tok(0),
        out_shape=jax.ShapeDtypeStruct((batch, s, width), F32),
        scratch_shapes=[pltpu.VMEM((GDN_TOKENS + CONV_PAD, width), F32)] * 3
        + [pltpu.VMEM((GDN_TOKENS, width), F32)] * 2 + [pltpu.VMEM((GDN_TOKENS, width), BF16)] * 7
        + [pltpu.VMEM((h, dk, dv), F32)],
        compiler_params=_params("parallel", "arbitrary"),
        name="gdn",
    )(qkv, qkv, qkv, z.reshape(batch, s, width), ba.reshape(batch, s, LANE),
      conv_w, conv_w, conv_w, alog_row, dtb_row, norm_w.reshape(1, dv))
    return out.reshape(m, width)


GLA_TOKENS = 256


def _log_sigmoid(x):
    return jnp.minimum(x, 0.0) - jnp.log1p(jnp.exp(-jnp.abs(x)))


def _gla_kernel(qk_ref, v_ref, r_ref, a_ref, gw_ref, gb_ref, nw_ref, o_ref, ks_ref, bs_ref, vs_ref, state_ref):
    t_blk = qk_ref.shape[1]
    c_len = GLA_CHUNK
    hdk = GLA_HEADS * GLA_DK
    hdv = GLA_HEADS * GLA_DV

    @pl.when(pl.program_id(1) == 0)
    def _():
        state_ref[...] = jnp.zeros_like(state_ref)
        ks_ref[0:c_len, :] = jnp.zeros((c_len, hdk), F32)
        bs_ref[0:c_len, :] = jnp.zeros((c_len, hdk), F32)
        vs_ref[0:c_len, :] = jnp.zeros((c_len, hdv), F32)

    qk = qk_ref[0]
    q = qk[:, :hdk] * (GLA_DK ** -0.5)
    k = qk[:, hdk:]
    v = v_ref[0]
    log_a = _log_sigmoid(_mm(a_ref[0].astype(BF16), gw_ref[...]) + gb_ref[...]) * (1.0 / GLA_TAU)

    ti = lax.broadcasted_iota(jnp.int32, (t_blk, t_blk), 0)
    tj = lax.broadcasted_iota(jnp.int32, (t_blk, t_blk), 1)
    same = (ti // c_len) == (tj // c_len)
    cum = _mm((same & (tj <= ti)).astype(F32), log_a, precision=HIGHEST)
    tot = _mm(same.astype(F32), log_a, precision=HIGHEST)

    ks_ref[c_len:, :] = k
    bs_ref[c_len:, :] = cum
    vs_ref[c_len:, :] = v
    pos = lax.broadcasted_iota(jnp.int32, (t_blk, 1), 0) % c_len
    hd = lax.broadcasted_iota(jnp.int32, (hdk, hdv), 0) // GLA_DK
    he = lax.broadcasted_iota(jnp.int32, (hdk, hdv), 1) // GLA_DV
    spread = (hd == he).astype(BF16)
    o = jnp.zeros((t_blk, hdv), F32)
    for off in range(c_len):
        k_o = ks_ref[pl.ds(c_len - off, t_blk), :]
        b_o = bs_ref[pl.ds(c_len - off, t_blk), :]
        v_o = vs_ref[pl.ds(c_len - off, t_blk), :]
        w = jnp.exp(jnp.where(pos >= off, cum - b_o, MASK_VALUE))
        o = o + _mm((q * k_o * w).astype(BF16), spread) * v_o

    q_dec = (q * jnp.exp(cum)).astype(BF16)
    k_dec = (k * jnp.exp(tot - cum)).astype(BF16)
    c_dec = jnp.exp(tot)
    v16 = v.astype(BF16)
    states = [state_ref[h] for h in range(GLA_HEADS)]
    inter = []
    for n in range(t_blk // c_len):
        rows = slice(n * c_len, (n + 1) * c_len)
        parts = []
        for h in range(GLA_HEADS):
            dks = slice(h * GLA_DK, (h + 1) * GLA_DK)
            dvs = slice(h * GLA_DV, (h + 1) * GLA_DV)
            parts.append(_nt(q_dec[rows, dks], states[h].astype(BF16)))
            states[h] = states[h] * c_dec[n * c_len:n * c_len + 1, dks] + _tn(v16[rows, dvs], k_dec[rows, dks])
        inter.append(jnp.concatenate(parts, axis=-1))
    for h in range(GLA_HEADS):
        state_ref[h] = states[h]
    o = o + jnp.concatenate(inter, axis=0)

    r = r_ref[0]
    nw = nw_ref[...]
    for h in range(GLA_HEADS):
        dvs = slice(h * GLA_DV, (h + 1) * GLA_DV)
        o_ref[0, :, dvs] = _rms(o[:, dvs], nw) * _silu(r[:, dvs])


def _gla(qk, v, r, a_low, gate_w, gate_b, norm_w, batch):
    m = qk.shape[0]
    s = m // batch
    hdk, hdv = GLA_HEADS * GLA_DK, GLA_HEADS * GLA_DV
    gw = jnp.pad(gate_w, ((0, LANE - GLA_RANK), (0, 0))).astype(BF16)
    blk = lambda w: pl.BlockSpec((1, GLA_TOKENS, w), lambda b, i: (b, i, 0))
    out = pl.pallas_call(
        _gla_kernel,
        grid=(batch, s // GLA_TOKENS),
        in_specs=[blk(2 * hdk), blk(hdv), blk(hdv), blk(LANE),
                  _resident((LANE, hdk)), _resident((1, hdk)), _resident((1, GLA_DV))],
        out_specs=blk(hdv),
        out_shape=jax.ShapeDtypeStruct((batch, s, hdv), F32),
        scratch_shapes=[pltpu.VMEM((GLA_TOKENS + GLA_CHUNK, hdk), F32),
                        pltpu.VMEM((GLA_TOKENS + GLA_CHUNK, hdk), F32),
                        pltpu.VMEM((GLA_TOKENS + GLA_CHUNK, hdv), F32),
                        pltpu.VMEM((GLA_HEADS, GLA_DV, GLA_DK), F32)],
        compiler_params=_params("parallel", "arbitrary"),
        name="gla",
    )(qk.reshape(batch, s, 2 * hdk), v.reshape(batch, s, hdv), r.reshape(batch, s, hdv),
      a_low.reshape(batch, s, LANE), gw, gate_b.reshape(1, hdk), norm_w.reshape(1, GLA_DV))
    return out.reshape(m, hdv)


def _gelu_tanh(x):
    return 0.5 * x * (1.0 + jnp.tanh(math.sqrt(2.0 / math.pi) * (x + 0.044715 * x * x * x)))


def _cmp_kernel(seg_ref, wa_ref, wb_ref, pek_ref, w1k_ref, pev_ref, w1v_ref, w2_ref, k_ref, v_ref):
    seg = seg_ref[0]
    n_seg = seg.shape[0]
    first = _mm(seg, wa_ref[...])
    second = _mm(seg, wb_ref[...])
    bias_k = _mm(pek_ref[...], w1k_ref[...])[0:1]
    bias_v = _mm(pev_ref[...], w1v_ref[...])[0:1]
    bias = jnp.concatenate([bias_k, bias_v] * NSA_GROUPS, axis=-1)
    hid = _gelu_tanh(first + pltpu.roll(second, n_seg - 1, 0) + bias).astype(BF16)
    out = _mm(hid, w2_ref[...])
    kw = NSA_GROUPS * LANE
    k_ref[0] = out[:, :kw].astype(BF16)
    v_ref[0] = out[:, kw:].astype(BF16)


def _cmp_weights(pe_k, w1k, w2k, pe_v, w1v, w2v):
    g, dk, dv, hid = NSA_GROUPS, NSA_DK, NSA_DV, CMP_HIDDEN
    w1k = w1k.reshape(CMP_LEN, dk, hid)
    w1v = w1v.reshape(CMP_LEN, dv, hid)

    same_group = np.eye(g, dtype=np.float32)

    def half(lo):
        top = jnp.concatenate([w1k[lo:lo + CMP_STRIDE], jnp.zeros((CMP_STRIDE, dk, hid), F32)], axis=-1)
        bot = jnp.concatenate([jnp.zeros((CMP_STRIDE, dv, hid), F32), w1v[lo:lo + CMP_STRIDE]], axis=-1)
        per_group = jnp.concatenate([top, bot], axis=1)
        blk = per_group[:, None, :, None, :] * same_group[None, :, None, :, None]
        return blk.reshape(CMP_STRIDE * g * (dk + dv), g * 2 * hid).astype(BF16)

    padded = jnp.stack([jnp.pad(w2k, ((0, 0), (0, LANE - dk))), jnp.pad(w2v, ((0, 0), (0, LANE - dv)))])
    place = np.zeros((g, 2, 2 * g), np.float32)
    for gg in range(g):
        place[gg, 0, gg] = 1.0
        place[gg, 1, g + gg] = 1.0
    w2 = (padded[None, :, :, None, :] * place[:, :, None, :, None]).reshape(g * 2 * hid, 2 * g * LANE).astype(BF16)
    pad8 = lambda pe: jnp.pad(pe.reshape(1, -1), ((0, 7), (0, 0))).astype(BF16)
    return half(0), half(CMP_STRIDE), pad8(pe_k), w1k.reshape(-1, hid).astype(BF16), pad8(pe_v), \
        w1v.reshape(-1, hid).astype(BF16), w2


def _compress(kv_cmp, weights, batch):
    m, width = kv_cmp.shape
    s = m // batch
    n_seg = s // CMP_STRIDE
    seg = kv_cmp.reshape(batch, n_seg, CMP_STRIDE * width)
    wa, wb, pek, w1k, pev, w1v, w2 = weights
    full = lambda a: _resident(a.shape)
    kw = vw = NSA_GROUPS * LANE
    return pl.pallas_call(
        _cmp_kernel,
        grid=(batch,),
        in_specs=[pl.BlockSpec((1, n_seg, CMP_STRIDE * width), lambda b: (b, 0, 0)),
                  full(wa), full(wb), full(pek), full(w1k), full(pev), full(w1v), full(w2)],
        out_specs=[pl.BlockSpec((1, n_seg, kw), lambda b: (b, 0, 0)),
                   pl.BlockSpec((1, n_seg, vw), lambda b: (b, 0, 0))],
        out_shape=[jax.ShapeDtypeStruct((batch, n_seg, kw), BF16), jax.ShapeDtypeStruct((batch, n_seg, vw), BF16)],
        compiler_params=_params("parallel"),
        name="nsa_compress",
    )(seg, wa, wb, pek, w1k, pev, w1v, w2)


NSA_KEY_CHUNK = 512


def _online_step(carry, s, v):
    m_old, acc = carry
    m_new = jnp.maximum(m_old, jnp.max(s, axis=-1, keepdims=True))
    p = jnp.exp(s - m_new).astype(BF16)
    return m_new, jnp.exp(m_old - m_new) * acc + _mm(p, v)


def _normalised(acc):
    low = lax.broadcasted_iota(jnp.int32, acc.shape, 1) < NSA_DV
    return jnp.where(low, acc / pltpu.roll(acc, NSA_DV, 1), 0.0)


def _nsa_kernel(far_ref, q_ref, gate_ref, kc_ref, vc_ref, ks_ref, vs_ref, eb_ref, kw_ref, vw_ref, gcb_ref, bpd_ref,
                o_ref, *, n_sel, k_top):
    qb = pl.program_id(1)
    tq = Q_BLOCK
    rows = NSA_HPG * tq
    n_cmp_pad = kc_ref.shape[1]
    q0 = qb * tq
    qi = lax.broadcasted_iota(jnp.int32, (rows, 1), 0) % tq
    row_t = q0 + qi

    cmp_n = lax.broadcasted_iota(jnp.int32, (rows, n_cmp_pad), 1)
    valid_c = row_t >= cmp_n * CMP_STRIDE + (CMP_LEN - 1)
    any_c = (row_t >= CMP_LEN - 1).astype(F32)
    band_n = lax.broadcasted_iota(jnp.int32, (n_cmp_pad, LANE), 0)
    band_l = lax.broadcasted_iota(jnp.int32, (n_cmp_pad, LANE), 1)
    band_m = band_l % CMP_BAND_PAD
    band_keys = ((band_l < 2 * CMP_BAND_PAD)
                 & ((band_n - CMP_PER_Q * qb == band_m + CMP_BAND_LO) | (band_m == CMP_BAND_PAD - 1))).astype(BF16)
    ov_s = lax.broadcasted_iota(jnp.int32, (n_sel, n_cmp_pad), 0) * SEL_LEN
    ov_c = lax.broadcasted_iota(jnp.int32, (n_sel, n_cmp_pad), 1) * CMP_STRIDE
    overlap_t = ((ov_c < ov_s + SEL_LEN) & (ov_c + CMP_LEN > ov_s)).astype(BF16)

    sel_s = lax.broadcasted_iota(jnp.int32, (n_sel, tq), 0)
    blk_t = (q0 + lax.broadcasted_iota(jnp.int32, (n_sel, tq), 1)) // SEL_LEN
    forced = (sel_s == 0) | (sel_s == blk_t) | (sel_s == blk_t - 1)
    future = sel_s > blk_t

    q_all = q_ref[0]
    gates = gate_ref[0]
    prev_base = pl.multiple_of(jnp.maximum(qb - 1, 0) * tq, tq)
    diag_base = pl.multiple_of(q0, tq)

    def add_far(s, heads):
        return jnp.concatenate([s[i * tq:(i + 1) * tq] + far_ref[h] for i, h in enumerate(heads)], axis=0)

    groups = range(NSA_GROUPS)
    heads = [range(g * NSA_HPG, (g + 1) * NSA_HPG) for g in groups]
    cols = [slice(g * LANE, (g + 1) * LANE) for g in groups]
    q4 = [jnp.concatenate([q_all[:, h * LANE:(h + 1) * LANE] for h in heads[g]], axis=0) for g in groups]

    s = [_nt(jnp.concatenate([q4[g], gcb_ref[g]], axis=1),
             jnp.concatenate([kc_ref[0, :, cols[g]], band_keys], axis=1)) for g in groups]
    s = [jnp.where(valid_c, x, MASK_VALUE) for x in s]
    p = [jnp.exp(x - jnp.max(x, axis=-1, keepdims=True)) for x in s]
    p = [x * (any_c / jnp.sum(x, axis=-1, keepdims=True)) for x in p]
    o_cmp = [_mm(p[g].astype(BF16), vc_ref[0, :, cols[g]]) for g in groups]
    p_sum = [sum(x[i * tq:(i + 1) * tq] for i in range(1, NSA_HPG)) + x[0:tq] for x in p]

    p_hi = [x.astype(BF16) for x in p_sum]
    p_lo = [(p_sum[g] - p_hi[g].astype(F32)).astype(BF16) for g in groups]
    p_lo2 = [(p_sum[g] - p_hi[g].astype(F32) - p_lo[g].astype(F32)).astype(BF16) for g in groups]
    imp = [_nt(overlap_t, p_hi[g]) + (_nt(overlap_t, p_lo[g]) + _nt(overlap_t, p_lo2[g]))
           for g in groups]
    imp = [jnp.where(forced, jnp.inf, jnp.where(future, -jnp.inf, x)) for x in imp]
    chosen = [jnp.zeros((n_sel, tq), F32) for _ in groups]
    for _ in range(k_top):
        best = [jnp.max(x, axis=0, keepdims=True) for x in imp]
        first = [jnp.min(jnp.where(imp[g] == best[g], sel_s, n_sel), axis=0, keepdims=True) for g in groups]
        pick = [sel_s == x for x in first]
        chosen = [jnp.where(pick[g], 1.0, chosen[g]) for g in groups]
        imp = [jnp.where(pick[g], -jnp.inf, imp[g]) for g in groups]
    skipped = [1.0 - x for x in chosen]
    older = sel_s < (tq // SEL_LEN) * (qb - 1)
    skipped_far = [jnp.where(older, x, 1.0) for x in skipped]
    q_far = [jnp.concatenate([q4[g], jnp.concatenate([skipped_far[g].T.astype(BF16)] * NSA_HPG, axis=0)], axis=1)
             for g in groups]
    q_near = [jnp.concatenate([q4[g], jnp.concatenate([skipped[g].T.astype(BF16)] * NSA_HPG, axis=0)], axis=1)
              for g in groups]

    def far_logits(g, base):
        k_aug = jnp.concatenate([ks_ref[0, pl.ds(base, NSA_KEY_CHUNK), cols[g]],
                                 eb_ref[pl.ds(base, NSA_KEY_CHUNK), :]], axis=1)
        return add_far(_nt(q_far[g], k_aug), heads[g])

    def far_body(c, carry):
        base0 = pl.multiple_of(c * (2 * NSA_KEY_CHUNK), NSA_KEY_CHUNK)
        base1 = pl.multiple_of(base0 + NSA_KEY_CHUNK, NSA_KEY_CHUNK)
        s0 = [far_logits(g, base0) for g in groups]
        s1 = [far_logits(g, base1) for g in groups]
        carry = [_online_step(carry[g], s0[g], vs_ref[0, pl.ds(base0, NSA_KEY_CHUNK), cols[g]]) for g in groups]
        carry = [_online_step(carry[g], s1[g], vs_ref[0, pl.ds(base1, NSA_KEY_CHUNK), cols[g]]) for g in groups]
        return tuple(carry)

    n_far = (jnp.maximum(qb - 1, 0) * tq + 2 * NSA_KEY_CHUNK - 1) // (2 * NSA_KEY_CHUNK)
    carry = lax.fori_loop(0, n_far, far_body,
                          tuple((jnp.full((rows, 1), MASK_VALUE, F32), jnp.zeros((rows, LANE), F32)) for _ in groups))

    e_near = jnp.concatenate([eb_ref[pl.ds(prev_base, tq), :], eb_ref[pl.ds(diag_base, tq), :]], axis=0)
    k_aug = [jnp.concatenate(
        [jnp.concatenate([ks_ref[0, pl.ds(prev_base, tq), cols[g]], ks_ref[0, pl.ds(diag_base, tq), cols[g]]], axis=0),
         e_near], axis=1) for g in groups]
    v_near = [jnp.concatenate([vs_ref[0, pl.ds(prev_base, tq), cols[g]], vs_ref[0, pl.ds(diag_base, tq), cols[g]]],
                              axis=0) for g in groups]
    key = lax.broadcasted_iota(jnp.int32, (rows, 2 * tq), 1)
    first_key = jnp.where(qb >= 1, 0, tq)
    valid = (key >= first_key) & (key <= qi + tq)
    s = [jnp.where(valid, _nt(q_near[g], k_aug[g]) + bpd_ref[g], MASK_VALUE) for g in groups]
    o_sel = [_normalised(_online_step(carry[g], s[g], v_near[g])[1]) for g in groups]

    n_win = WINDOW + tq
    s = [_nt(q4[g], kw_ref[0, pl.ds(diag_base, n_win), cols[g]]) for g in groups]
    s = [jnp.concatenate([add_far(s[g][:, :n_win - 2 * tq], heads[g]), s[g][:, n_win - 2 * tq:] + bpd_ref[g]], axis=1)
         for g in groups]
    key = lax.broadcasted_iota(jnp.int32, (rows, n_win), 1)
    valid = (key > qi) & (key <= qi + WINDOW) & (key >= WINDOW - q0)
    s = [jnp.where(valid, x, MASK_VALUE) for x in s]
    p = [jnp.exp(x - jnp.max(x, axis=-1, keepdims=True)).astype(BF16) for x in s]
    o_win = [_normalised(_mm(p[g], vw_ref[0, pl.ds(diag_base, n_win), cols[g]])) for g in groups]

    outs = []
    for g in groups:
        for i, h in enumerate(heads[g]):
            r = slice(i * tq, (i + 1) * tq)
            c0 = h * N_BRANCH
            outs.append(gates[:, c0:c0 + 1] * o_cmp[g][r]
                        + gates[:, c0 + 1:c0 + 2] * o_sel[g][r]
                        + gates[:, c0 + 2:c0 + 3] * o_win[g][r])
    o_ref[0] = jnp.concatenate(outs, axis=1)


def _rel_bucket(dist):
    n = np.maximum(dist, 0)
    max_exact = REL_BUCKETS // 2
    nf = np.maximum(n, 1).astype(np.float32)
    large = max_exact + (np.log(nf / np.float32(max_exact)) / np.float32(math.log(REL_MAX_DIST / max_exact))
                         * np.float32(REL_BUCKETS - max_exact)).astype(np.int32)
    large = np.minimum(large, REL_BUCKETS - 1)
    return np.where(n < max_exact, n, large)


def _bias_tables(rel_bias):
    tbl = rel_bias.astype(F32).T

    def lookup(dist):
        onehot = np.eye(REL_BUCKETS, dtype=np.float32)[_rel_bucket(dist)]
        return jnp.einsum("hb,ijb->hij", tbl, onehot, precision=HIGHEST)

    i = np.arange(Q_BLOCK)
    dist = i[:, None] - i[None, :]
    diag = lookup(dist)
    prev = lookup(dist + Q_BLOCK)
    far = tbl[:, REL_BUCKETS - 1]
    m = np.arange(CMP_BAND_PAD)
    d_c = i[:, None] - CMP_STRIDE * (m[None, :] + CMP_BAND_LO) - (CMP_LEN - 1)
    band = lookup(d_c) - far[:, None, None]
    last = (m == CMP_BAND_PAD - 1)[None, None, :]
    band = jnp.where(last, far[:, None, None], jnp.where((m < CMP_BAND)[None, None, :], band, 0.0))
    band = band.reshape(NSA_GROUPS, NSA_HPG * Q_BLOCK, CMP_BAND_PAD)
    band_hi = band.astype(BF16)
    band_lo = (band - band_hi.astype(F32)).astype(BF16)
    band = jnp.pad(jnp.concatenate([band_hi, band_lo], axis=-1), ((0, 0), (0, 0), (0, LANE - 2 * CMP_BAND_PAD)))
    near = jnp.concatenate([prev, diag], axis=-1).reshape(NSA_GROUPS, NSA_HPG * Q_BLOCK, 2 * Q_BLOCK)
    return far, band, near


def _nsa(q, gates, k_cmp, v_cmp, k_sel, v_sel, k_win, v_win, tables, batch):
    m = q.shape[0]
    s = m // batch
    assert s % (2 * NSA_KEY_CHUNK) == 0
    n_sel = s // SEL_LEN
    k_top = min(SEL_TOPK, n_sel)
    hw = NSA_HEADS * LANE
    kvw = NSA_GROUPS * LANE
    far, band, near = tables
    n_seg = k_cmp.shape[1]
    blk_of_key = jnp.arange(s)[:, None] // SEL_LEN == jnp.arange(n_sel)[None, :]
    key_blk = jnp.where(blk_of_key, MASK_VALUE, 0.0).astype(BF16)
    front = ((0, 0), (WINDOW, 0), (0, 0))
    k_win = jnp.pad(k_win.reshape(batch, s, kvw), front)
    v_win = jnp.pad(v_win.reshape(batch, s, kvw), front)
    per_batch = lambda n: pl.BlockSpec((1, n, kvw), lambda b, i: (b, 0, 0), pipeline_mode=pl.Buffered(1))
    tok = lambda w: pl.BlockSpec((1, Q_BLOCK, w), lambda b, i: (b, i, 0))
    out = pl.pallas_call(
        functools.partial(_nsa_kernel, n_sel=n_sel, k_top=k_top),
        grid=(batch, s // Q_BLOCK),
        in_specs=[pl.BlockSpec(memory_space=pltpu.SMEM),
                  tok(hw), tok(LANE),
                  per_batch(n_seg), per_batch(n_seg),
                  per_batch(s), per_batch(s), _resident(key_blk.shape), per_batch(s + WINDOW), per_batch(s + WINDOW),
                  _resident(band.shape), _resident(near.shape)],
        out_specs=tok(hw),
        out_shape=jax.ShapeDtypeStruct((batch, s, hw), F32),
        compiler_params=_params("parallel", "arbitrary"),
        name="nsa_attend",
    )(far, q.reshape(batch, s, hw), gates.reshape(batch, s, LANE), k_cmp, v_cmp,
      k_sel.reshape(batch, s, kvw), v_sel.reshape(batch, s, kvw), key_blk, k_win, v_win, band, near)
    return out.reshape(m, hw)


MERGE_TOKENS = 512


def _merge_kernel(x_ref, oa_ref, ob_ref, oc_ref, gate_ref, wa_ref, wb_ref, wc_ref, wo_ref, nw_ref, o_ref):
    d = x_ref.shape[1]
    merged = (gate_ref[:, 0:d] * _mm(oa_ref[...].astype(BF16), wa_ref[...])
              + gate_ref[:, d:2 * d] * _mm(ob_ref[...].astype(BF16), wb_ref[...])
              + gate_ref[:, 2 * d:3 * d] * _mm(oc_ref[...].astype(BF16), wc_ref[...]))
    y = _mm(merged.astype(BF16), wo_ref[...])
    o_ref[...] = x_ref[...] + _rms(y, nw_ref[...])


def _merge(x, o_a, o_b, o_c, gates, w_a, w_b, w_c, w_out, norm_post):
    m, d = x.shape
    row = lambda w: pl.BlockSpec((MERGE_TOKENS, w), lambda i: (i, 0))
    full = lambda a: _resident(a.shape)
    ws = [w.astype(BF16) for w in (w_a, w_b, w_c, w_out)]
    return pl.pallas_call(
        _merge_kernel,
        grid=(m // MERGE_TOKENS,),
        in_specs=[row(d), row(o_a.shape[1]), row(o_b.shape[1]), row(o_c.shape[1]), row(N_BRANCH * d),
                  full(ws[0]), full(ws[1]), full(ws[2]), full(ws[3]), _resident((1, d))],
        out_specs=row(d),
        out_shape=jax.ShapeDtypeStruct((m, d), F32),
        compiler_params=_params("parallel"),
        name="merge_out",
    )(x, o_a, o_b, o_c, gates, *ws, norm_post.reshape(1, d))


_IN_COLUMNS = (
    ("gdn_q", GDN_HEADS * GDN_DK), ("gdn_k", GDN_HEADS * GDN_DK), ("gdn_v", GDN_HEADS * GDN_DV),
    ("gdn_z", GDN_HEADS * GDN_DV), ("gdn_b", GDN_HEADS), ("gdn_a", GDN_HEADS),
    ("nsa_q", NSA_HEADS * NSA_DK), ("nsa_kv_cmp", NSA_GROUPS * (NSA_DK + NSA_DV)),
    ("nsa_kv_sel", NSA_GROUPS * (NSA_DK + NSA_DV)), ("nsa_kv_win", NSA_GROUPS * (NSA_DK + NSA_DV)),
    ("nsa_gate", 3 * NSA_HEADS),
    ("gla_q", GLA_HEADS * GLA_DK), ("gla_k", GLA_HEADS * GLA_DK), ("gla_v", GLA_HEADS * GLA_DV),
    ("gla_r", GLA_HEADS * GLA_DV), ("gla_a", GLA_RANK), ("merge_gate", None),
)


def _split_w_in(w_in):
    out, off = {}, 0
    for name, width in _IN_COLUMNS:
        width = w_in.shape[1] - off if width is None else width
        out[name] = w_in[:, off:off + width]
        off += width
    return out


def _pad_cols(w, width):
    return jnp.pad(w, ((0, 0), (0, width - w.shape[1])))


def _kv_layout(w):
    d = w.shape[0]
    w = w.reshape(d, NSA_GROUPS, NSA_DK + NSA_DV)
    k = jnp.pad(w[:, :, :NSA_DK], ((0, 0), (0, 0), (0, LANE - NSA_DK))).reshape(d, NSA_GROUPS * LANE)
    v = jnp.pad(w[:, :, NSA_DK:], ((0, 0), (0, 0), (0, LANE - NSA_DV))).reshape(d, NSA_GROUPS * LANE)
    return k, v


def _mix_pieces(w_in):
    d = w_in.shape[0]
    c = _split_w_in(w_in)
    gdn = [
        (jnp.concatenate([c["gdn_q"], c["gdn_k"], c["gdn_v"]], axis=1), F32, None),
        (c["gdn_z"], F32, None),
        (_pad_cols(jnp.concatenate([c["gdn_b"], c["gdn_a"]], axis=1), LANE), F32, None),
    ]
    gla = [
        (jnp.concatenate([c["gla_q"], c["gla_k"]], axis=1), F32, None),
        (c["gla_v"], F32, None),
        (c["gla_r"], F32, None),
        (_pad_cols(c["gla_a"], LANE), F32, None),
    ]
    q = jnp.pad(c["nsa_q"].reshape(d, NSA_HEADS, NSA_DK), ((0, 0), (0, 0), (0, LANE - NSA_DK)))
    k_sel, v_sel = _kv_layout(c["nsa_kv_sel"])
    k_win, v_win = _kv_layout(c["nsa_kv_win"])
    gate = _pad_cols(c["nsa_gate"], LANE)
    nsa = [
        (q.reshape(d, NSA_HEADS * LANE), BF16, NSA_DK ** -0.5),
        (c["nsa_kv_cmp"], BF16, None),
        (k_sel, BF16, None), (v_sel, BF16, "ones_hi"), (k_win, BF16, None), (v_win, BF16, "ones_hi"),
        (gate, F32, "sigmoid"),
        (c["merge_gate"], BF16, "sigmoid"),
    ]
    return gdn + gla, nsa


def _layer(x, batch, tables, p):
    x = _ffn(x, p["ffn1_norm_pre"], p["ffn1_w_gate_up"], p["ffn1_w_down"], p["ffn1_norm_post"])
    rec_pieces, nsa_pieces = _mix_pieces(p["w_in"])
    qkv, z, ba, gla_qk, gla_v, gla_r, gla_a = _proj(x, p["mix_norm_pre"], rec_pieces)
    nsa_q, kv_cmp, k_sel, v_sel, k_win, v_win, nsa_gate, merge_gate = _proj(x, p["mix_norm_pre"], nsa_pieces)
    o_a = _gdn(qkv, z, ba, p["gdn_conv_w"], p["gdn_a_log"], p["gdn_dt_bias"], p["gdn_norm_w"], batch)
    o_c = _gla(gla_qk, gla_v, gla_r, gla_a, p["gla_gate_w"], p["gla_gate_b"], p["gla_norm_w"], batch)
    cmp_w = _cmp_weights(p["nsa_pe_k"], p["nsa_cmp_k_w1"], p["nsa_cmp_k_w2"],
                         p["nsa_pe_v"], p["nsa_cmp_v_w1"], p["nsa_cmp_v_w2"])
    k_cmp, v_cmp = _compress(kv_cmp, cmp_w, batch)
    o_b = _nsa(nsa_q, nsa_gate, k_cmp, v_cmp, k_sel, v_sel, k_win, v_win, tables, batch)
    w_nsa = p["w_branch_nsa"].reshape(NSA_HEADS, NSA_DV, -1)
    w_nsa = jnp.pad(w_nsa, ((0, 0), (0, LANE - NSA_DV), (0, 0))).reshape(NSA_HEADS * LANE, -1)
    x = _merge(x, o_a, o_b, o_c, merge_gate, p["w_branch_gdn"], w_nsa, p["w_branch_gla"],
               p["w_out"], p["mix_norm_post"])
    return _ffn(x, p["ffn2_norm_pre"], p["ffn2_w_gate_up"], p["ffn2_w_down"], p["ffn2_norm_post"])


_LAYER_PARAMS = (
    "ffn1_norm_pre", "ffn1_w_gate_up", "ffn1_w_down", "ffn1_norm_post", "mix_norm_pre", "w_in",
    "gdn_conv_w", "gdn_a_log", "gdn_dt_bias", "gdn_norm_w",
    "nsa_pe_k", "nsa_cmp_k_w1", "nsa_cmp_k_w2", "nsa_pe_v", "nsa_cmp_v_w1", "nsa_cmp_v_w2",
    "gla_gate_w", "gla_gate_b", "gla_norm_w",
    "w_branch_gdn", "w_branch_nsa", "w_branch_gla", "w_out", "mix_norm_post",
    "ffn2_norm_pre", "ffn2_w_gate_up", "ffn2_w_down", "ffn2_norm_post",
)


def kernel(x, rel_bias, ffn1_norm_pre, ffn1_w_gate_up, ffn1_w_down, ffn1_norm_post, mix_norm_pre, w_in, gdn_conv_w, gdn_a_log, gdn_dt_bias, gdn_norm_w, nsa_pe_k, nsa_cmp_k_w1, nsa_cmp_k_w2, nsa_pe_v, nsa_cmp_v_w1, nsa_cmp_v_w2, gla_gate_w, gla_gate_b, gla_norm_w, w_branch_gdn, w_branch_nsa, w_branch_gla, w_out, mix_norm_post, ffn2_norm_pre, ffn2_w_gate_up, ffn2_w_down, ffn2_norm_post):
    stacked = dict(zip(_LAYER_PARAMS, (
        ffn1_norm_pre, ffn1_w_gate_up, ffn1_w_down, ffn1_norm_post, mix_norm_pre, w_in,
        gdn_conv_w, gdn_a_log, gdn_dt_bias, gdn_norm_w,
        nsa_pe_k, nsa_cmp_k_w1, nsa_cmp_k_w2, nsa_pe_v, nsa_cmp_v_w1, nsa_cmp_v_w2,
        gla_gate_w, gla_gate_b, gla_norm_w,
        w_branch_gdn, w_branch_nsa, w_branch_gla, w_out, mix_norm_post,
        ffn2_norm_pre, ffn2_w_gate_up, ffn2_w_down, ffn2_norm_post)))
    batch, seq, d = x.shape
    tables = _bias_tables(rel_bias)
    h = x.reshape(batch * seq, d)
    for layer in range(ffn1_norm_pre.shape[0]):
        h = _layer(h, batch, tables, {name: value[layer] for name, value in stacked.items()})
    return h.reshape(batch, seq, d)
```

```python
import functools
import math

import jax
import jax.numpy as jnp
import numpy as np
from jax import lax
from jax.experimental import pallas as pl
from jax.experimental.pallas import tpu as pltpu

F32 = jnp.float32
BF16 = jnp.bfloat16
HIGHEST = lax.Precision.HIGHEST

LANE = 128
VMEM_LIMIT_BYTES = 56 * 1024 * 1024

NORM_EPS = 1e-6
FFN_RES_SCALE = 0.5
REL_BUCKETS = 32
REL_MAX_DIST = 128
GDN_HEADS, GDN_DK, GDN_DV, GDN_CONV, GDN_CHUNK = 4, 128, 128, 4, 64
NSA_HEADS, NSA_GROUPS, NSA_DK, NSA_DV = 8, 2, 96, 64
NSA_HPG = NSA_HEADS // NSA_GROUPS
CMP_LEN, CMP_STRIDE, CMP_HIDDEN = 32, 16, 256
SEL_LEN, SEL_TOPK, WINDOW, Q_BLOCK = 64, 16, 512, 128
GLA_HEADS, GLA_DK, GLA_DV, GLA_RANK, GLA_TAU, GLA_CHUNK = 4, 64, 128, 16, 16, 16
N_BRANCH = 3
MASK_VALUE = -1e30
L2_EPS = 1e-6

CMP_PER_Q = Q_BLOCK // CMP_STRIDE
CMP_BAND_LO = -10
CMP_BAND = 17
CMP_BAND_PAD = 32


def _nt(a, b, **kw):
    return lax.dot_general(a, b, (((1,), (1,)), ((), ())), preferred_element_type=F32, **kw)


def _tn(a, b, **kw):
    return lax.dot_general(a, b, (((0,), (0,)), ((), ())), preferred_element_type=F32, **kw)


def _mm(a, b, **kw):
    return jnp.dot(a, b, preferred_element_type=F32, **kw)


def _rms(x, w):
    return x * lax.rsqrt(jnp.mean(x * x, axis=-1, keepdims=True) + NORM_EPS) * w


def _silu(x):
    return x * jax.nn.sigmoid(x)


def _softplus(x):
    return jnp.maximum(x, 0.0) + jnp.log1p(jnp.exp(-jnp.abs(x)))


def _params(*semantics):
    return pltpu.CompilerParams(dimension_semantics=semantics, vmem_limit_bytes=VMEM_LIMIT_BYTES)


def _resident(shape):
    return pl.BlockSpec(shape, lambda *_: (0,) * len(shape), pipeline_mode=pl.Buffered(1))


FFN_TOKENS = 512
FFN_CHUNK = 256


def _ffn_kernel(x_ref, npre_ref, wgu_ref, wd_ref, npost_ref, o_ref):
    d_ff = wd_ref.shape[0]
    x = x_ref[...]
    h = _rms(x, npre_ref[...]).astype(BF16)
    acc = jnp.zeros(x.shape, F32)
    for c in range(d_ff // FFN_CHUNK):
        lo = c * FFN_CHUNK
        g = _mm(h, wgu_ref[:, lo:lo + FFN_CHUNK])
        u = _mm(h, wgu_ref[:, d_ff + lo:d_ff + lo + FFN_CHUNK])
        a = (_silu(g) * u).astype(BF16)
        acc = acc + _mm(a, wd_ref[lo:lo + FFN_CHUNK, :])
    o_ref[...] = x + FFN_RES_SCALE * _rms(acc, npost_ref[...])


def _ffn(x, norm_pre, w_gate_up, w_down, norm_post):
    m, d = x.shape
    d_ff = w_down.shape[0]
    row = pl.BlockSpec((FFN_TOKENS, d), lambda i: (i, 0))
    return pl.pallas_call(
        _ffn_kernel,
        grid=(m // FFN_TOKENS,),
        in_specs=[row, _resident((1, d)), _resident((d, 2 * d_ff)), _resident((d_ff, d)), _resident((1, d))],
        out_specs=row,
        out_shape=jax.ShapeDtypeStruct((m, d), F32),
        compiler_params=_params("parallel"),
        name="ffn",
    )(x, norm_pre.reshape(1, d), w_gate_up.astype(BF16), w_down.astype(BF16), norm_post.reshape(1, d))


PROJ_TOKENS = 512
PROJ_CHUNK = 512


def _proj_kernel(x_ref, nw_ref, w_ref, *o_refs, segs):
    h = _rms(x_ref[...], nw_ref[...]).astype(BF16)
    for (off, width, epilogue), o_ref in zip(segs, o_refs):
        for lo in range(0, width, PROJ_CHUNK):
            hi = min(lo + PROJ_CHUNK, width)
            y = _mm(h, w_ref[:, off + lo:off + hi])
            if epilogue == "sigmoid":
                y = jax.nn.sigmoid(y)
            elif epilogue == "ones_hi":
                lane = lax.broadcasted_iota(jnp.int32, y.shape, 1) % LANE
                y = jnp.where(lane >= LANE // 2, 1.0, y)
            elif epilogue is not None:
                y = y * epilogue
            o_ref[:, lo:hi] = y.astype(o_ref.dtype)


def _proj(x, norm_w, pieces):
    m, d = x.shape
    segs, cols, off = [], [], 0
    for w, _, epilogue in pieces:
        width = w.shape[1]
        pad = (-width) % LANE
        segs.append((off, width, epilogue))
        cols.append(w)
        if pad:
            cols.append(jnp.zeros((d, pad), w.dtype))
        off += width + pad
    w_all = jnp.concatenate(cols, axis=1).astype(BF16)
    row = pl.BlockSpec((PROJ_TOKENS, d), lambda i: (i, 0))
    return pl.pallas_call(
        functools.partial(_proj_kernel, segs=tuple(segs)),
        grid=(m // PROJ_TOKENS,),
        in_specs=[row, _resident((1, d)), _resident((d, off))],
        out_specs=[pl.BlockSpec((PROJ_TOKENS, w.shape[1]), lambda i: (i, 0)) for w, _, _ in pieces],
        out_shape=[jax.ShapeDtypeStruct((m, w.shape[1]), dt) for w, dt, _ in pieces],
        compiler_params=_params("parallel"),
        name="in_proj",
    )(x, norm_w.reshape(1, d), w_all)


GDN_TOKENS = 512
CONV_PAD = 8


def _split(x):
    hi = x.astype(BF16)
    return hi, (x - hi.astype(F32)).astype(BF16)


def _gdn_kernel(q_ref, k_ref, v_ref, z_ref, ba_ref, cwq_ref, cwk_ref, cwv_ref, alog_ref, dtb_ref, nw_ref,
                o_ref, xq_ref, xk_ref, xv_ref, gc_ref, cdec_ref,
                q16_ref, k16_ref, kb16_ref, qd_ref, kd_ref, vb_ref, kbe_ref, state_ref):
    t_blk = q_ref.shape[1]
    c_len = GDN_CHUNK
    width = GDN_HEADS * LANE

    @pl.when(pl.program_id(1) == 0)
    def _():
        state_ref[...] = jnp.zeros_like(state_ref)
        for x_ref in (xq_ref, xk_ref, xv_ref):
            x_ref[0:CONV_PAD, :] = jnp.zeros((CONV_PAD, width), F32)

    def conv_silu(raw_ref, x_ref, cw_ref):
        x_ref[CONV_PAD:CONV_PAD + t_blk, :] = raw_ref[0]
        y = jnp.zeros((t_blk, width), F32)
        for tap in range(GDN_CONV):
            y = y + cw_ref[tap:tap + 1, :] * x_ref[pl.ds(CONV_PAD - (GDN_CONV - 1) + tap, t_blk), :]
        x_ref[0:CONV_PAD, :] = x_ref[t_blk:t_blk + CONV_PAD, :]
        return _silu(y)

    q = conv_silu(q_ref, xq_ref, cwq_ref)
    k = conv_silu(k_ref, xk_ref, cwk_ref)
    v = conv_silu(v_ref, xv_ref, cwv_ref)

    ba = ba_ref[0]
    g_all = -jnp.exp(alog_ref[...]) * _softplus(ba + dtb_ref[...])
    src = lax.broadcasted_iota(jnp.int32, (LANE, width), 0)
    dst_head = lax.broadcasted_iota(jnp.int32, (LANE, width), 1) // LANE
    pick_b = (src == dst_head).astype(BF16)
    pick_a = (src == dst_head + GDN_HEADS).astype(BF16)
    b_hi, b_lo = _split(ba)
    g_hi, g_lo = _split(g_all)
    beta = jax.nn.sigmoid(_mm(b_hi, pick_b) + _mm(b_lo, pick_b))
    g_hi, g_lo = _split(_mm(g_hi, pick_a) + _mm(g_lo, pick_a))

    ti = lax.broadcasted_iota(jnp.int32, (t_blk, t_blk), 0)
    tj = lax.broadcasted_iota(jnp.int32, (t_blk, t_blk), 1)
    same = (ti // c_len) == (tj // c_len)
    in_chunk_tri = (same & (tj <= ti)).astype(BF16)
    in_chunk = same.astype(BF16)
    gc_all = _mm(in_chunk_tri, g_hi) + _mm(in_chunk_tri, g_lo)
    gl_all = _mm(in_chunk, g_hi) + _mm(in_chunk, g_lo)
    gc_ref[...] = gc_all
    cdec_ref[...] = jnp.exp(gl_all)
    e_gc = jnp.exp(gc_all)
    e_rest = jnp.exp(gl_all - gc_all)
    for h in range(GDN_HEADS):
        cols = slice(h * LANE, (h + 1) * LANE)
        qh, kh = q[:, cols], k[:, cols]
        qh = qh * lax.rsqrt(jnp.sum(qh * qh, axis=-1, keepdims=True) + L2_EPS) * (GDN_DK ** -0.5)
        kh = kh * lax.rsqrt(jnp.sum(kh * kh, axis=-1, keepdims=True) + L2_EPS)
        kb = kh * beta[:, cols]
        q16_ref[:, cols] = qh.astype(BF16)
        k16_ref[:, cols] = kh.astype(BF16)
        kb16_ref[:, cols] = kb.astype(BF16)
        qd_ref[:, cols] = (qh * e_gc[:, cols]).astype(BF16)
        kd_ref[:, cols] = (kh * e_rest[:, cols]).astype(BF16)
        vb_ref[:, cols] = (v[:, cols] * beta[:, cols]).astype(BF16)
        kbe_ref[:, cols] = (kb * e_gc[:, cols]).astype(BF16)

    ci = lax.broadcasted_iota(jnp.int32, (c_len, c_len), 0)
    cj = lax.broadcasted_iota(jnp.int32, (c_len, c_len), 1)
    lower = ci >= cj
    strict = ci > cj
    eye = (ci == cj).astype(F32)
    lane0 = (lax.broadcasted_iota(jnp.int32, (c_len, LANE), 1) == 0).astype(BF16)
    nw = nw_ref[...]

    def chunk_body(c, states):
        start = pl.multiple_of(c * c_len, c_len)
        rows = pl.ds(start, c_len)
        heads = range(GDN_HEADS)
        cols = [slice(h * LANE, (h + 1) * LANE) for h in heads]
        starts = [pl.multiple_of((2 * c + j) * c_len, c_len) for j in range(2)]
        rows2 = [pl.ds(s, c_len) for s in starts]
        items = [(j, h) for j in range(2) for h in heads]
        n_it = len(items)
        gc = [gc_ref[rows2[j], cols[h]] for j, h in items]
        gc_split = [_split(x) for x in gc]
        gc_row = [_nt(lane0, hi) + _nt(lane0, lo) for hi, lo in gc_split]
        decay = [jnp.exp(jnp.where(lower, gc[i][:, :c_len] - gc_row[i], MASK_VALUE)) for i in range(n_it)]
        k16 = [k16_ref[rows2[j], cols[h]] for j, h in items]
        kk = [_nt(kb16_ref[rows2[j], cols[h]], k16[i]) for i, (j, h) in enumerate(items)]
        pw = [jnp.where(strict, -(kk[i] * decay[i]), 0.0) for i in range(n_it)]
        inv = [eye + p for p in pw]
        for _ in range(5):
            p16 = [p.astype(BF16) for p in pw]
            pw = [_mm(p, p) for p in p16]
            inv = [inv[i] + _mm(inv[i].astype(BF16), pw[i].astype(BF16)) for i in range(n_it)]
        inv16 = [x.astype(BF16) for x in inv]
        u = [_mm(inv16[i], vb_ref[rows2[j], cols[h]]) for i, (j, h) in enumerate(items)]
        w = [_mm(inv16[i], kbe_ref[rows2[j], cols[h]]).astype(BF16) for i, (j, h) in enumerate(items)]
        attn = [(_nt(q16_ref[rows2[j], cols[h]], k16[i]) * decay[i]).astype(BF16) for i, (j, h) in enumerate(items)]
        cur = [states[h] for h in heads]
        for j in range(2):
            at = lambda h: j * GDN_HEADS + h
            s16 = [cur[h].astype(BF16) for h in heads]
            v16 = [(u[at(h)] - _mm(w[at(h)], s16[h])).astype(BF16) for h in heads]
            o = [_mm(qd_ref[rows2[j], cols[h]], s16[h]) + _mm(attn[at(h)], v16[h]) for h in heads]
            cur = [cur[h] * cdec_ref[pl.ds(starts[j], 1), cols[h]] + _tn(kd_ref[rows2[j], cols[h]], v16[h])
                   for h in heads]
            o_ref[0, rows2[j], :] = (jnp.concatenate([_rms(x, nw) for x in o], axis=-1)
                                     * _silu(z_ref[0, rows2[j], :]))
        return jnp.stack(cur)

    state_ref[...] = lax.fori_loop(0, t_blk // (2 * c_len), chunk_body, state_ref[...])


def _gdn(qkv, z, ba, conv_w, a_log, dt_bias, norm_w, batch):
    m = qkv.shape[0]
    s = m // batch
    h, dk, dv = GDN_HEADS, GDN_DK, GDN_DV
    width = h * LANE
    qkv = qkv.reshape(batch, s, 3 * width)
    tok = lambda col: pl.BlockSpec((1, GDN_TOKENS, width), lambda b, i, col=col: (b, i, col))
    cw = lambda col: pl.BlockSpec((GDN_CONV, width), lambda b, i, col=col: (0, col))
    lane_pad = jnp.zeros((LANE - 2 * h,), F32)
    alog_row = jnp.concatenate([jnp.zeros((h,), F32), a_log, lane_pad]).reshape(1, LANE)
    dtb_row = jnp.concatenate([jnp.zeros((h,), F32), dt_bias, lane_pad]).reshape(1, LANE)
    out = pl.pallas_call(
        _gdn_kernel,
        grid=(batch, s // GDN_TOKENS),
        in_specs=[tok(0), tok(1), tok(2), tok(0),
                  pl.BlockSpec((1, GDN_TOKENS, LANE), lambda b, i: (b, i, 0)),
                  cw(0), cw(1), cw(2),
                  _resident((1, LANE)), _resident((1, LANE)), _resident((1, dv))],
        out_specs=tok(0),
        out_shape=jax.ShapeDtypeStruct((batch, s, width), F32),
        scratch_shapes=[pltpu.VMEM((GDN_TOKENS + CONV_PAD, width), F32)] * 3
        + [pltpu.VMEM((GDN_TOKENS, width), F32)] * 2 + [pltpu.VMEM((GDN_TOKENS, width), BF16)] * 7
        + [pltpu.VMEM((h, dk, dv), F32)],
        compiler_params=_params("parallel", "arbitrary"),
        name="gdn",
    )(qkv, qkv, qkv, z.reshape(batch, s, width), ba.reshape(batch, s, LANE),
      conv_w, conv_w, conv_w, alog_row, dtb_row, norm_w.reshape(1, dv))
    return out.reshape(m, width)


GLA_TOKENS = 256


def _log_sigmoid(x):
    return jnp.minimum(x, 0.0) - jnp.log1p(jnp.exp(-jnp.abs(x)))


def _gla_kernel(qk_ref, v_ref, r_ref, a_ref, gw_ref, gb_ref, nw_ref, o_ref, ks_ref, bs_ref, vs_ref, state_ref):
    t_blk = qk_ref.shape[1]
    c_len = GLA_CHUNK
    hdk = GLA_HEADS * GLA_DK
    hdv = GLA_HEADS * GLA_DV

    @pl.when(pl.program_id(1) == 0)
    def _():
        state_ref[...] = jnp.zeros_like(state_ref)
        ks_ref[0:c_len, :] = jnp.zeros((c_len, hdk), F32)
        bs_ref[0:c_len, :] = jnp.zeros((c_len, hdk), F32)
        vs_ref[0:c_len, :] = jnp.zeros((c_len, hdv), F32)

    qk = qk_ref[0]
    q = qk[:, :hdk] * (GLA_DK ** -0.5)
    k = qk[:, hdk:]
    v = v_ref[0]
    log_a = _log_sigmoid(_mm(a_ref[0].astype(BF16), gw_ref[...]) + gb_ref[...]) * (1.0 / GLA_TAU)

    ti = lax.broadcasted_iota(jnp.int32, (t_blk, t_blk), 0)
    tj = lax.broadcasted_iota(jnp.int32, (t_blk, t_blk), 1)
    same = (ti // c_len) == (tj // c_len)
    cum = _mm((same & (tj <= ti)).astype(F32), log_a, precision=HIGHEST)
    tot = _mm(same.astype(F32), log_a, precision=HIGHEST)

    ks_ref[c_len:, :] = k
    bs_ref[c_len:, :] = cum
    vs_ref[c_len:, :] = v
    pos = lax.broadcasted_iota(jnp.int32, (t_blk, 1), 0) % c_len
    hd = lax.broadcasted_iota(jnp.int32, (hdk, hdv), 0) // GLA_DK
    he = lax.broadcasted_iota(jnp.int32, (hdk, hdv), 1) // GLA_DV
    spread = (hd == he).astype(BF16)
    o = jnp.zeros((t_blk, hdv), F32)
    for off in range(c_len):
        k_o = ks_ref[pl.ds(c_len - off, t_blk), :]
        b_o = bs_ref[pl.ds(c_len - off, t_blk), :]
        v_o = vs_ref[pl.ds(c_len - off, t_blk), :]
        w = jnp.exp(jnp.where(pos >= off, cum - b_o, MASK_VALUE))
        o = o + _mm((q * k_o * w).astype(BF16), spread) * v_o

    q_dec = (q * jnp.exp(cum)).astype(BF16)
    k_dec = (k * jnp.exp(tot - cum)).astype(BF16)
    c_dec = jnp.exp(tot)
    v16 = v.astype(BF16)
    states = [state_ref[h] for h in range(GLA_HEADS)]
    inter = []
    for n in range(t_blk // c_len):
        rows = slice(n * c_len, (n + 1) * c_len)
        parts = []
        for h in range(GLA_HEADS):
            dks = slice(h * GLA_DK, (h + 1) * GLA_DK)
            dvs = slice(h * GLA_DV, (h + 1) * GLA_DV)
            parts.append(_nt(q_dec[rows, dks], states[h].astype(BF16)))
            states[h] = states[h] * c_dec[n * c_len:n * c_len + 1, dks] + _tn(v16[rows, dvs], k_dec[rows, dks])
        inter.append(jnp.concatenate(parts, axis=-1))
    for h in range(GLA_HEADS):
        state_ref[h] = states[h]
    o = o + jnp.concatenate(inter, axis=0)

    r = r_ref[0]
    nw = nw_ref[...]
    for h in range(GLA_HEADS):
        dvs = slice(h * GLA_DV, (h + 1) * GLA_DV)
        o_ref[0, :, dvs] = _rms(o[:, dvs], nw) * _silu(r[:, dvs])


def _gla(qk, v, r, a_low, gate_w, gate_b, norm_w, batch):
    m = qk.shape[0]
    s = m // batch
    hdk, hdv = GLA_HEADS * GLA_DK, GLA_HEADS * GLA_DV
    gw = jnp.pad(gate_w, ((0, LANE - GLA_RANK), (0, 0))).astype(BF16)
    blk = lambda w: pl.BlockSpec((1, GLA_TOKENS, w), lambda b, i: (b, i, 0))
    out = pl.pallas_call(
        _gla_kernel,
        grid=(batch, s // GLA_TOKENS),
        in_specs=[blk(2 * hdk), blk(hdv), blk(hdv), blk(LANE),
                  _resident((LANE, hdk)), _resident((1, hdk)), _resident((1, GLA_DV))],
        out_specs=blk(hdv),
        out_shape=jax.ShapeDtypeStruct((batch, s, hdv), F32),
        scratch_shapes=[pltpu.VMEM((GLA_TOKENS + GLA_CHUNK, hdk), F32),
                        pltpu.VMEM((GLA_TOKENS + GLA_CHUNK, hdk), F32),
                        pltpu.VMEM((GLA_TOKENS + GLA_CHUNK, hdv), F32),
                        pltpu.VMEM((GLA_HEADS, GLA_DV, GLA_DK), F32)],
        compiler_params=_params("parallel", "arbitrary"),
        name="gla",
    )(qk.reshape(batch, s, 2 * hdk), v.reshape(batch, s, hdv), r.reshape(batch, s, hdv),
      a_low.reshape(batch, s, LANE), gw, gate_b.reshape(1, hdk), norm_w.reshape(1, GLA_DV))
    return out.reshape(m, hdv)


def _gelu_tanh(x):
    return 0.5 * x * (1.0 + jnp.tanh(math.sqrt(2.0 / math.pi) * (x + 0.044715 * x * x * x)))


def _cmp_kernel(seg_ref, wa_ref, wb_ref, pek_ref, w1k_ref, pev_ref, w1v_ref, w2_ref, k_ref, v_ref):
    seg = seg_ref[0]
    n_seg = seg.shape[0]
    first = _mm(seg, wa_ref[...])
    second = _mm(seg, wb_ref[...])
    bias_k = _mm(pek_ref[...], w1k_ref[...])[0:1]
    bias_v = _mm(pev_ref[...], w1v_ref[...])[0:1]
    bias = jnp.concatenate([bias_k, bias_v] * NSA_GROUPS, axis=-1)
    hid = _gelu_tanh(first + pltpu.roll(second, n_seg - 1, 0) + bias).astype(BF16)
    out = _mm(hid, w2_ref[...])
    kw = NSA_GROUPS * LANE
    k_ref[0] = out[:, :kw].astype(BF16)
    v_ref[0] = out[:, kw:].astype(BF16)


def _cmp_weights(pe_k, w1k, w2k, pe_v, w1v, w2v):
    g, dk, dv, hid = NSA_GROUPS, NSA_DK, NSA_DV, CMP_HIDDEN
    w1k = w1k.reshape(CMP_LEN, dk, hid)
    w1v = w1v.reshape(CMP_LEN, dv, hid)

    same_group = np.eye(g, dtype=np.float32)

    def half(lo):
        top = jnp.concatenate([w1k[lo:lo + CMP_STRIDE], jnp.zeros((CMP_STRIDE, dk, hid), F32)], axis=-1)
        bot = jnp.concatenate([jnp.zeros((CMP_STRIDE, dv, hid), F32), w1v[lo:lo + CMP_STRIDE]], axis=-1)
        per_group = jnp.concatenate([top, bot], axis=1)
        blk = per_group[:, None, :, None, :] * same_group[None, :, None, :, None]
        return blk.reshape(CMP_STRIDE * g * (dk + dv), g * 2 * hid).astype(BF16)

    padded = jnp.stack([jnp.pad(w2k, ((0, 0), (0, LANE - dk))), jnp.pad(w2v, ((0, 0), (0, LANE - dv)))])
    place = np.zeros((g, 2, 2 * g), np.float32)
    for gg in range(g):
        place[gg, 0, gg] = 1.0
        place[gg, 1, g + gg] = 1.0
    w2 = (padded[None, :, :, None, :] * place[:, :, None, :, None]).reshape(g * 2 * hid, 2 * g * LANE).astype(BF16)
    pad8 = lambda pe: jnp.pad(pe.reshape(1, -1), ((0, 7), (0, 0))).astype(BF16)
    return half(0), half(CMP_STRIDE), pad8(pe_k), w1k.reshape(-1, hid).astype(BF16), pad8(pe_v), \
        w1v.reshape(-1, hid).astype(BF16), w2


def _compress(kv_cmp, weights, batch):
    m, width = kv_cmp.shape
    s = m // batch
    n_seg = s // CMP_STRIDE
    seg = kv_cmp.reshape(batch, n_seg, CMP_STRIDE * width)
    wa, wb, pek, w1k, pev, w1v, w2 = weights
    full = lambda a: _resident(a.shape)
    kw = vw = NSA_GROUPS * LANE
    return pl.pallas_call(
        _cmp_kernel,
        grid=(batch,),
        in_specs=[pl.BlockSpec((1, n_seg, CMP_STRIDE * width), lambda b: (b, 0, 0)),
                  full(wa), full(wb), full(pek), full(w1k), full(pev), full(w1v), full(w2)],
        out_specs=[pl.BlockSpec((1, n_seg, kw), lambda b: (b, 0, 0)),
                   pl.BlockSpec((1, n_seg, vw), lambda b: (b, 0, 0))],
        out_shape=[jax.ShapeDtypeStruct((batch, n_seg, kw), BF16), jax.ShapeDtypeStruct((batch, n_seg, vw), BF16)],
        compiler_params=_params("parallel"),
        name="nsa_compress",
    )(seg, wa, wb, pek, w1k, pev, w1v, w2)


NSA_KEY_CHUNK = 512


def _online_step(carry, s, v):
    m_old, acc = carry
    m_new = jnp.maximum(m_old, jnp.max(s, axis=-1, keepdims=True))
    p = jnp.exp(s - m_new).astype(BF16)
    return m_new, jnp.exp(m_old - m_new) * acc + _mm(p, v)


def _normalised(acc):
    low = lax.broadcasted_iota(jnp.int32, acc.shape, 1) < NSA_DV
    return jnp.where(low, acc / pltpu.roll(acc, NSA_DV, 1), 0.0)


def _nsa_kernel(far_ref, q_ref, gate_ref, kc_ref, vc_ref, ks_ref, vs_ref, eb_ref, kw_ref, vw_ref, gcb_ref, bpd_ref,
                o_ref, *, n_sel, k_top):
    qb = pl.program_id(1)
    tq = Q_BLOCK
    rows = NSA_HPG * tq
    n_cmp_pad = kc_ref.shape[1]
    q0 = qb * tq
    qi = lax.broadcasted_iota(jnp.int32, (rows, 1), 0) % tq
    row_t = q0 + qi

    cmp_n = lax.broadcasted_iota(jnp.int32, (rows, n_cmp_pad), 1)
    valid_c = row_t >= cmp_n * CMP_STRIDE + (CMP_LEN - 1)
    any_c = (row_t >= CMP_LEN - 1).astype(F32)
    band_n = lax.broadcasted_iota(jnp.int32, (n_cmp_pad, LANE), 0)
    band_l = lax.broadcasted_iota(jnp.int32, (n_cmp_pad, LANE), 1)
    band_m = band_l % CMP_BAND_PAD
    band_keys = ((band_l < 2 * CMP_BAND_PAD)
                 & ((band_n - CMP_PER_Q * qb == band_m + CMP_BAND_LO) | (band_m == CMP_BAND_PAD - 1))).astype(BF16)
    ov_s = lax.broadcasted_iota(jnp.int32, (n_sel, n_cmp_pad), 0) * SEL_LEN
    ov_c = lax.broadcasted_iota(jnp.int32, (n_sel, n_cmp_pad), 1) * CMP_STRIDE
    overlap_t = ((ov_c < ov_s + SEL_LEN) & (ov_c + CMP_LEN > ov_s)).astype(BF16)

    sel_s = lax.broadcasted_iota(jnp.int32, (n_sel, tq), 0)
    blk_t = (q0 + lax.broadcasted_iota(jnp.int32, (n_sel, tq), 1)) // SEL_LEN
    forced = (sel_s == 0) | (sel_s == blk_t) | (sel_s == blk_t - 1)
    future = sel_s > blk_t

    q_all = q_ref[0]
    gates = gate_ref[0]
    prev_base = pl.multiple_of(jnp.maximum(qb - 1, 0) * tq, tq)
    diag_base = pl.multiple_of(q0, tq)

    def add_far(s, heads):
        return jnp.concatenate([s[i * tq:(i + 1) * tq] + far_ref[h] for i, h in enumerate(heads)], axis=0)

    groups = range(NSA_GROUPS)
    heads = [range(g * NSA_HPG, (g + 1) * NSA_HPG) for g in groups]
    cols = [slice(g * LANE, (g + 1) * LANE) for g in groups]
    q4 = [jnp.concatenate([q_all[:, h * LANE:(h + 1) * LANE] for h in heads[g]], axis=0) for g in groups]

    s = [_nt(jnp.concatenate([q4[g], gcb_ref[g]], axis=1),
             jnp.concatenate([kc_ref[0, :, cols[g]], band_keys], axis=1)) for g in groups]
    s = [jnp.where(valid_c, x, MASK_VALUE) for x in s]
    p = [jnp.exp(x - jnp.max(x, axis=-1, keepdims=True)) for x in s]
    p = [x * (any_c / jnp.sum(x, axis=-1, keepdims=True)) for x in p]
    o_cmp = [_mm(p[g].astype(BF16), vc_ref[0, :, cols[g]]) for g in groups]
    p_sum = [sum(x[i * tq:(i + 1) * tq] for i in range(1, NSA_HPG)) + x[0:tq] for x in p]

    p_hi = [x.astype(BF16) for x in p_sum]
    p_lo = [(p_sum[g] - p_hi[g].astype(F32)).astype(BF16) for g in groups]
    p_lo2 = [(p_sum[g] - p_hi[g].astype(F32) - p_lo[g].astype(F32)).astype(BF16) for g in groups]
    imp = [_nt(overlap_t, p_hi[g]) + (_nt(overlap_t, p_lo[g]) + _nt(overlap_t, p_lo2[g]))
           for g in groups]
    imp = [jnp.where(forced, jnp.inf, jnp.where(future, -jnp.inf, x)) for x in imp]
    chosen = [jnp.zeros((n_sel, tq), F32) for _ in groups]
    for _ in range(k_top):
        best = [jnp.max(x, axis=0, keepdims=True) for x in imp]
        first = [jnp.min(jnp.where(imp[g] == best[g], sel_s, n_sel), axis=0, keepdims=True) for g in groups]
        pick = [sel_s == x for x in first]
        chosen = [jnp.where(pick[g], 1.0, chosen[g]) for g in groups]
        imp = [jnp.where(pick[g], -jnp.inf, imp[g]) for g in groups]
    skipped = [1.0 - x for x in chosen]
    older = sel_s < (tq // SEL_LEN) * (qb - 1)
    skipped_far = [jnp.where(older, x, 1.0) for x in skipped]
    q_far = [jnp.concatenate([q4[g], jnp.concatenate([skipped_far[g].T.astype(BF16)] * NSA_HPG, axis=0)], axis=1)
             for g in groups]
    q_near = [jnp.concatenate([q4[g], jnp.concatenate([skipped[g].T.astype(BF16)] * NSA_HPG, axis=0)], axis=1)
              for g in groups]

    def far_logits(g, base):
        k_aug = jnp.concatenate([ks_ref[0, pl.ds(base, NSA_KEY_CHUNK), cols[g]],
                                 eb_ref[pl.ds(base, NSA_KEY_CHUNK), :]], axis=1)
        return add_far(_nt(q_far[g], k_aug), heads[g])

    def far_body(c, carry):
        base0 = pl.multiple_of(c * (2 * NSA_KEY_CHUNK), NSA_KEY_CHUNK)
        base1 = pl.multiple_of(base0 + NSA_KEY_CHUNK, NSA_KEY_CHUNK)
        s0 = [far_logits(g, base0) for g in groups]
        s1 = [far_logits(g, base1) for g in groups]
        carry = [_online_step(carry[g], s0[g], vs_ref[0, pl.ds(base0, NSA_KEY_CHUNK), cols[g]]) for g in groups]
        carry = [_online_step(carry[g], s1[g], vs_ref[0, pl.ds(base1, NSA_KEY_CHUNK), cols[g]]) for g in groups]
        return tuple(carry)

    n_far = (jnp.maximum(qb - 1, 0) * tq + 2 * NSA_KEY_CHUNK - 1) // (2 * NSA_KEY_CHUNK)
    carry = lax.fori_loop(0, n_far, far_body,
                          tuple((jnp.full((rows, 1), MASK_VALUE, F32), jnp.zeros((rows, LANE), F32)) for _ in groups))

    e_near = jnp.concatenate([eb_ref[pl.ds(prev_base, tq), :], eb_ref[pl.ds(diag_base, tq), :]], axis=0)
    k_aug = [jnp.concatenate(
        [jnp.concatenate([ks_ref[0, pl.ds(prev_base, tq), cols[g]], ks_ref[0, pl.ds(diag_base, tq), cols[g]]], axis=0),
         e_near], axis=1) for g in groups]
    v_near = [jnp.concatenate([vs_ref[0, pl.ds(prev_base, tq), cols[g]], vs_ref[0, pl.ds(diag_base, tq), cols[g]]],
                              axis=0) for g in groups]
    key = lax.broadcasted_iota(jnp.int32, (rows, 2 * tq), 1)
    first_key = jnp.where(qb >= 1, 0, tq)
    valid = (key >= first_key) & (key <= qi + tq)
    s = [jnp.where(valid, _nt(q_near[g], k_aug[g]) + bpd_ref[g], MASK_VALUE) for g in groups]
    o_sel = [_normalised(_online_step(carry[g], s[g], v_near[g])[1]) for g in groups]

    n_win = WINDOW + tq
    s = [_nt(q4[g], kw_ref[0, pl.ds(diag_base, n_win), cols[g]]) for g in groups]
    s = [jnp.concatenate([add_far(s[g][:, :n_win - 2 * tq], heads[g]), s[g][:, n_win - 2 * tq:] + bpd_ref[g]], axis=1)
         for g in groups]
    key = lax.broadcasted_iota(jnp.int32, (rows, n_win), 1)
    valid = (key > qi) & (key <= qi + WINDOW) & (key >= WINDOW - q0)
    s = [jnp.where(valid, x, MASK_VALUE) for x in s]
    p = [jnp.exp(x - jnp.max(x, axis=-1, keepdims=True)).astype(BF16) for x in s]
    o_win = [_normalised(_mm(p[g], vw_ref[0, pl.ds(diag_base, n_win), cols[g]])) for g in groups]

    outs = []
    for g in groups:
        for i, h in enumerate(heads[g]):
            r = slice(i * tq, (i + 1) * tq)
            c0 = h * N_BRANCH
            outs.append(gates[:, c0:c0 + 1] * o_cmp[g][r]
                        + gates[:, c0 + 1:c0 + 2] * o_sel[g][r]
                        + gates[:, c0 + 2:c0 + 3] * o_win[g][r])
    o_ref[0] = jnp.concatenate(outs, axis=1)


def _rel_bucket(dist):
    n = np.maximum(dist, 0)
    max_exact = REL_BUCKETS // 2
    nf = np.maximum(n, 1).astype(np.float32)
    large = max_exact + (np.log(nf / np.float32(max_exact)) / np.float32(math.log(REL_MAX_DIST / max_exact))
                         * np.float32(REL_BUCKETS - max_exact)).astype(np.int32)
    large = np.minimum(large, REL_BUCKETS - 1)
    return np.where(n < max_exact, n, large)


def _bias_tables(rel_bias):
    tbl = rel_bias.astype(F32).T

    def lookup(dist):
        onehot = np.eye(REL_BUCKETS, dtype=np.float32)[_rel_bucket(dist)]
        return jnp.einsum("hb,ijb->hij", tbl, onehot, precision=HIGHEST)

    i = np.arange(Q_BLOCK)
    dist = i[:, None] - i[None, :]
    diag = lookup(dist)
    prev = lookup(dist + Q_BLOCK)
    far = tbl[:, REL_BUCKETS - 1]
    m = np.arange(CMP_BAND_PAD)
    d_c = i[:, None] - CMP_STRIDE * (m[None, :] + CMP_BAND_LO) - (CMP_LEN - 1)
    band = lookup(d_c) - far[:, None, None]
    last = (m == CMP_BAND_PAD - 1)[None, None, :]
    band = jnp.where(last, far[:, None, None], jnp.where((m < CMP_BAND)[None, None, :], band, 0.0))
    band = band.reshape(NSA_GROUPS, NSA_HPG * Q_BLOCK, CMP_BAND_PAD)
    band_hi = band.astype(BF16)
    band_lo = (band - band_hi.astype(F32)).astype(BF16)
    band = jnp.pad(jnp.concatenate([band_hi, band_lo], axis=-1), ((0, 0), (0, 0), (0, LANE - 2 * CMP_BAND_PAD)))
    near = jnp.concatenate([prev, diag], axis=-1).reshape(NSA_GROUPS, NSA_HPG * Q_BLOCK, 2 * Q_BLOCK)
    return far, band, near


def _nsa(q, gates, k_cmp, v_cmp, k_sel, v_sel, k_win, v_win, tables, batch):
    m = q.shape[0]
    s = m // batch
    assert s % (2 * NSA_KEY_CHUNK) == 0
    n_sel = s // SEL_LEN
    k_top = min(SEL_TOPK, n_sel)
    hw = NSA_HEADS * LANE
    kvw = NSA_GROUPS * LANE
    far, band, near = tables
    n_seg = k_cmp.shape[1]
    blk_of_key = jnp.arange(s)[:, None] // SEL_LEN == jnp.arange(n_sel)[None, :]
    key_blk = jnp.where(blk_of_key, MASK_VALUE, 0.0).astype(BF16)
    front = ((0, 0), (WINDOW, 0), (0, 0))
    k_win = jnp.pad(k_win.reshape(batch, s, kvw), front)
    v_win = jnp.pad(v_win.reshape(batch, s, kvw), front)
    per_batch = lambda n: pl.BlockSpec((1, n, kvw), lambda b, i: (b, 0, 0), pipeline_mode=pl.Buffered(1))
    tok = lambda w: pl.BlockSpec((1, Q_BLOCK, w), lambda b, i: (b, i, 0))
    out = pl.pallas_call(
        functools.partial(_nsa_kernel, n_sel=n_sel, k_top=k_top),
        grid=(batch, s // Q_BLOCK),
        in_specs=[pl.BlockSpec(memory_space=pltpu.SMEM),
                  tok(hw), tok(LANE),
                  per_batch(n_seg), per_batch(n_seg),
                  per_batch(s), per_batch(s), _resident(key_blk.shape), per_batch(s + WINDOW), per_batch(s + WINDOW),
                  _resident(band.shape), _resident(near.shape)],
        out_specs=tok(hw),
        out_shape=jax.ShapeDtypeStruct((batch, s, hw), F32),
        compiler_params=_params("parallel", "arbitrary"),
        name="nsa_attend",
    )(far, q.reshape(batch, s, hw), gates.reshape(batch, s, LANE), k_cmp, v_cmp,
      k_sel.reshape(batch, s, kvw), v_sel.reshape(batch, s, kvw), key_blk, k_win, v_win, band, near)
    return out.reshape(m, hw)


MERGE_TOKENS = 512


def _merge_kernel(x_ref, oa_ref, ob_ref, oc_ref, gate_ref, wa_ref, wb_ref, wc_ref, wo_ref, nw_ref, o_ref):
    d = x_ref.shape[1]
    merged = (gate_ref[:, 0:d] * _mm(oa_ref[...].astype(BF16), wa_ref[...])
              + gate_ref[:, d:2 * d] * _mm(ob_ref[...].astype(BF16), wb_ref[...])
              + gate_ref[:, 2 * d:3 * d] * _mm(oc_ref[...].astype(BF16), wc_ref[...]))
    y = _mm(merged.astype(BF16), wo_ref[...])
    o_ref[...] = x_ref[...] + _rms(y, nw_ref[...])


def _merge(x, o_a, o_b, o_c, gates, w_a, w_b, w_c, w_out, norm_post):
    m, d = x.shape
    row = lambda w: pl.BlockSpec((MERGE_TOKENS, w), lambda i: (i, 0))
    full = lambda a: _resident(a.shape)
    ws = [w.astype(BF16) for w in (w_a, w_b, w_c, w_out)]
    return pl.pallas_call(
        _merge_kernel,
        grid=(m // MERGE_TOKENS,),
        in_specs=[row(d), row(o_a.shape[1]), row(o_b.shape[1]), row(o_c.shape[1]), row(N_BRANCH * d),
                  full(ws[0]), full(ws[1]), full(ws[2]), full(ws[3]), _resident((1, d))],
        out_specs=row(d),
        out_shape=jax.ShapeDtypeStruct((m, d), F32),
        compiler_params=_params("parallel"),
        name="merge_out",
    )(x, o_a, o_b, o_c, gates, *ws, norm_post.reshape(1, d))


_IN_COLUMNS = (
    ("gdn_q", GDN_HEADS * GDN_DK), ("gdn_k", GDN_HEADS * GDN_DK), ("gdn_v", GDN_HEADS * GDN_DV),
    ("gdn_z", GDN_HEADS * GDN_DV), ("gdn_b", GDN_HEADS), ("gdn_a", GDN_HEADS),
    ("nsa_q", NSA_HEADS * NSA_DK), ("nsa_kv_cmp", NSA_GROUPS * (NSA_DK + NSA_DV)),
    ("nsa_kv_sel", NSA_GROUPS * (NSA_DK + NSA_DV)), ("nsa_kv_win", NSA_GROUPS * (NSA_DK + NSA_DV)),
    ("nsa_gate", 3 * NSA_HEADS),
    ("gla_q", GLA_HEADS * GLA_DK), ("gla_k", GLA_HEADS * GLA_DK), ("gla_v", GLA_HEADS * GLA_DV),
    ("gla_r", GLA_HEADS * GLA_DV), ("gla_a", GLA_RANK), ("merge_gate", None),
)


def _split_w_in(w_in):
    out, off = {}, 0
    for name, width in _IN_COLUMNS:
        width = w_in.shape[1] - off if width is None else width
        out[name] = w_in[:, off:off + width]
        off += width
    return out


def _pad_cols(w, width):
    return jnp.pad(w, ((0, 0), (0, width - w.shape[1])))


def _kv_layout(w):
    d = w.shape[0]
    w = w.reshape(d, NSA_GROUPS, NSA_DK + NSA_DV)
    k = jnp.pad(w[:, :, :NSA_DK], ((0, 0), (0, 0), (0, LANE - NSA_DK))).reshape(d, NSA_GROUPS * LANE)
    v = jnp.pad(w[:, :, NSA_DK:], ((0, 0), (0, 0), (0, LANE - NSA_DV))).reshape(d, NSA_GROUPS * LANE)
    return k, v


def _mix_pieces(w_in):
    d = w_in.shape[0]
    c = _split_w_in(w_in)
    gdn = [
        (jnp.concatenate([c["gdn_q"], c["gdn_k"], c["gdn_v"]], axis=1), F32, None),
        (c["gdn_z"], F32, None),
        (_pad_cols(jnp.concatenate([c["gdn_b"], c["gdn_a"]], axis=1), LANE), F32, None),
    ]
    gla = [
        (jnp.concatenate([c["gla_q"], c["gla_k"]], axis=1), F32, None),
        (c["gla_v"], F32, None),
        (c["gla_r"], F32, None),
        (_pad_cols(c["gla_a"], LANE), F32, None),
    ]
    q = jnp.pad(c["nsa_q"].reshape(d, NSA_HEADS, NSA_DK), ((0, 0), (0, 0), (0, LANE - NSA_DK)))
    k_sel, v_sel = _kv_layout(c["nsa_kv_sel"])
    k_win, v_win = _kv_layout(c["nsa_kv_win"])
    gate = _pad_cols(c["nsa_gate"], LANE)
    nsa = [
        (q.reshape(d, NSA_HEADS * LANE), BF16, NSA_DK ** -0.5),
        (c["nsa_kv_cmp"], BF16, None),
        (k_sel, BF16, None), (v_sel, BF16, "ones_hi"), (k_win, BF16, None), (v_win, BF16, "ones_hi"),
        (gate, F32, "sigmoid"),
        (c["merge_gate"], BF16, "sigmoid"),
    ]
    return gdn + gla, nsa


def _layer(x, batch, tables, p):
    x = _ffn(x, p["ffn1_norm_pre"], p["ffn1_w_gate_up"], p["ffn1_w_down"], p["ffn1_norm_post"])
    rec_pieces, nsa_pieces = _mix_pieces(p["w_in"])
    qkv, z, ba, gla_qk, gla_v, gla_r, gla_a = _proj(x, p["mix_norm_pre"], rec_pieces)
    nsa_q, kv_cmp, k_sel, v_sel, k_win, v_win, nsa_gate, merge_gate = _proj(x, p["mix_norm_pre"], nsa_pieces)
    o_a = _gdn(qkv, z, ba, p["gdn_conv_w"], p["gdn_a_log"], p["gdn_dt_bias"], p["gdn_norm_w"], batch)
    o_c = _gla(gla_qk, gla_v, gla_r, gla_a, p["gla_gate_w"], p["gla_gate_b"], p["gla_norm_w"], batch)
    cmp_w = _cmp_weights(p["nsa_pe_k"], p["nsa_cmp_k_w1"], p["nsa_cmp_k_w2"],
                         p["nsa_pe_v"], p["nsa_cmp_v_w1"], p["nsa_cmp_v_w2"])
    k_cmp, v_cmp = _compress(kv_cmp, cmp_w, batch)
    o_b = _nsa(nsa_q, nsa_gate, k_cmp, v_cmp, k_sel, v_sel, k_win, v_win, tables, batch)
    w_nsa = p["w_branch_nsa"].reshape(NSA_HEADS, NSA_DV, -1)
    w_nsa = jnp.pad(w_nsa, ((0, 0), (0, LANE - NSA_DV), (0, 0))).reshape(NSA_HEADS * LANE, -1)
    x = _merge(x, o_a, o_b, o_c, merge_gate, p["w_branch_gdn"], w_nsa, p["w_branch_gla"],
               p["w_out"], p["mix_norm_post"])
    return _ffn(x, p["ffn2_norm_pre"], p["ffn2_w_gate_up"], p["ffn2_w_down"], p["ffn2_norm_post"])


_LAYER_PARAMS = (
    "ffn1_norm_pre", "ffn1_w_gate_up", "ffn1_w_down", "ffn1_norm_post", "mix_norm_pre", "w_in",
    "gdn_conv_w", "gdn_a_log", "gdn_dt_bias", "gdn_norm_w",
    "nsa_pe_k", "nsa_cmp_k_w1", "nsa_cmp_k_w2", "nsa_pe_v", "nsa_cmp_v_w1", "nsa_cmp_v_w2",
    "gla_gate_w", "gla_gate_b", "gla_norm_w",
    "w_branch_gdn", "w_branch_nsa", "w_branch_gla", "w_out", "mix_norm_post",
    "ffn2_norm_pre", "ffn2_w_gate_up", "ffn2_w_down", "ffn2_norm_post",
)


def kernel(x, rel_bias, ffn1_norm_pre, ffn1_w_gate_up, ffn1_w_down, ffn1_norm_post, mix_norm_pre, w_in, gdn_conv_w, gdn_a_log, gdn_dt_bias, gdn_norm_w, nsa_pe_k, nsa_cmp_k_w1, nsa_cmp_k_w2, nsa_pe_v, nsa_cmp_v_w1, nsa_cmp_v_w2, gla_gate_w, gla_gate_b, gla_norm_w, w_branch_gdn, w_branch_nsa, w_branch_gla, w_out, mix_norm_post, ffn2_norm_pre, ffn2_w_gate_up, ffn2_w_down, ffn2_norm_post):
    stacked = dict(zip(_LAYER_PARAMS, (
        ffn1_norm_pre, ffn1_w_gate_up, ffn1_w_down, ffn1_norm_post, mix_norm_pre, w_in,
        gdn_conv_w, gdn_a_log, gdn_dt_bias, gdn_norm_w,
        nsa_pe_k, nsa_cmp_k_w1, nsa_cmp_k_w2, nsa_pe_v, nsa_cmp_v_w1, nsa_cmp_v_w2,
        gla_gate_w, gla_gate_b, gla_norm_w,
        w_branch_gdn, w_branch_nsa, w_branch_gla, w_out, mix_norm_post,
        ffn2_norm_pre, ffn2_w_gate_up, ffn2_w_down, ffn2_norm_post)))
    batch, seq, d = x.shape
    tables = _bias_tables(rel_bias)
    h = x.reshape(batch * seq, d)
    for layer in range(ffn1_norm_pre.shape[0]):
        h = _layer(h, batch, tables, {name: value[layer] for name, value in stacked.items()})
    return h.reshape(batch, seq, d)
```
